```python
import math
import jax, jax.numpy as jnp
from jax import lax
import numpy as np

D_MODEL = 2048
BATCH = 1
SEQ = 8192
DEPTH = 4

N_MIXERS = 3
N_MLSTM_LAYERS = len(range(0, DEPTH, N_MIXERS))
N_RWKV_LAYERS = len(range(1, DEPTH, N_MIXERS))
N_DIFF_LAYERS = len(range(2, DEPTH, N_MIXERS))
NORM_EPS = 1e-6
MLSTM_HEADS = 4
MLSTM_DQK = D_MODEL // 2 // MLSTM_HEADS
MLSTM_DV = D_MODEL // MLSTM_HEADS
MLSTM_CHUNK = 128
GATE_SOFTCAP = 15.0
MLSTM_IN = 2 * MLSTM_HEADS * MLSTM_DQK + 2 * D_MODEL + 2 * MLSTM_HEADS
RWKV_HEAD = 64
RWKV_HEADS = D_MODEL // RWKV_HEAD
DECAY_LORA = 96
AAA_LORA = 96
GATE_LORA = 256
RWKV_GN_EPS = 64e-5
DIFF_HEADS = 8
DIFF_DQK = D_MODEL // DIFF_HEADS // 2
DIFF_DV = 2 * DIFF_DQK
DIFF_IN = 4 * DIFF_HEADS * DIFF_DQK + DIFF_HEADS * DIFF_DV
DIFF_SUBLN_EPS = 1e-5
ROPE_DIM = DIFF_DQK // 4
ROPE_THETA = 500000.0
Q_BLOCK = 128
N_EXPERTS = 32
TOP_K = 4
D_EXPERT = 768
SWIGLU_LIMIT = 7.0
SWIGLU_ALPHA = 1.702
MOE_BLOCK = 128

kernel_name = "hybrid_mlstm_rwkv7_diffattn_moe"


def rms_norm(x, gain, eps=NORM_EPS):
    xf = x.astype(jnp.float32)
    y = xf * lax.rsqrt(jnp.mean(xf * xf, axis=-1, keepdims=True) + eps)
    return (y * gain.astype(jnp.float32)).astype(x.dtype)


def modulate(h, shift, scale):
    return h * (1 + scale[:, None, :]) + shift[:, None, :]


def soft_cap(t):
    return GATE_SOFTCAP * jnp.tanh(t / GATE_SOFTCAP)


def mlstm_mixer(h, w_in, b_gates, norm_gain, w_out):
    B, S, D = h.shape
    H, DK, DV, L = MLSTM_HEADS, MLSTM_DQK, MLSTM_DV, MLSTM_CHUNK
    NC = S // L
    f32 = jnp.float32
    qk = H * DK
    proj = h @ w_in
    o_gate = jax.nn.sigmoid(proj[..., 2 * qk + D:2 * qk + 2 * D])
    gates = soft_cap(proj[..., 2 * qk + 2 * D:].astype(f32) + b_gates.astype(f32))

    def chunked(t, d):
        return t.astype(f32).reshape(B, NC, L, H, d).transpose(0, 3, 1, 2, 4)

    q = chunked(proj[..., :qk], DK) * (DK ** -0.5)
    k = chunked(proj[..., qk:2 * qk], DK)
    v = chunked(proj[..., 2 * qk:2 * qk + D], DV)
    log_i = gates[..., :H].reshape(B, NC, L, H).transpose(0, 3, 1, 2)
    log_f = jax.nn.log_sigmoid(gates[..., H:]).reshape(B, NC, L, H).transpose(0, 3, 1, 2)
    g = jnp.cumsum(log_f, axis=-1)
    g_tot = g[..., -1]

    w_state = g_tot[..., None] - g + log_i
    m_loc = jnp.max(w_state, axis=-1)
    e_state = jnp.exp(w_state - m_loc[..., None])
    kv_chunk = jnp.einsum('bhcl,bhcld,bhcle->bhcde', e_state, k, v)
    n_chunk = jnp.einsum('bhcl,bhcld->bhcd', e_state, k)

    def step(carry, inp):
        C, n, m = carry
        kv_c, n_c, gt, ml = inp
        m_new = jnp.maximum(gt + m, ml)
        a = jnp.exp(gt + m - m_new)
        b = jnp.exp(ml - m_new)
        C_new = a[..., None, None] * C + b[..., None, None] * kv_c
        n_new = a[..., None] * n + b[..., None] * n_c
        return (C_new, n_new, m_new), (C, n, m)

    init = (jnp.zeros((B, H, DK, DV), f32), jnp.zeros((B, H, DK), f32), jnp.zeros((B, H), f32))
    xs = (kv_chunk.transpose(2, 0, 1, 3, 4), n_chunk.transpose(2, 0, 1, 3),
          g_tot.transpose(2, 0, 1), m_loc.transpose(2, 0, 1))
    _, (C_prev, n_prev, m_prev) = lax.scan(step, init, xs)
    C_prev = C_prev.transpose(1, 2, 0, 3, 4)
    n_prev = n_prev.transpose(1, 2, 0, 3)
    m_prev = m_prev.transpose(1, 2, 0)

    causal = jnp.tril(jnp.ones((L, L), dtype=bool))
    d_log = jnp.where(causal, g[..., :, None] - g[..., None, :] + log_i[..., None, :], -jnp.inf)
    inter_log = g + m_prev[..., None]
    m_t = jnp.maximum(inter_log, jnp.max(d_log, axis=-1))
    s = jnp.einsum('bhcid,bhcjd->bhcij', q, k) * jnp.exp(d_log - m_t[..., None])
    inter_scale = jnp.exp(inter_log - m_t)
    num = jnp.einsum('bhcij,bhcje->bhcie', s, v) + inter_scale[..., None] * jnp.einsum('bhcid,bhcde->bhcie', q, C_prev)
    den = jnp.sum(s, axis=-1) + inter_scale * jnp.einsum('bhcid,bhcd->bhci', q, n_prev)
    hh = num / jnp.maximum(jnp.abs(den), jnp.exp(-m_t))[..., None]
    hh = hh.transpose(0, 2, 3, 1, 4).reshape(B, S, H, DV)
    hh = rms_norm(hh, norm_gain.reshape(H, DV)).reshape(B, S, D).astype(h.dtype)
    return (o_gate * hh) @ w_out


def rwkv7_mixer(h, mu, w_rkv, w0, w1, w2, a0, a1, a2, g1, g2, k_k, k_a, r_k, ln_gain, ln_bias, w_o):
    B, S, D = h.shape
    H, N = RWKV_HEADS, RWKV_HEAD
    f32 = jnp.float32
    xx = jnp.pad(h, ((0, 0), (1, 0), (0, 0)))[:, :-1] - h
    mixed = h[:, :, None, :] + xx[:, :, None, :] * mu
    rkv = jnp.einsum('bsjd,jde->bsje', mixed[:, :, :3], w_rkv)
    r, k, v = rkv[:, :, 0], rkv[:, :, 1], rkv[:, :, 2]
    xw, xa, xg = mixed[:, :, 3], mixed[:, :, 4], mixed[:, :, 5]
    w_log = -jax.nn.softplus(-(w0 + jnp.tanh(xw @ w1) @ w2)) - 0.5
    decay = jnp.exp(-jnp.exp(w_log.astype(f32)))
    a = jax.nn.sigmoid(a0 + (xa @ a1) @ a2)
    gate = jax.nn.sigmoid(xg @ g1) @ g2
    kk = (k * k_k).reshape(B, S, H, N).astype(f32)
    kk = kk / jnp.maximum(jnp.linalg.norm(kk, axis=-1, keepdims=True), 1e-12)
    k = k * (1 + (a - 1) * k_a)

    def heads(t):
        return t.astype(f32).reshape(B, S, H, N)

    r_h, k_h, v_h, a_h, w_h = heads(r), heads(k), heads(v), heads(a), heads(decay)

    def step(state, inp):
        r_t, w_t, k_t, v_t, kk_t, a_t = inp
        sa = jnp.einsum('bhvk,bhk->bhv', state, -kk_t)
        state = state * w_t[:, :, None, :] + sa[..., None] * (kk_t * a_t)[:, :, None, :] + v_t[..., None] * k_t[:, :, None, :]
        return state, jnp.einsum('bhvk,bhk->bhv', state, r_t)

    xs = tuple(t.transpose(1, 0, 2, 3) for t in (r_h, w_h, k_h, v_h, kk, a_h))
    _, y = lax.scan(step, jnp.zeros((B, H, N, N), f32), xs)
    y = y.transpose(1, 0, 2, 3)
    mean = jnp.mean(y, axis=-1, keepdims=True)
    var = jnp.mean(jnp.square(y - mean), axis=-1, keepdims=True)
    y = ((y - mean) * lax.rsqrt(var + RWKV_GN_EPS)).reshape(B, S, D) * ln_gain.astype(f32) + ln_bias.astype(f32)
    bonus = jnp.sum(r_h * k_h * r_k.astype(f32), axis=-1, keepdims=True) * v_h
    y = (y + bonus.reshape(B, S, D)).astype(h.dtype)
    return (y * gate) @ w_o


def rope_tables(positions):
    inv_freq = ROPE_THETA ** (-jnp.arange(0, ROPE_DIM, 2, dtype=jnp.float32) / ROPE_DIM)
    ang = positions.astype(jnp.float32)[..., None] * inv_freq
    return jnp.cos(ang)[:, :, None, None, :], jnp.sin(ang)[:, :, None, None, :]


def partial_rope(t, cos, sin):
    half = ROPE_DIM // 2
    tf = t.astype(jnp.float32)
    x1, x2 = tf[..., :half], tf[..., half:ROPE_DIM]
    out = jnp.concatenate([x1 * cos - x2 * sin, x2 * cos + x1 * sin, tf[..., ROPE_DIM:]], axis=-1)
    return out.astype(t.dtype)


def diff_attn_mixer(h, cos, sin, w_qkv, lam_params, subln_gain, w_o, layer):
    B, S, D = h.shape
    H, DK, DV = DIFF_HEADS, DIFF_DQK, DIFF_DV
    proj = h @ w_qkv
    q = partial_rope(proj[..., :2 * H * DK].reshape(B, S, H, 2, DK), cos, sin)
    k = partial_rope(proj[..., 2 * H * DK:4 * H * DK].reshape(B, S, H, 2, DK), cos, sin)
    v = proj[..., 4 * H * DK:].reshape(B, S, H, DV)
    lambda_init = 0.8 - 0.6 * math.exp(-0.3 * layer)
    lp = lam_params.astype(jnp.float32)
    lam = jnp.exp(jnp.sum(lp[0] * lp[1])) - jnp.exp(jnp.sum(lp[2] * lp[3])) + lambda_init
    qh = q.transpose(0, 2, 3, 1, 4)
    kh = k.transpose(0, 2, 3, 1, 4)
    vh = v.transpose(0, 2, 1, 3)
    key_idx = jnp.arange(S)
    scale = DK ** -0.5

    def block(i):
        start = i * Q_BLOCK
        qb = lax.dynamic_slice_in_dim(qh, start, Q_BLOCK, axis=3)
        s = jnp.einsum('bhcqd,bhckd->bhcqk', qb, kh).astype(jnp.float32) * scale
        mask = key_idx[None, :] <= (start + jnp.arange(Q_BLOCK))[:, None]
        p = jax.nn.softmax(jnp.where(mask, s, -jnp.inf), axis=-1)
        attn = p[:, :, 0] - lam * p[:, :, 1]
        return jnp.einsum('bhqk,bhkd->bhqd', attn.astype(vh.dtype), vh)

    out = lax.map(block, jnp.arange(S // Q_BLOCK))
    out = out.transpose(1, 0, 3, 2, 4).reshape(B, S, H, DV)
    out = rms_norm(out, subln_gain, DIFF_SUBLN_EPS) * (1 - lambda_init)
    return out.reshape(B, S, D) @ w_o


def moe_ffn(h, router_w, router_b, w_gu, b_gu, w_down, b_down):
    B, S, D = h.shape
    T = B * S
    E, K, BLK = N_EXPERTS, TOP_K, MOE_BLOCK
    xt = h.reshape(T, D)
    logits = (xt @ router_w + router_b).astype(jnp.float32)
    top_val, top_idx = lax.top_k(logits, K)
    gate_w = jax.nn.softmax(top_val, axis=-1)
    flat_e = top_idx.reshape(-1)
    order = jnp.argsort(flat_e)
    sorted_e = flat_e[order]
    tok = (order // K).astype(jnp.int32)
    counts = jnp.bincount(flat_e, length=E)
    group_start = jnp.cumsum(counts) - counts
    padded = (counts + BLK - 1) // BLK * BLK
    padded_end = jnp.cumsum(padded)
    padded_start = padded_end - padded
    dest = padded_start[sorted_e] + jnp.arange(T * K) - group_start[sorted_e]
    n_rows = T * K + E * BLK
    n_blk = n_rows // BLK
    row_tok = jnp.full((n_rows,), T, jnp.int32).at[dest].set(tok)
    x_pad = jnp.concatenate([xt, jnp.zeros((1, D), xt.dtype)], axis=0)
    xb = x_pad[row_tok].reshape(n_blk, BLK, D)
    blk_expert = jnp.minimum(jnp.searchsorted(padded_end, jnp.arange(n_blk) * BLK, side='right'), E - 1)

    def expert_block(args):
        xblk, e = args
        gu = xblk @ w_gu[e] + b_gu[e]
        g_lin = jnp.minimum(gu[:, 0::2], SWIGLU_LIMIT)
        up = jnp.clip(gu[:, 1::2], -SWIGLU_LIMIT, SWIGLU_LIMIT)
        glu = g_lin * jax.nn.sigmoid(g_lin * SWIGLU_ALPHA)
        return ((up + 1) * glu) @ w_down[e] + b_down[e]

    yb = lax.map(expert_block, (xb, blk_expert)).reshape(n_rows, D)
    w_sorted = gate_w.reshape(-1)[order].astype(yb.dtype)
    y = jax.ops.segment_sum(yb[dest] * w_sorted[:, None], tok, num_segments=T)
    return y.reshape(B, S, D)


def setup_inputs(seed: int = 0) -> dict:
    key = jax.random.key(seed)
    ks = list(jax.random.split(key, 48))
    f32 = jnp.float32
    D = D_MODEL

    def nrm(shape, scale):
        return jax.random.normal(ks.pop(), shape, f32) * scale

    A, Bn, Cn = N_MLSTM_LAYERS, N_RWKV_LAYERS, N_DIFF_LAYERS
    inp = {}
    inp['x'] = nrm((BATCH, SEQ, D), 1.0)
    inp['c'] = nrm((BATCH, D), 1.0)
    offset = jax.random.randint(ks.pop(), (BATCH, 1), 0, 4096, jnp.int32)
    inp['positions'] = (offset + jnp.arange(SEQ, dtype=jnp.int32)[None, :]).astype(jnp.int32)
    inp['norm_gain'] = 1.0 + nrm((DEPTH, 2, D), 0.02)
    inp['ada_w'] = nrm((DEPTH, D, 6 * D), 0.5 * D ** -0.5)
    inp['ada_b'] = nrm((DEPTH, 6 * D), 0.02)
    inp['mlstm_w_in'] = nrm((A, D, MLSTM_IN), D ** -0.5)
    f_bias = jnp.linspace(3.0, 6.0, MLSTM_HEADS, dtype=f32)[None, :]
    inp['mlstm_b_gates'] = jnp.concatenate([nrm((A, MLSTM_HEADS), 0.1), f_bias + nrm((A, MLSTM_HEADS), 0.1)], axis=-1)
    inp['mlstm_norm_gain'] = 1.0 + nrm((A, D), 0.02)
    inp['mlstm_w_out'] = nrm((A, D, D), D ** -0.5)
    inp['rwkv_mu'] = jax.random.uniform(ks.pop(), (Bn, 6, D), f32)
    inp['rwkv_w_rkv'] = nrm((Bn, 3, D, D), D ** -0.5)
    inp['rwkv_w0'] = jnp.linspace(-6.5, -1.5, D, dtype=f32)[None, :] + nrm((Bn, D), 0.1)
    inp['rwkv_w1'] = nrm((Bn, D, DECAY_LORA), D ** -0.5)
    inp['rwkv_w2'] = nrm((Bn, DECAY_LORA, D), 0.1 * DECAY_LORA ** -0.5)
    inp['rwkv_a0'] = nrm((Bn, D), 0.1)
    inp['rwkv_a1'] = nrm((Bn, D, AAA_LORA), D ** -0.5)
    inp['rwkv_a2'] = nrm((Bn, AAA_LORA, D), 0.1 * AAA_LORA ** -0.5)
    inp['rwkv_g1'] = nrm((Bn, D, GATE_LORA), D ** -0.5)
    inp['rwkv_g2'] = nrm((Bn, GATE_LORA, D), GATE_LORA ** -0.5)
    inp['rwkv_k_k'] = 0.85 + nrm((Bn, D), 0.02)
    inp['rwkv_k_a'] = 1.0 + nrm((Bn, D), 0.02)
    inp['rwkv_r_k'] = nrm((Bn, RWKV_HEADS, RWKV_HEAD), 0.1)
    inp['rwkv_ln_gain'] = 1.0 + nrm((Bn, D), 0.02)
    inp['rwkv_ln_bias'] = nrm((Bn, D), 0.02)
    inp['rwkv_w_o'] = nrm((Bn, D, D), D ** -0.5)
    inp['diff_w_qkv'] = nrm((Cn, D, DIFF_IN), D ** -0.5)
    inp['diff_lambda'] = nrm((Cn, 4, DIFF_DQK), 0.1)
    inp['diff_subln_gain'] = 1.0 + nrm((Cn, DIFF_DV), 0.02)
    inp['diff_w_o'] = nrm((Cn, D, D), D ** -0.5)
    inp['moe_router_w'] = nrm((DEPTH, D, N_EXPERTS), D ** -0.5)
    inp['moe_router_b'] = nrm((DEPTH, N_EXPERTS), 0.01)
    inp['moe_w_gate_up'] = nrm((DEPTH, N_EXPERTS, D, 2 * D_EXPERT), D ** -0.5)
    inp['moe_b_gate_up'] = nrm((DEPTH, N_EXPERTS, 2 * D_EXPERT), 0.02)
    inp['moe_w_down'] = nrm((DEPTH, N_EXPERTS, D_EXPERT, D), D_EXPERT ** -0.5)
    inp['moe_b_down'] = nrm((DEPTH, N_EXPERTS, D), 0.02)
    inp['final_gain'] = 1.0 + nrm((D,), 0.02)
    return inp


def reference(x, c, positions, norm_gain, ada_w, ada_b, mlstm_w_in, mlstm_b_gates, mlstm_norm_gain, mlstm_w_out,
              rwkv_mu, rwkv_w_rkv, rwkv_w0, rwkv_w1, rwkv_w2, rwkv_a0, rwkv_a1, rwkv_a2, rwkv_g1, rwkv_g2,
              rwkv_k_k, rwkv_k_a, rwkv_r_k, rwkv_ln_gain, rwkv_ln_bias, rwkv_w_o,
              diff_w_qkv, diff_lambda, diff_subln_gain, diff_w_o,
              moe_router_w, moe_router_b, moe_w_gate_up, moe_b_gate_up, moe_w_down, moe_b_down, final_gain):
    cos, sin = rope_tables(positions)
    mod = jnp.einsum('bd,lde->lbe', jax.nn.silu(c), ada_w) + ada_b[:, None, :]
    seen = [0, 0, 0]
    for layer in range(DEPTH):
        kind = layer % N_MIXERS
        j = seen[kind]
        seen[kind] += 1
        sh1, sc1, g1, sh2, sc2, g2 = jnp.split(mod[layer], 6, axis=-1)
        hm = modulate(rms_norm(x, norm_gain[layer, 0]), sh1, sc1)
        if kind == 0:
            y = mlstm_mixer(hm, mlstm_w_in[j], mlstm_b_gates[j], mlstm_norm_gain[j], mlstm_w_out[j])
        elif kind == 1:
            y = rwkv7_mixer(hm, rwkv_mu[j], rwkv_w_rkv[j], rwkv_w0[j], rwkv_w1[j], rwkv_w2[j], rwkv_a0[j],
                            rwkv_a1[j], rwkv_a2[j], rwkv_g1[j], rwkv_g2[j], rwkv_k_k[j], rwkv_k_a[j],
                            rwkv_r_k[j], rwkv_ln_gain[j], rwkv_ln_bias[j], rwkv_w_o[j])
        else:
            y = diff_attn_mixer(hm, cos, sin, diff_w_qkv[j], diff_lambda[j], diff_subln_gain[j], diff_w_o[j], layer)
        x = x + g1[:, None, :] * y
        hf = modulate(rms_norm(x, norm_gain[layer, 1]), sh2, sc2)
        x = x + g2[:, None, :] * moe_ffn(hf, moe_router_w[layer], moe_router_b[layer], moe_w_gate_up[layer],
                                         moe_b_gate_up[layer], moe_w_down[layer], moe_b_down[layer])
    return rms_norm(x, final_gain)
```

```python
import functools
import math

import numpy as np
import jax
import jax.numpy as jnp
from jax import lax
from jax.experimental import pallas as pl
from jax.experimental.pallas import tpu as pltpu

F32 = jnp.float32
BF16 = jnp.bfloat16
HIGHEST = lax.Precision.HIGHEST

LANES = 128
VMEM_LIMIT = 52 * 1024 * 1024

NORM_EPS = 1e-6
MLSTM_HEADS = 4
MLSTM_CHUNK = 128
GATE_SOFTCAP = 15.0
RWKV_HEAD = 64
RWKV_GN_EPS = 64e-5
DIFF_HEADS = 8
DIFF_DQK = 128
DIFF_DV = 256
DIFF_SUBLN_EPS = 1e-5
ROPE_DIM = 32
ROPE_THETA = 500000.0
N_EXPERTS = 32
TOP_K = 4
SWIGLU_LIMIT = 7.0
SWIGLU_ALPHA = 1.702
MOE_ROWS = 256
TOK_BLOCK = 256


def _params(*sem):
    return pltpu.CompilerParams(dimension_semantics=sem, vmem_limit_bytes=VMEM_LIMIT)


def _bdot(a, b):
    return jnp.dot(a.astype(BF16), b.astype(BF16), preferred_element_type=F32)


def _dot_nt(a, b):
    return lax.dot_general(a.astype(BF16), b.astype(BF16), (((1,), (1,)), ((), ())),
                           preferred_element_type=F32)


def _dot_tn(a, b):
    return lax.dot_general(a.astype(BF16), b.astype(BF16), (((0,), (0,)), ((), ())),
                           preferred_element_type=F32)


def _ada_body(c_ref, w_ref, b_ref, o_ref, *, kc):
    d = w_ref.shape[1]

    def step(i, acc):
        cv = c_ref[pl.ds(i * kc, kc), :]
        sc = cv * jax.nn.sigmoid(cv)
        return acc + jnp.sum(sc * w_ref[0, pl.ds(i * kc, kc), :], axis=0, keepdims=True)

    acc = lax.fori_loop(0, d // kc, step, jnp.zeros((1, w_ref.shape[2]), F32))
    o_ref[0] = acc + b_ref[0]


def ada_mod(c, ada_w, ada_b, tn=1024, kc=128):
    depth, d, n = ada_w.shape
    return pl.pallas_call(
        functools.partial(_ada_body, kc=kc),
        grid=(depth, n // tn),
        in_specs=[pl.BlockSpec((d, 1), lambda l, j: (0, 0)),
                  pl.BlockSpec((1, d, tn), lambda l, j: (l, 0, j)),
                  pl.BlockSpec((1, 1, tn), lambda l, j: (l, 0, j))],
        out_specs=pl.BlockSpec((1, 1, tn), lambda l, j: (l, 0, j)),
        out_shape=jax.ShapeDtypeStruct((depth, 1, n), F32),
        compiler_params=_params("arbitrary", "arbitrary"),
        name="ada_mod",
    )(c.reshape(d, 1), ada_w, ada_b.reshape(depth, 1, n))


def _norm_rows(x, gain, shift, scale, eps):
    y = x * lax.rsqrt(jnp.mean(x * x, axis=-1, keepdims=True) + eps)
    return (y * gain) * (1.0 + scale) + shift


def _norm_mod_body(x_ref, g_ref, sh_ref, sc_ref, o_ref):
    o_ref[...] = _norm_rows(x_ref[...], g_ref[...], sh_ref[...], sc_ref[...], NORM_EPS).astype(o_ref.dtype)


def norm_mod(x, gain, shift, scale, out_dtype, tm=256):
    s, d = x.shape
    row = pl.BlockSpec((1, d), lambda i: (0, 0))
    return pl.pallas_call(
        _norm_mod_body,
        grid=(s // tm,),
        in_specs=[pl.BlockSpec((tm, d), lambda i: (i, 0)), row, row, row],
        out_specs=pl.BlockSpec((tm, d), lambda i: (i, 0)),
        out_shape=jax.ShapeDtypeStruct((s, d), out_dtype),
        compiler_params=_params("arbitrary"),
        name="norm_mod",
    )(x, gain.reshape(1, d), shift.reshape(1, d), scale.reshape(1, d))


def _mm_body(*refs, fused):
    if fused:
        x_ref, w_ref, res_ref, gate_ref, o_ref, wb_ref = refs
    else:
        x_ref, w_ref, o_ref, wb_ref = refs

    @pl.when(pl.program_id(1) == 0)
    def _():
        wb_ref[...] = w_ref[...].astype(BF16)

    acc = jnp.dot(x_ref[...].astype(BF16), wb_ref[...], preferred_element_type=F32)
    if fused:
        acc = res_ref[...] + gate_ref[...] * acc
    o_ref[...] = acc.astype(o_ref.dtype)


def matmul(x, w, out_dtype=F32, res=None, gate=None, tm=512, tn=1024):
    m, k = x.shape
    n = w.shape[1]
    tn = min(tn, n)
    tm = min(tm, m)
    fused = res is not None
    in_specs = [pl.BlockSpec((tm, k), lambda j, i: (i, 0)),
                pl.BlockSpec((k, tn), lambda j, i: (0, j))]
    args = [x, w]
    if fused:
        in_specs += [pl.BlockSpec((tm, tn), lambda j, i: (i, j)),
                     pl.BlockSpec((1, tn), lambda j, i: (0, j))]
        args += [res, gate.reshape(1, n)]
    return pl.pallas_call(
        functools.partial(_mm_body, fused=fused),
        grid=(n // tn, m // tm),
        in_specs=in_specs,
        out_specs=pl.BlockSpec((tm, tn), lambda j, i: (i, j)),
        out_shape=jax.ShapeDtypeStruct((m, n), out_dtype),
        scratch_shapes=[pltpu.VMEM((k, tn), BF16)],
        compiler_params=_params("arbitrary", "arbitrary"),
        name="matmul",
    )(*args)


def _mlstm_body(q_ref, k_ref, v_ref, o_ref, gt_ref, bg_ref, ng_ref, out_ref, c_s, n_s, m_s):
    h = pl.program_id(0)
    L = q_ref.shape[0]
    dk = q_ref.shape[1]
    dv = v_ref.shape[1]

    @pl.when(pl.program_id(1) == 0)
    def _():
        c_s[...] = jnp.zeros_like(c_s)
        n_s[...] = jnp.zeros_like(n_s)
        m_s[...] = jnp.zeros_like(m_s)

    lane = lax.broadcasted_iota(jnp.int32, (L, LANES), 1)
    row = lax.broadcasted_iota(jnp.int32, (L, L), 0)
    col = lax.broadcasted_iota(jnp.int32, (L, L), 1)

    capped = GATE_SOFTCAP * jnp.tanh((gt_ref[...] + bg_ref[...]) / GATE_SOFTCAP)
    li = jnp.sum(jnp.where(lane == h, capped, 0.0), axis=-1, keepdims=True)
    fpre = jnp.sum(jnp.where(lane == h + MLSTM_HEADS, capped, 0.0), axis=-1, keepdims=True)
    lf = jax.nn.log_sigmoid(fpre)

    tril = (row >= col).astype(F32)
    g_b = jnp.dot(tril, jnp.broadcast_to(lf, (L, L)), precision=HIGHEST,
                  preferred_element_type=F32)
    g = g_b[:, 0:1]
    g_tot = jnp.sum(lf, axis=0, keepdims=True)
    eye = row == col
    g_row = jnp.sum(jnp.where(eye, g_b, 0.0), axis=0, keepdims=True)
    li_row = jnp.sum(jnp.where(eye, jnp.broadcast_to(li, (L, L)), 0.0), axis=0, keepdims=True)

    m_prev = m_s[...]
    c_prev = c_s[...]
    n_prev = n_s[...]

    q = q_ref[...] * (dk ** -0.5)
    k = k_ref[...]
    vb = v_ref[...].astype(BF16)
    qb = q.astype(BF16)

    d_log = jnp.where(row >= col, g_b - g_row + li_row, -jnp.inf)
    inter_log = g + m_prev
    m_t = jnp.maximum(inter_log, jnp.max(d_log, axis=-1, keepdims=True))
    s = _dot_nt(qb, k) * jnp.exp(d_log - m_t)
    inter = jnp.exp(inter_log - m_t)
    num = _bdot(s, vb) + inter * _bdot(qb, c_prev)
    den = jnp.sum(s, axis=-1, keepdims=True) + inter * jnp.sum(q * n_prev, axis=-1, keepdims=True)
    hh = num / jnp.maximum(jnp.abs(den), jnp.exp(-m_t))
    hh = hh * lax.rsqrt(jnp.mean(hh * hh, axis=-1, keepdims=True) + NORM_EPS) * ng_ref[...]
    out_ref[...] = (jax.nn.sigmoid(o_ref[...]) * hh).astype(out_ref.dtype)

    w_state = g_tot - g + li
    m_loc = jnp.max(w_state, axis=0, keepdims=True)
    ks = jnp.exp(w_state - m_loc) * k
    kv = _dot_tn(ks, vb)
    n_c = jnp.sum(ks, axis=0, keepdims=True)
    m_new = jnp.maximum(g_tot + m_prev, m_loc)
    a = jnp.exp(g_tot + m_prev - m_new)
    b = jnp.exp(m_loc - m_new)
    c_s[...] = a * c_prev + b * kv
    n_s[...] = a * n_prev + b * n_c
    m_s[...] = m_new


def mlstm_cells(proj, gates, b_gates_pad, norm_gain):
    s = proj.shape[0]
    d = norm_gain.shape[0]
    H, L = MLSTM_HEADS, MLSTM_CHUNK
    dk = (proj.shape[1] - 2 * d) // (2 * H)
    dv = d // H
    kb, vb, ob = H, (2 * H * dk) // dv, (2 * H * dk + d) // dv
    return pl.pallas_call(
        _mlstm_body,
        grid=(H, s // L),
        in_specs=[pl.BlockSpec((L, dk), lambda h, c: (c, h)),
                  pl.BlockSpec((L, dk), lambda h, c: (c, kb + h)),
                  pl.BlockSpec((L, dv), lambda h, c: (c, vb + h)),
                  pl.BlockSpec((L, dv), lambda h, c: (c, ob + h)),
                  pl.BlockSpec((L, LANES), lambda h, c: (c, 0)),
                  pl.BlockSpec((1, LANES), lambda h, c: (0, 0)),
                  pl.BlockSpec((1, dv), lambda h, c: (0, h))],
        out_specs=pl.BlockSpec((L, dv), lambda h, c: (c, h)),
        out_shape=jax.ShapeDtypeStruct((s, d), BF16),
        scratch_shapes=[pltpu.VMEM((dk, dv), F32), pltpu.VMEM((1, dk), F32), pltpu.VMEM((1, 1), F32)],
        compiler_params=_params("arbitrary", "arbitrary"),
        name="mlstm_cells",
    )(proj, proj, proj, proj, gates, b_gates_pad, norm_gain.reshape(1, d))


def mlstm_mixer(x, gain, shift, scale, w_in, b_gates, norm_gain, w_out, gate_row):
    d = x.shape[1]
    n_main = w_in.shape[1] - 2 * MLSTM_HEADS
    hm = norm_mod(x, gain, shift, scale, BF16)
    proj = matmul(hm, w_in[:, :n_main])
    w_g = jnp.pad(w_in[:, n_main:], ((0, 0), (0, LANES - 2 * MLSTM_HEADS)))
    gates = matmul(hm, w_g)
    b_pad = jnp.pad(b_gates, (0, LANES - 2 * MLSTM_HEADS)).reshape(1, LANES)
    cells = mlstm_cells(proj, gates, b_pad, norm_gain)
    return matmul(cells, w_out, res=x, gate=gate_row)


def _rwkv_prep_body(x_ref, xp_ref, g_ref, sh_ref, sc_ref, mu_ref, *outs):
    tm = x_ref.shape[0]
    hm = _norm_rows(x_ref[...], g_ref[...], sh_ref[...], sc_ref[...], NORM_EPS)
    prev = _norm_rows(xp_ref[7:8, :], g_ref[...], sh_ref[...], sc_ref[...], NORM_EPS)
    prev = jnp.where(pl.program_id(0) == 0, 0.0, prev)
    rows = lax.broadcasted_iota(jnp.int32, hm.shape, 0)
    shifted = jnp.where(rows == 0, prev, pltpu.roll(hm, 1, axis=0))
    xx = shifted - hm
    for j, o_ref in enumerate(outs):
        o_ref[...] = (hm + xx * mu_ref[j:j + 1, :]).astype(o_ref.dtype)


def rwkv_prep(x, gain, shift, scale, mu, tm=256):
    s, d = x.shape
    row = pl.BlockSpec((1, d), lambda i: (0, 0))
    blk = pl.BlockSpec((tm, d), lambda i: (i, 0))
    return pl.pallas_call(
        _rwkv_prep_body,
        grid=(s // tm,),
        in_specs=[blk, pl.BlockSpec((8, d), lambda i: (jnp.maximum(i * (tm // 8) - 1, 0), 0)),
                  row, row, row, pl.BlockSpec((8, d), lambda i: (0, 0))],
        out_specs=[blk] * 6,
        out_shape=[jax.ShapeDtypeStruct((s, d), BF16)] * 6,
        compiler_params=_params("arbitrary"),
        name="rwkv_prep",
    )(x, x, gain.reshape(1, d), shift.reshape(1, d), scale.reshape(1, d), jnp.pad(mu, ((0, 2), (0, 0))))


def _lora_body(x_ref, w1_ref, w2_ref, o_ref, *, act):
    t = _bdot(x_ref[...], w1_ref[...])
    if act == "tanh":
        t = jnp.tanh(t)
    elif act == "sigmoid":
        t = jax.nn.sigmoid(t)
    o_ref[...] = _bdot(t, w2_ref[...])


def lora(x, w1, w2, act, tm=512):
    s, d = x.shape
    r = w1.shape[1]
    rp = -(-r // LANES) * LANES
    w1p = jnp.pad(w1, ((0, 0), (0, rp - r)))
    w2p = jnp.pad(w2, ((0, rp - r), (0, 0)))
    n = w2.shape[1]
    return pl.pallas_call(
        functools.partial(_lora_body, act=act),
        grid=(s // tm,),
        in_specs=[pl.BlockSpec((tm, d), lambda i: (i, 0)),
                  pl.BlockSpec((d, rp), lambda i: (0, 0)),
                  pl.BlockSpec((rp, n), lambda i: (0, 0))],
        out_specs=pl.BlockSpec((tm, n), lambda i: (i, 0)),
        out_shape=jax.ShapeDtypeStruct((s, n), F32),
        compiler_params=_params("arbitrary"),
        name="lora",
    )(x, w1p, w2p)


def _split3(x):
    hi = x.astype(BF16)
    r1 = x - hi.astype(F32)
    mid = r1.astype(BF16)
    lo = (r1 - mid.astype(F32)).astype(BF16)
    return hi, mid, lo


def _segsum(x, bmat, passes):
    parts = _split3(x)[:passes] if passes > 1 else (x.astype(BF16),)
    acc = jnp.dot(parts[0], bmat, preferred_element_type=F32)
    for p in parts[1:]:
        acc = acc + jnp.dot(p, bmat, preferred_element_type=F32)
    return acc


SUBLANES = 8


def _rwkv_scan_body(r_ref, k_ref, v_ref, wl_ref, al_ref, gt_ref, w0_ref, a0_ref, kk_ref, ka_ref, rk_ref,
                    lg_ref, lb_ref, o_ref, st_s, nkk_s, w_s, kka_s, kp_s, y_s, b_s, m1_s,
                    nkk3, w3, kka3, kp3, r3, v3, y3):
    tb, d = r_ref.shape
    nv = RWKV_HEAD
    ntile = d // LANES
    ngrp = tb // SUBLANES

    @pl.when(pl.program_id(0) == 0)
    def _():
        st_s[...] = jnp.zeros_like(st_s)
        ri = lax.broadcasted_iota(jnp.int32, (LANES, LANES), 0)
        ci = lax.broadcasted_iota(jnp.int32, (LANES, LANES), 1)
        b_s[...] = ((ri // nv) == (ci // nv)).astype(BF16)
        vi = lax.broadcasted_iota(jnp.int32, (nv, LANES), 0)
        li = lax.broadcasted_iota(jnp.int32, (nv, LANES), 1)
        m1_s[...] = (vi == (li % nv)).astype(F32)

    for j in range(ntile):
        cs = slice(j * LANES, (j + 1) * LANES)
        bmat = b_s[...]
        kraw = k_ref[:, cs]
        w_log = -jax.nn.softplus(-(w0_ref[:, cs] + wl_ref[:, cs])) - 0.5
        w_s[:, cs] = jnp.exp(-jnp.exp(w_log))
        a = jax.nn.sigmoid(a0_ref[:, cs] + al_ref[:, cs])
        kk = kraw * kk_ref[:, cs]
        nrm = jnp.sqrt(_segsum(kk * kk, bmat, 3))
        kk = kk / jnp.maximum(nrm, 1e-12)
        nkk_s[:, cs] = -kk
        kka_s[:, cs] = kk * a
        kp_s[:, cs] = kraw * (1.0 + (a - 1.0) * ka_ref[:, cs])

    def replicate(g, carry):
        base = pl.multiple_of(g * SUBLANES, SUBLANES)
        for src, dst in ((nkk_s, nkk3), (w_s, w3), (kka_s, kka3), (kp_s, kp3), (r_ref, r3), (v_ref, v3)):
            for j in range(ntile):
                cs = slice(j * LANES, (j + 1) * LANES)
                grp = src[pl.ds(base, SUBLANES), cs]
                for i in range(SUBLANES):
                    dst[base + i, :, cs] = jnp.broadcast_to(grp[i:i + 1, :], (SUBLANES, LANES))
        return carry

    lax.fori_loop(0, ngrp, replicate, 0)

    def tall(ref, t, cs):
        return jnp.concatenate([ref[t, :, cs]] * (nv // SUBLANES), axis=0)

    def step(t, carry):
        for j in range(ntile):
            cs = slice(j * LANES, (j + 1) * LANES)
            bmat = b_s[...]
            m1 = m1_s[...]
            st = st_s[:, cs]
            sa = _segsum(st * tall(nkk3, t, cs), bmat, 2)
            vcol = _segsum(m1 * tall(v3, t, cs), bmat, 1)
            sn = st * tall(w3, t, cs) + sa * tall(kka3, t, cs) + vcol * tall(kp3, t, cs)
            st_s[:, cs] = sn
            part = m1 * _segsum(sn * tall(r3, t, cs), bmat, 1)
            acc = part[0:SUBLANES]
            for i in range(1, nv // SUBLANES):
                acc = acc + part[i * SUBLANES:(i + 1) * SUBLANES]
            y3[t, :, cs] = acc
        return carry

    lax.fori_loop(0, tb, step, 0)

    def collect(g, carry):
        base = pl.multiple_of(g * SUBLANES, SUBLANES)
        for j in range(ntile):
            cs = slice(j * LANES, (j + 1) * LANES)
            rows = [jnp.sum(y3[base + i, :, cs], axis=0, keepdims=True) for i in range(SUBLANES)]
            y_s[pl.ds(base, SUBLANES), cs] = jnp.concatenate(rows, axis=0)
        return carry

    lax.fori_loop(0, ngrp, collect, 0)

    inv = 1.0 / nv
    for j in range(ntile):
        cs = slice(j * LANES, (j + 1) * LANES)
        bmat = b_s[...]
        y = y_s[:, cs]
        mean = _segsum(y, bmat, 3) * inv
        yc = y - mean
        var = _segsum(yc * yc, bmat, 3) * inv
        yn = yc * lax.rsqrt(var + RWKV_GN_EPS) * lg_ref[:, cs] + lb_ref[:, cs]
        bonus = _segsum(r_ref[:, cs] * kp_s[:, cs] * rk_ref[:, cs], bmat, 3) * v_ref[:, cs]
        o_ref[:, cs] = ((yn + bonus) * gt_ref[:, cs]).astype(o_ref.dtype)


def rwkv_scan(r, k, v, wl, al, gt, w0, a0, k_k, k_a, r_k, ln_gain, ln_bias, tb=32):
    s, d = r.shape
    blk = pl.BlockSpec((tb, d), lambda i: (i, 0))
    row = pl.BlockSpec((1, d), lambda i: (0, 0))
    rows = [t.reshape(1, d) for t in (w0, a0, k_k, k_a, r_k, ln_gain, ln_bias)]
    return pl.pallas_call(
        _rwkv_scan_body,
        grid=(s // tb,),
        in_specs=[blk] * 6 + [row] * 7,
        out_specs=blk,
        out_shape=jax.ShapeDtypeStruct((s, d), BF16),
        scratch_shapes=[pltpu.VMEM((RWKV_HEAD, d), F32)] + [pltpu.VMEM((tb, d), F32)] * 5
                       + [pltpu.VMEM((LANES, LANES), BF16), pltpu.VMEM((RWKV_HEAD, LANES), F32)]
                       + [pltpu.VMEM((tb, SUBLANES, d), F32)] * 7,
        compiler_params=_params("arbitrary"),
        name="rwkv_scan",
    )(r, k, v, wl, al, gt, *rows)


def rwkv_mixer(x, gain, shift, scale, mu, w_rkv, w0, w1, w2, a0, a1, a2, g1, g2, k_k, k_a, r_k,
               ln_gain, ln_bias, w_o, gate_row):
    xr, xk, xv, xw, xa, xg = rwkv_prep(x, gain, shift, scale, mu)
    r = matmul(xr, w_rkv[0])
    k = matmul(xk, w_rkv[1])
    v = matmul(xv, w_rkv[2])
    wl = lora(xw, w1, w2, "tanh")
    al = lora(xa, a1, a2, "none")
    gt = lora(xg, g1, g2, "sigmoid")
    y = rwkv_scan(r, k, v, wl, al, gt, w0, a0, k_k, k_a, r_k.reshape(-1), ln_gain, ln_bias)
    return matmul(y, w_o, res=x, gate=gate_row)


def _rope_body(p_ref, pos_ref, o_ref, cos_s, sin_s, *, n_rot_tiles, q_tiles):
    j = pl.program_id(1)
    x = p_ref[...]
    half = ROPE_DIM // 2

    @pl.when(j == 0)
    def _():
        lane = lax.broadcasted_iota(jnp.int32, x.shape, 1)
        fidx = (lane % half).astype(F32)
        inv_freq = jnp.exp(fidx * (-2.0 * math.log(ROPE_THETA) / ROPE_DIM))
        ang = pos_ref[...].astype(F32) * inv_freq
        cos_s[...] = jnp.cos(ang)
        sin_s[...] = jnp.sin(ang)

    @pl.when(j < n_rot_tiles)
    def _():
        lane = lax.broadcasted_iota(jnp.int32, x.shape, 1)
        cos, sin = cos_s[...], sin_s[...]
        partner = jnp.where(lane < half, pltpu.roll(x, LANES - half, axis=1), pltpu.roll(x, half, axis=1))
        rot = jnp.where(lane < half, x * cos - partner * sin, x * cos + partner * sin)
        y = jnp.where(lane < ROPE_DIM, rot, x)
        y = jnp.where(j < q_tiles, y * (DIFF_DQK ** -0.5), y)
        o_ref[...] = y.astype(o_ref.dtype)

    @pl.when(j >= n_rot_tiles)
    def _():
        o_ref[...] = x.astype(o_ref.dtype)


def rope_cast(proj, positions, tm=512):
    s, n = proj.shape
    q_tiles = 2 * DIFF_HEADS
    return pl.pallas_call(
        functools.partial(_rope_body, n_rot_tiles=2 * q_tiles, q_tiles=q_tiles),
        grid=(s // tm, n // LANES),
        in_specs=[pl.BlockSpec((tm, LANES), lambda i, j: (i, j)),
                  pl.BlockSpec((tm, 1), lambda i, j: (i, 0))],
        out_specs=pl.BlockSpec((tm, LANES), lambda i, j: (i, j)),
        out_shape=jax.ShapeDtypeStruct((s, n), BF16),
        scratch_shapes=[pltpu.VMEM((tm, LANES), F32), pltpu.VMEM((tm, LANES), F32)],
        compiler_params=_params("arbitrary", "arbitrary"),
        name="rope_cast",
    )(proj, positions.reshape(s, 1))


def _diff_attn_body(q_ref, k_ref, v_ref, lam_ref, sg_ref, o_ref, m_s, l_s, acc_s, *, lambda_init):
    qi, kj = pl.program_id(1), pl.program_id(2)
    tq, tk = q_ref.shape[0], k_ref.shape[0]

    @pl.when(kj == 0)
    def _():
        m_s[...] = jnp.full_like(m_s, -jnp.inf)
        l_s[...] = jnp.zeros_like(l_s)
        acc_s[...] = jnp.zeros_like(acc_s)

    @pl.when(kj * tk < (qi + 1) * tq)
    def _():
        rows = qi * tq + lax.broadcasted_iota(jnp.int32, (tq, tk), 0)
        cols = kj * tk + lax.broadcasted_iota(jnp.int32, (tq, tk), 1)
        mask = cols <= rows
        v = v_ref[...]
        for c in range(2):
            cs = slice(c * DIFF_DQK, (c + 1) * DIFF_DQK)
            s = jnp.where(mask, _dot_nt(q_ref[:, cs], k_ref[:, cs]), -jnp.inf)
            m_old = m_s[c]
            m_new = jnp.maximum(m_old, jnp.max(s, axis=-1, keepdims=True))
            alpha = jnp.exp(m_old - m_new)
            p = jnp.exp(s - m_new)
            l_s[c] = alpha * l_s[c] + jnp.sum(p, axis=-1, keepdims=True)
            acc_s[c] = alpha * acc_s[c] + _bdot(p, v)
            m_s[c] = m_new

    @pl.when(kj == pl.num_programs(2) - 1)
    def _():
        lp = lam_ref[...]
        lam = (jnp.exp(jnp.sum(lp[0:1] * lp[1:2], axis=-1, keepdims=True))
               - jnp.exp(jnp.sum(lp[2:3] * lp[3:4], axis=-1, keepdims=True)) + lambda_init)
        out = acc_s[0] / l_s[0] - lam * (acc_s[1] / l_s[1])
        out = out * lax.rsqrt(jnp.mean(out * out, axis=-1, keepdims=True) + DIFF_SUBLN_EPS)
        o_ref[...] = (out * sg_ref[...] * (1.0 - lambda_init)).astype(o_ref.dtype)


def diff_attention(qkv, lam_params, subln_gain, lambda_init, tq=512, tk=512):
    s = qkv.shape[0]
    H, dv = DIFF_HEADS, DIFF_DV
    nq, nk = s // tq, s // tk

    def kv_idx(qi, kj):
        return jnp.minimum(kj, ((qi + 1) * tq - 1) // tk)

    return pl.pallas_call(
        functools.partial(_diff_attn_body, lambda_init=lambda_init),
        grid=(H, nq, nk),
        in_specs=[pl.BlockSpec((tq, dv), lambda h, qi, kj: (qi, h)),
                  pl.BlockSpec((tk, dv), lambda h, qi, kj: (kv_idx(qi, kj), H + h)),
                  pl.BlockSpec((tk, dv), lambda h, qi, kj: (kv_idx(qi, kj), 2 * H + h)),
                  pl.BlockSpec((4, DIFF_DQK), lambda h, qi, kj: (0, 0)),
                  pl.BlockSpec((1, dv), lambda h, qi, kj: (0, 0))],
        out_specs=pl.BlockSpec((tq, dv), lambda h, qi, kj: (qi, h)),
        out_shape=jax.ShapeDtypeStruct((s, H * dv), BF16),
        scratch_shapes=[pltpu.VMEM((2, tq, 1), F32), pltpu.VMEM((2, tq, 1), F32),
                        pltpu.VMEM((2, tq, dv), F32)],
        compiler_params=_params("arbitrary", "arbitrary", "arbitrary"),
        name="diff_attention",
    )(qkv, qkv, qkv, lam_params, subln_gain.reshape(1, dv))


def diff_mixer(x, gain, shift, scale, positions, w_qkv, lam_params, subln_gain, w_o, layer, gate_row):
    hm = norm_mod(x, gain, shift, scale, BF16)
    proj = matmul(hm, w_qkv)
    qkv = rope_cast(proj, positions)
    lambda_init = 0.8 - 0.6 * math.exp(-0.3 * layer)
    att = diff_attention(qkv, lam_params, subln_gain, lambda_init)
    return matmul(att, w_o, res=x, gate=gate_row)


def _router_body(x_ref, g_ref, sh_ref, sc_ref, rw_ref, rb_ref, hf_ref, lg_ref):
    hf = _norm_rows(x_ref[...], g_ref[...], sh_ref[...], sc_ref[...], NORM_EPS)
    hf_ref[...] = hf
    lg_ref[...] = jnp.dot(hf, rw_ref[...], precision=HIGHEST, preferred_element_type=F32) + rb_ref[...]


def router(x, gain, shift, scale, router_w, router_b, tm=256):
    s, d = x.shape
    e = router_w.shape[1]
    rw = jnp.pad(router_w, ((0, 0), (0, LANES - e)))
    rb = jnp.pad(router_b, (0, LANES - e), constant_values=-1e30).reshape(1, LANES)
    row = pl.BlockSpec((1, d), lambda i: (0, 0))
    return pl.pallas_call(
        _router_body,
        grid=(s // tm,),
        in_specs=[pl.BlockSpec((tm, d), lambda i: (i, 0)), row, row, row,
                  pl.BlockSpec((d, LANES), lambda i: (0, 0)), pl.BlockSpec((1, LANES), lambda i: (0, 0))],
        out_specs=[pl.BlockSpec((tm, d), lambda i: (i, 0)), pl.BlockSpec((tm, LANES), lambda i: (i, 0))],
        out_shape=[jax.ShapeDtypeStruct((s, d), F32), jax.ShapeDtypeStruct((s, LANES), F32)],
        compiler_params=_params("arbitrary"),
        name="router",
    )(x, gain.reshape(1, d), shift.reshape(1, d), scale.reshape(1, d), rw, rb)


def _topk_body(lg_ref, eidx_ref, gw_ref, rk_ref, cnt_ref):
    tm = lg_ref.shape[0]

    @pl.when(pl.program_id(0) == 0)
    def _():
        cnt_ref[...] = jnp.zeros_like(cnt_ref)

    lane = lax.broadcasted_iota(jnp.int32, (tm, LANES), 1).astype(F32)
    work = lg_ref[...]
    vals, idxs, hots = [], [], []
    for _ in range(TOP_K):
        m = jnp.max(work, axis=-1, keepdims=True)
        idx = jnp.min(jnp.where(work == m, lane, float(LANES)), axis=-1, keepdims=True)
        hot = lane == idx
        work = jnp.where(hot, -jnp.inf, work)
        vals.append(m)
        idxs.append(idx)
        hots.append(hot)
    exps = [jnp.exp(v - vals[0]) for v in vals]
    den = exps[0] + exps[1] + exps[2] + exps[3]
    sel = jnp.zeros((tm, LANES), F32)
    for hot in hots:
        sel = sel + hot.astype(F32)
    ri = lax.broadcasted_iota(jnp.int32, (tm, tm), 0)
    ci = lax.broadcasted_iota(jnp.int32, (tm, tm), 1)
    before = jnp.dot((ri > ci).astype(BF16), sel.astype(BF16), preferred_element_type=F32)
    rank = before + cnt_ref[...]
    cnt_ref[...] = cnt_ref[...] + jnp.sum(sel, axis=0, keepdims=True)
    eidx = jnp.zeros((tm, LANES), F32)
    gw = jnp.zeros((tm, LANES), F32)
    rk = jnp.zeros((tm, LANES), F32)
    for kslot in range(TOP_K):
        here = lane == kslot
        eidx = jnp.where(here, idxs[kslot], eidx)
        gw = jnp.where(here, exps[kslot] / den, gw)
        rk = jnp.where(here, jnp.sum(jnp.where(hots[kslot], rank, 0.0), axis=-1, keepdims=True), rk)
    eidx_ref[...] = eidx
    gw_ref[...] = gw
    rk_ref[...] = rk


def topk_route(logits, tm=TOK_BLOCK):
    s = logits.shape[0]
    blk = pl.BlockSpec((tm, LANES), lambda i: (i, 0))
    return pl.pallas_call(
        _topk_body,
        grid=(s // tm,),
        in_specs=[blk],
        out_specs=[blk, blk, blk, pl.BlockSpec((1, LANES), lambda i: (0, 0))],
        out_shape=[jax.ShapeDtypeStruct((s, LANES), F32), jax.ShapeDtypeStruct((s, LANES), F32),
                   jax.ShapeDtypeStruct((s, LANES), F32), jax.ShapeDtypeStruct((1, LANES), F32)],
        compiler_params=_params("arbitrary"),
        name="topk_route",
    )(logits)


def _slot_body(eidx_ref, rk_ref, ps_ref, pos_ref):
    tm = eidx_ref.shape[0]
    lane = lax.broadcasted_iota(jnp.int32, (tm, LANES), 1).astype(F32)
    eidx = eidx_ref[...]
    rk = rk_ref[...]
    pos = jnp.zeros((tm, LANES), F32)
    for kslot in range(TOP_K):
        e_k = jnp.sum(jnp.where(lane == kslot, eidx, 0.0), axis=-1, keepdims=True)
        start = jnp.sum(jnp.where(lane == e_k, ps_ref[...], 0.0), axis=-1, keepdims=True)
        r_k = jnp.sum(jnp.where(lane == kslot, rk, 0.0), axis=-1, keepdims=True)
        pos = jnp.where(lane == kslot, start + r_k, pos)
    pos_ref[...] = pos.astype(jnp.int32)


def slot_positions(eidx, rk, padded_start, tm=TOK_BLOCK):
    s = eidx.shape[0]
    blk = pl.BlockSpec((tm, LANES), lambda i: (i, 0))
    return pl.pallas_call(
        _slot_body,
        grid=(s // tm,),
        in_specs=[blk, blk, pl.BlockSpec((1, LANES), lambda i: (0, 0))],
        out_specs=blk,
        out_shape=jax.ShapeDtypeStruct((s, LANES), jnp.int32),
        compiler_params=_params("arbitrary"),
        name="slot_positions",
    )(eidx, rk, padded_start)


def _dispatch_body(pos_ref, hf_ref, xs_in_ref, xs_ref, sem):
    del xs_in_ref
    tm = hf_ref.shape[0]

    def issue(t, carry):
        for kslot in range(TOP_K):
            p = pos_ref[t * TOP_K + kslot]
            pltpu.make_async_copy(hf_ref.at[pl.ds(t, 1)], xs_ref.at[pl.ds(p, 1)], sem.at[0]).start()
        return carry

    lax.fori_loop(0, tm, issue, 0)

    def drain(t, carry):
        pltpu.make_async_copy(hf_ref.at[pl.ds(0, 1)], xs_ref.at[pl.ds(0, 1)], sem.at[0]).wait()
        return carry

    lax.fori_loop(0, tm * TOP_K, drain, 0)


def dispatch(pos_flat, hf, n_rows, tm=TOK_BLOCK):
    s, d = hf.shape
    return pl.pallas_call(
        _dispatch_body,
        grid=(s // tm,),
        in_specs=[pl.BlockSpec((tm * TOP_K,), lambda i: (i,), memory_space=pltpu.SMEM),
                  pl.BlockSpec((tm, d), lambda i: (i, 0)),
                  pl.BlockSpec(memory_space=pl.ANY)],
        out_specs=pl.BlockSpec(memory_space=pl.ANY),
        out_shape=jax.ShapeDtypeStruct((n_rows, d), hf.dtype),
        scratch_shapes=[pltpu.SemaphoreType.DMA((1,))],
        input_output_aliases={2: 0},
        compiler_params=_params("arbitrary"),
        name="moe_dispatch",
    )(pos_flat, hf, jnp.zeros((n_rows, d), hf.dtype))


def _expert_body(be_ref, tot_ref, xs_ref, wgu_ref, bgu_ref, wdn_ref, bdn_ref, perm_ref, ys_ref):
    i = pl.program_id(0)
    rows = xs_ref.shape[0]
    n_gu = wgu_ref.shape[2]
    grp = perm_ref.shape[0]
    half = grp // 2

    @pl.when(i * rows < tot_ref[0])
    def _():
        gu = _bdot(xs_ref[...], wgu_ref[0]) + bgu_ref[0]
        gub = gu.astype(BF16)
        gs, us = [], []
        for j in range(n_gu // grp):
            pj = jnp.dot(gub[:, j * grp:(j + 1) * grp], perm_ref[...], preferred_element_type=F32)
            gs.append(pj[:, :half])
            us.append(pj[:, half:])
        g = jnp.minimum(jnp.concatenate(gs, axis=1), SWIGLU_LIMIT)
        u = jnp.clip(jnp.concatenate(us, axis=1), -SWIGLU_LIMIT, SWIGLU_LIMIT)
        act = (u + 1.0) * (g * jax.nn.sigmoid(g * SWIGLU_ALPHA))
        ys_ref[...] = _bdot(act, wdn_ref[0]) + bdn_ref[0]

    @pl.when(i * rows >= tot_ref[0])
    def _():
        ys_ref[...] = jnp.zeros_like(ys_ref)


def expert_ffn(blk_expert, total_rows, xs, w_gu, b_gu, w_dn, b_dn, rows=MOE_ROWS):
    n_rows, d = xs.shape
    e, _, n_gu = w_gu.shape
    dh = w_dn.shape[1]
    grp = 2 * LANES
    perm = np.zeros((grp, grp), np.float32)
    perm[2 * np.arange(LANES), np.arange(LANES)] = 1.0
    perm[2 * np.arange(LANES) + 1, LANES + np.arange(LANES)] = 1.0
    grid_spec = pltpu.PrefetchScalarGridSpec(
        num_scalar_prefetch=2,
        grid=(n_rows // rows,),
        in_specs=[pl.BlockSpec((rows, d), lambda i, be, tot: (i, 0)),
                  pl.BlockSpec((1, d, n_gu), lambda i, be, tot: (be[i], 0, 0)),
                  pl.BlockSpec((1, 1, n_gu), lambda i, be, tot: (be[i], 0, 0)),
                  pl.BlockSpec((1, dh, d), lambda i, be, tot: (be[i], 0, 0)),
                  pl.BlockSpec((1, 1, d), lambda i, be, tot: (be[i], 0, 0)),
                  pl.BlockSpec((grp, grp), lambda i, be, tot: (0, 0))],
        out_specs=pl.BlockSpec((rows, d), lambda i, be, tot: (i, 0)),
    )
    return pl.pallas_call(
        _expert_body,
        grid_spec=grid_spec,
        out_shape=jax.ShapeDtypeStruct((n_rows, d), F32),
        compiler_params=_params("arbitrary"),
        name="moe_experts",
    )(blk_expert, total_rows, xs, w_gu, b_gu.reshape(e, 1, n_gu), w_dn, b_dn.reshape(e, 1, d),
      jnp.asarray(perm, BF16))


def _combine_body(pos_ref, x_ref, gw_ref, g2_ref, ys_ref, o_ref, buf, sem):
    tm = x_ref.shape[0]

    def issue(t, carry):
        for kslot in range(TOP_K):
            p = pos_ref[t * TOP_K + kslot]
            pltpu.make_async_copy(ys_ref.at[pl.ds(p, 1)], buf.at[kslot, pl.ds(t, 1)], sem.at[0]).start()
        return carry

    lax.fori_loop(0, tm, issue, 0)

    def drain(t, carry):
        pltpu.make_async_copy(ys_ref.at[pl.ds(0, 1)], buf.at[0, pl.ds(0, 1)], sem.at[0]).wait()
        return carry

    lax.fori_loop(0, tm * TOP_K, drain, 0)

    gw = gw_ref[...]
    acc = gw[:, 0:1] * buf[0]
    for kslot in range(1, TOP_K):
        acc = acc + gw[:, kslot:kslot + 1] * buf[kslot]
    o_ref[...] = x_ref[...] + g2_ref[...] * acc


def combine(pos_flat, x, gw, g2_row, ys, tm=TOK_BLOCK):
    s, d = x.shape
    return pl.pallas_call(
        _combine_body,
        grid=(s // tm,),
        in_specs=[pl.BlockSpec((tm * TOP_K,), lambda i: (i,), memory_space=pltpu.SMEM),
                  pl.BlockSpec((tm, d), lambda i: (i, 0)),
                  pl.BlockSpec((tm, LANES), lambda i: (i, 0)),
                  pl.BlockSpec((1, d), lambda i: (0, 0)),
                  pl.BlockSpec(memory_space=pl.ANY)],
        out_specs=pl.BlockSpec((tm, d), lambda i: (i, 0)),
        out_shape=jax.ShapeDtypeStruct((s, d), F32),
        scratch_shapes=[pltpu.VMEM((TOP_K, tm, d), F32), pltpu.SemaphoreType.DMA((1,))],
        compiler_params=_params("arbitrary"),
        name="moe_combine",
    )(pos_flat, x, gw, g2_row.reshape(1, d), ys)


def moe_layer(x, gain, shift, scale, g2_row, router_w, router_b, w_gu, b_gu, w_dn, b_dn):
    s, d = x.shape
    e = router_w.shape[1]
    hf, logits = router(x, gain, shift, scale, router_w, router_b)
    eidx, gw, rk, counts = topk_route(logits)
    cnt = counts[0, :e].astype(jnp.int32)
    padded = (cnt + MOE_ROWS - 1) // MOE_ROWS * MOE_ROWS
    padded_end = jnp.cumsum(padded)
    padded_start = jnp.pad((padded_end - padded).astype(F32), (0, LANES - e)).reshape(1, LANES)
    n_rows = s * TOP_K + e * MOE_ROWS
    blk_expert = jnp.minimum(
        jnp.searchsorted(padded_end, jnp.arange(n_rows // MOE_ROWS) * MOE_ROWS, side="right"), e - 1
    ).astype(jnp.int32)
    total_rows = padded_end[-1:].astype(jnp.int32)
    pos = slot_positions(eidx, rk, padded_start)
    pos_flat = pos[:, :TOP_K].reshape(-1)
    xs = dispatch(pos_flat, hf, n_rows)
    ys = expert_ffn(blk_expert, total_rows, xs, w_gu.astype(BF16), b_gu, w_dn.astype(BF16), b_dn)
    return combine(pos_flat, x, gw, g2_row, ys)


def kernel(x, c, positions, norm_gain, ada_w, ada_b, mlstm_w_in, mlstm_b_gates, mlstm_norm_gain, mlstm_w_out, rwkv_mu, rwkv_w_rkv, rwkv_w0, rwkv_w1, rwkv_w2, rwkv_a0, rwkv_a1, rwkv_a2, rwkv_g1, rwkv_g2, rwkv_k_k, rwkv_k_a, rwkv_r_k, rwkv_ln_gain, rwkv_ln_bias, rwkv_w_o, diff_w_qkv, diff_lambda, diff_subln_gain, diff_w_o, moe_router_w, moe_router_b, moe_w_gate_up, moe_b_gate_up, moe_w_down, moe_b_down, final_gain):
    b, s, d = x.shape
    assert b == 1
    depth = norm_gain.shape[0]
    xt = x.reshape(s, d)
    pos = positions.reshape(s)
    mod = ada_mod(c, ada_w, ada_b).reshape(depth, 6, d)
    seen = [0, 0, 0]
    for layer in range(depth):
        kind = layer % 3
        j = seen[kind]
        seen[kind] += 1
        sh1, sc1, g1, sh2, sc2, g2 = (mod[layer, i] for i in range(6))
        gain = norm_gain[layer, 0]
        if kind == 0:
            xt = mlstm_mixer(xt, gain, sh1, sc1, mlstm_w_in[j], mlstm_b_gates[j], mlstm_norm_gain[j],
                             mlstm_w_out[j], g1)
        elif kind == 1:
            xt = rwkv_mixer(xt, gain, sh1, sc1, rwkv_mu[j], rwkv_w_rkv[j], rwkv_w0[j], rwkv_w1[j], rwkv_w2[j],
                            rwkv_a0[j], rwkv_a1[j], rwkv_a2[j], rwkv_g1[j], rwkv_g2[j], rwkv_k_k[j],
                            rwkv_k_a[j], rwkv_r_k[j], rwkv_ln_gain[j], rwkv_ln_bias[j], rwkv_w_o[j], g1)
        else:
            xt = diff_mixer(xt, gain, sh1, sc1, pos, diff_w_qkv[j], diff_lambda[j], diff_subln_gain[j],
                            diff_w_o[j], layer, g1)
        xt = moe_layer(xt, norm_gain[layer, 1], sh2, sc2, g2, moe_router_w[layer], moe_router_b[layer],
                       moe_w_gate_up[layer], moe_b_gate_up[layer], moe_w_down[layer], moe_b_down[layer])
    zeros = jnp.zeros((d,), F32)
    return norm_mod(xt, final_gain, zeros, zeros, F32).reshape(b, s, d)
```

```python
import functools
import math

import numpy as np
import jax
import jax.numpy as jnp
from jax import lax
from jax.experimental import pallas as pl
from jax.experimental.pallas import tpu as pltpu

F32 = jnp.float32
BF16 = jnp.bfloat16
HIGHEST = lax.Precision.HIGHEST

LANES = 128
VMEM_LIMIT = 52 * 1024 * 1024

NORM_EPS = 1e-6
MLSTM_HEADS = 4
MLSTM_CHUNK = 128
GATE_SOFTCAP = 15.0
RWKV_HEAD = 64
RWKV_GN_EPS = 64e-5
DIFF_HEADS = 8
DIFF_DQK = 128
DIFF_DV = 256
DIFF_SUBLN_EPS = 1e-5
ROPE_DIM = 32
ROPE_THETA = 500000.0
N_EXPERTS = 32
TOP_K = 4
SWIGLU_LIMIT = 7.0
SWIGLU_ALPHA = 1.702
MOE_ROWS = 256
TOK_BLOCK = 256


def _params(*sem):
    return pltpu.CompilerParams(dimension_semantics=sem, vmem_limit_bytes=VMEM_LIMIT)


def _bdot(a, b):
    return jnp.dot(a.astype(BF16), b.astype(BF16), preferred_element_type=F32)


def _dot_nt(a, b):
    return lax.dot_general(a.astype(BF16), b.astype(BF16), (((1,), (1,)), ((), ())),
                           preferred_element_type=F32)


def _dot_tn(a, b):
    return lax.dot_general(a.astype(BF16), b.astype(BF16), (((0,), (0,)), ((), ())),
                           preferred_element_type=F32)


def _ada_body(c_ref, w_ref, b_ref, o_ref, *, kc):
    d = w_ref.shape[1]

    def step(i, acc):
        cv = c_ref[pl.ds(i * kc, kc), :]
        sc = cv * jax.nn.sigmoid(cv)
        return acc + jnp.sum(sc * w_ref[0, pl.ds(i * kc, kc), :], axis=0, keepdims=True)

    acc = lax.fori_loop(0, d // kc, step, jnp.zeros((1, w_ref.shape[2]), F32))
    o_ref[0] = acc + b_ref[0]


def ada_mod(c, ada_w, ada_b, tn=1024, kc=128):
    depth, d, n = ada_w.shape
    return pl.pallas_call(
        functools.partial(_ada_body, kc=kc),
        grid=(depth, n // tn),
        in_specs=[pl.BlockSpec((d, 1), lambda l, j: (0, 0)),
                  pl.BlockSpec((1, d, tn), lambda l, j: (l, 0, j)),
                  pl.BlockSpec((1, 1, tn), lambda l, j: (l, 0, j))],
        out_specs=pl.BlockSpec((1, 1, tn), lambda l, j: (l, 0, j)),
        out_shape=jax.ShapeDtypeStruct((depth, 1, n), F32),
        compiler_params=_params("arbitrary", "arbitrary"),
        name="ada_mod",
    )(c.reshape(d, 1), ada_w, ada_b.reshape(depth, 1, n))


def _norm_rows(x, gain, shift, scale, eps):
    y = x * lax.rsqrt(jnp.mean(x * x, axis=-1, keepdims=True) + eps)
    return (y * gain) * (1.0 + scale) + shift


def _norm_mod_body(x_ref, g_ref, sh_ref, sc_ref, o_ref):
    o_ref[...] = _norm_rows(x_ref[...], g_ref[...], sh_ref[...], sc_ref[...], NORM_EPS).astype(o_ref.dtype)


def norm_mod(x, gain, shift, scale, out_dtype, tm=256):
    s, d = x.shape
    row = pl.BlockSpec((1, d), lambda i: (0, 0))
    return pl.pallas_call(
        _norm_mod_body,
        grid=(s // tm,),
        in_specs=[pl.BlockSpec((tm, d), lambda i: (i, 0)), row, row, row],
        out_specs=pl.BlockSpec((tm, d), lambda i: (i, 0)),
        out_shape=jax.ShapeDtypeStruct((s, d), out_dtype),
        compiler_params=_params("arbitrary"),
        name="norm_mod",
    )(x, gain.reshape(1, d), shift.reshape(1, d), scale.reshape(1, d))


def _mm_body(*refs, fused):
    if fused:
        x_ref, w_ref, res_ref, gate_ref, o_ref, wb_ref = refs
    else:
        x_ref, w_ref, o_ref, wb_ref = refs

    @pl.when(pl.program_id(1) == 0)
    def _():
        wb_ref[...] = w_ref[...].astype(BF16)

    acc = jnp.dot(x_ref[...].astype(BF16), wb_ref[...], preferred_element_type=F32)
    if fused:
        acc = res_ref[...] + gate_ref[...] * acc
    o_ref[...] = acc.astype(o_ref.dtype)


def matmul(x, w, out_dtype=F32, res=None, gate=None, tm=512, tn=1024):
    m, k = x.shape
    n = w.shape[1]
    tn = min(tn, n)
    tm = min(tm, m)
    fused = res is not None
    in_specs = [pl.BlockSpec((tm, k), lambda j, i: (i, 0)),
                pl.BlockSpec((k, tn), lambda j, i: (0, j))]
    args = [x, w]
    if fused:
        in_specs += [pl.BlockSpec((tm, tn), lambda j, i: (i, j)),
                     pl.BlockSpec((1, tn), lambda j, i: (0, j))]
        args += [res, gate.reshape(1, n)]
    return pl.pallas_call(
        functools.partial(_mm_body, fused=fused),
        grid=(n // tn, m // tm),
        in_specs=in_specs,
        out_specs=pl.BlockSpec((tm, tn), lambda j, i: (i, j)),
        out_shape=jax.ShapeDtypeStruct((m, n), out_dtype),
        scratch_shapes=[pltpu.VMEM((k, tn), BF16)],
        compiler_params=_params("arbitrary", "arbitrary"),
        name="matmul",
    )(*args)


def _mlstm_body(q_ref, k_ref, v_ref, o_ref, gt_ref, bg_ref, ng_ref, out_ref, c_s, n_s, m_s):
    h = pl.program_id(0)
    L = q_ref.shape[0]
    dk = q_ref.shape[1]
    dv = v_ref.shape[1]

    @pl.when(pl.program_id(1) == 0)
    def _():
        c_s[...] = jnp.zeros_like(c_s)
        n_s[...] = jnp.zeros_like(n_s)
        m_s[...] = jnp.zeros_like(m_s)

    lane = lax.broadcasted_iota(jnp.int32, (L, LANES), 1)
    row = lax.broadcasted_iota(jnp.int32, (L, L), 0)
    col = lax.broadcasted_iota(jnp.int32, (L, L), 1)

    capped = GATE_SOFTCAP * jnp.tanh((gt_ref[...] + bg_ref[...]) / GATE_SOFTCAP)
    li = jnp.sum(jnp.where(lane == h, capped, 0.0), axis=-1, keepdims=True)
    fpre = jnp.sum(jnp.where(lane == h + MLSTM_HEADS, capped, 0.0), axis=-1, keepdims=True)
    lf = jax.nn.log_sigmoid(fpre)

    tril = (row >= col).astype(F32)
    g_b = jnp.dot(tril, jnp.broadcast_to(lf, (L, L)), precision=HIGHEST,
                  preferred_element_type=F32)
    g = g_b[:, 0:1]
    g_tot = jnp.sum(lf, axis=0, keepdims=True)
    eye = row == col
    g_row = jnp.sum(jnp.where(eye, g_b, 0.0), axis=0, keepdims=True)
    li_row = jnp.sum(jnp.where(eye, jnp.broadcast_to(li, (L, L)), 0.0), axis=0, keepdims=True)

    m_prev = m_s[...]
    c_prev = c_s[...]
    n_prev = n_s[...]

    q = q_ref[...] * (dk ** -0.5)
    k = k_ref[...]
    vb = v_ref[...].astype(BF16)
    qb = q.astype(BF16)

    d_log = jnp.where(row >= col, g_b - g_row + li_row, -jnp.inf)
    inter_log = g + m_prev
    m_t = jnp.maximum(inter_log, jnp.max(d_log, axis=-1, keepdims=True))
    s = _dot_nt(qb, k) * jnp.exp(d_log - m_t)
    inter = jnp.exp(inter_log - m_t)
    num = _bdot(s, vb) + inter * _bdot(qb, c_prev)
    den = jnp.sum(s, axis=-1, keepdims=True) + inter * jnp.sum(q * n_prev, axis=-1, keepdims=True)
    hh = num / jnp.maximum(jnp.abs(den), jnp.exp(-m_t))
    hh = hh * lax.rsqrt(jnp.mean(hh * hh, axis=-1, keepdims=True) + NORM_EPS) * ng_ref[...]
    out_ref[...] = (jax.nn.sigmoid(o_ref[...]) * hh).astype(out_ref.dtype)

    w_state = g_tot - g + li
    m_loc = jnp.max(w_state, axis=0, keepdims=True)
    ks = jnp.exp(w_state - m_loc) * k
    kv = _dot_tn(ks, vb)
    n_c = jnp.sum(ks, axis=0, keepdims=True)
    m_new = jnp.maximum(g_tot + m_prev, m_loc)
    a = jnp.exp(g_tot + m_prev - m_new)
    b = jnp.exp(m_loc - m_new)
    c_s[...] = a * c_prev + b * kv
    n_s[...] = a * n_prev + b * n_c
    m_s[...] = m_new


def mlstm_cells(proj, gates, b_gates_pad, norm_gain):
    s = proj.shape[0]
    d = norm_gain.shape[0]
    H, L = MLSTM_HEADS, MLSTM_CHUNK
    dk = (proj.shape[1] - 2 * d) // (2 * H)
    dv = d // H
    kb, vb, ob = H, (2 * H * dk) // dv, (2 * H * dk + d) // dv
    return pl.pallas_call(
        _mlstm_body,
        grid=(H, s // L),
        in_specs=[pl.BlockSpec((L, dk), lambda h, c: (c, h)),
                  pl.BlockSpec((L, dk), lambda h, c: (c, kb + h)),
                  pl.BlockSpec((L, dv), lambda h, c: (c, vb + h)),
                  pl.BlockSpec((L, dv), lambda h, c: (c, ob + h)),
                  pl.BlockSpec((L, LANES), lambda h, c: (c, 0)),
                  pl.BlockSpec((1, LANES), lambda h, c: (0, 0)),
                  pl.BlockSpec((1, dv), lambda h, c: (0, h))],
        out_specs=pl.BlockSpec((L, dv), lambda h, c: (c, h)),
        out_shape=jax.ShapeDtypeStruct((s, d), BF16),
        scratch_shapes=[pltpu.VMEM((dk, dv), F32), pltpu.VMEM((1, dk), F32), pltpu.VMEM((1, 1), F32)],
        compiler_params=_params("arbitrary", "arbitrary"),
        name="mlstm_cells",
    )(proj, proj, proj, proj, gates, b_gates_pad, norm_gain.reshape(1, d))


def mlstm_mixer(x, gain, shift, scale, w_in, b_gates, norm_gain, w_out, gate_row):
    d = x.shape[1]
    n_main = w_in.shape[1] - 2 * MLSTM_HEADS
    hm = norm_mod(x, gain, shift, scale, BF16)
    proj = matmul(hm, w_in[:, :n_main])
    w_g = jnp.pad(w_in[:, n_main:], ((0, 0), (0, LANES - 2 * MLSTM_HEADS)))
    gates = matmul(hm, w_g)
    b_pad = jnp.pad(b_gates, (0, LANES - 2 * MLSTM_HEADS)).reshape(1, LANES)
    cells = mlstm_cells(proj, gates, b_pad, norm_gain)
    return matmul(cells, w_out, res=x, gate=gate_row)


def _rwkv_prep_body(x_ref, xp_ref, g_ref, sh_ref, sc_ref, mu_ref, *outs):
    tm = x_ref.shape[0]
    hm = _norm_rows(x_ref[...], g_ref[...], sh_ref[...], sc_ref[...], NORM_EPS)
    prev = _norm_rows(xp_ref[7:8, :], g_ref[...], sh_ref[...], sc_ref[...], NORM_EPS)
    prev = jnp.where(pl.program_id(0) == 0, 0.0, prev)
    rows = lax.broadcasted_iota(jnp.int32, hm.shape, 0)
    shifted = jnp.where(rows == 0, prev, pltpu.roll(hm, 1, axis=0))
    xx = shifted - hm
    for j, o_ref in enumerate(outs):
        o_ref[...] = (hm + xx * mu_ref[j:j + 1, :]).astype(o_ref.dtype)


def rwkv_prep(x, gain, shift, scale, mu, tm=256):
    s, d = x.shape
    row = pl.BlockSpec((1, d), lambda i: (0, 0))
    blk = pl.BlockSpec((tm, d), lambda i: (i, 0))
    return pl.pallas_call(
        _rwkv_prep_body,
        grid=(s // tm,),
        in_specs=[blk, pl.BlockSpec((8, d), lambda i: (jnp.maximum(i * (tm // 8) - 1, 0), 0)),
                  row, row, row, pl.BlockSpec((8, d), lambda i: (0, 0))],
        out_specs=[blk] * 6,
        out_shape=[jax.ShapeDtypeStruct((s, d), BF16)] * 6,
        compiler_params=_params("arbitrary"),
        name="rwkv_prep",
    )(x, x, gain.reshape(1, d), shift.reshape(1, d), scale.reshape(1, d), jnp.pad(mu, ((0, 2), (0, 0))))


def _lora_body(x_ref, w1_ref, w2_ref, o_ref, *, act):
    t = _bdot(x_ref[...], w1_ref[...])
    if act == "tanh":
        t = jnp.tanh(t)
    elif act == "sigmoid":
        t = jax.nn.sigmoid(t)
    o_ref[...] = _bdot(t, w2_ref[...])


def lora(x, w1, w2, act, tm=512):
    s, d = x.shape
    r = w1.shape[1]
    rp = -(-r // LANES) * LANES
    w1p = jnp.pad(w1, ((0, 0), (0, rp - r)))
    w2p = jnp.pad(w2, ((0, rp - r), (0, 0)))
    n = w2.shape[1]
    return pl.pallas_call(
        functools.partial(_lora_body, act=act),
        grid=(s // tm,),
        in_specs=[pl.BlockSpec((tm, d), lambda i: (i, 0)),
                  pl.BlockSpec((d, rp), lambda i: (0, 0)),
                  pl.BlockSpec((rp, n), lambda i: (0, 0))],
        out_specs=pl.BlockSpec((tm, n), lambda i: (i, 0)),
        out_shape=jax.ShapeDtypeStruct((s, n), F32),
        compiler_params=_params("arbitrary"),
        name="lora",
    )(x, w1p, w2p)


def _split3(x):
    hi = x.astype(BF16)
    r1 = x - hi.astype(F32)
    mid = r1.astype(BF16)
    lo = (r1 - mid.astype(F32)).astype(BF16)
    return hi, mid, lo


def _segsum(x, bmat, passes):
    parts = _split3(x)[:passes] if passes > 1 else (x.astype(BF16),)
    acc = jnp.dot(parts[0], bmat, preferred_element_type=F32)
    for p in parts[1:]:
        acc = acc + jnp.dot(p, bmat, preferred_element_type=F32)
    return acc


SUBLANES = 8


SCAN_W = 256


def _head_ones(width):
    ri = lax.broadcasted_iota(jnp.int32, (width, width), 0)
    ci = lax.broadcasted_iota(jnp.int32, (width, width), 1)
    return ((ri // RWKV_HEAD) == (ci // RWKV_HEAD)).astype(BF16)


def _rwkv_pre_body(k_ref, wl_ref, al_ref, w0_ref, a0_ref, kk_ref, ka_ref, nkk_ref, w_ref, kka_ref, kp_ref):
    d = k_ref.shape[1]
    bmat = _head_ones(SCAN_W)
    for j in range(d // SCAN_W):
        cs = slice(j * SCAN_W, (j + 1) * SCAN_W)
        kraw = k_ref[:, cs]
        w_log = -jax.nn.softplus(-(w0_ref[:, cs] + wl_ref[:, cs])) - 0.5
        w_ref[:, cs] = jnp.exp(-jnp.exp(w_log))
        a = jax.nn.sigmoid(a0_ref[:, cs] + al_ref[:, cs])
        kk = kraw * kk_ref[:, cs]
        nrm = jnp.sqrt(_segsum(kk * kk, bmat, 3))
        kk = kk / jnp.maximum(nrm, 1e-12)
        nkk_ref[:, cs] = -kk
        kka_ref[:, cs] = kk * a
        kp_ref[:, cs] = kraw * (1.0 + (a - 1.0) * ka_ref[:, cs])


def rwkv_pre(k, wl, al, w0, a0, k_k, k_a, tm=256):
    s, d = k.shape
    blk = pl.BlockSpec((tm, d), lambda i: (i, 0))
    row = pl.BlockSpec((1, d), lambda i: (0, 0))
    return pl.pallas_call(
        _rwkv_pre_body,
        grid=(s // tm,),
        in_specs=[blk] * 3 + [row] * 4,
        out_specs=[blk] * 4,
        out_shape=[jax.ShapeDtypeStruct((s, d), F32)] * 4,
        compiler_params=_params("arbitrary"),
        name="rwkv_pre",
    )(k, wl, al, *[t.reshape(1, d) for t in (w0, a0, k_k, k_a)])


def _rwkv_post_body(y_ref, r_ref, kp_ref, v_ref, gt_ref, rk_ref, lg_ref, lb_ref, o_ref):
    d = y_ref.shape[1]
    bmat = _head_ones(SCAN_W)
    inv = 1.0 / RWKV_HEAD
    for j in range(d // SCAN_W):
        cs = slice(j * SCAN_W, (j + 1) * SCAN_W)
        y = y_ref[:, cs]
        mean = _segsum(y, bmat, 3) * inv
        yc = y - mean
        var = _segsum(yc * yc, bmat, 3) * inv
        yn = yc * lax.rsqrt(var + RWKV_GN_EPS) * lg_ref[:, cs] + lb_ref[:, cs]
        bonus = _segsum(r_ref[:, cs] * kp_ref[:, cs] * rk_ref[:, cs], bmat, 3) * v_ref[:, cs]
        o_ref[:, cs] = ((yn + bonus) * gt_ref[:, cs]).astype(o_ref.dtype)


def rwkv_post(y, r, kp, v, gt, r_k, ln_gain, ln_bias, tm=256):
    s, d = y.shape
    blk = pl.BlockSpec((tm, d), lambda i: (i, 0))
    row = pl.BlockSpec((1, d), lambda i: (0, 0))
    return pl.pallas_call(
        _rwkv_post_body,
        grid=(s // tm,),
        in_specs=[blk] * 5 + [row] * 3,
        out_specs=blk,
        out_shape=jax.ShapeDtypeStruct((s, d), BF16),
        compiler_params=_params("arbitrary"),
        name="rwkv_post",
    )(y, r, kp, v, gt, *[t.reshape(1, d) for t in (r_k, ln_gain, ln_bias)])


def _rwkv_scan_body(nkk_ref, w_ref, kka_ref, kp_ref, r_ref, v_ref, y_ref, st_s, b_s, m1_s,
                    nkk3, w3, kka3, kp3, r3, v3, y3):
    tb, d = r_ref.shape
    nv = RWKV_HEAD
    ntile = d // SCAN_W
    ngrp = tb // SUBLANES
    nslab = nv // SUBLANES
    slab = ntile * SUBLANES
    rows = nslab * slab

    @pl.when(pl.program_id(0) == 0)
    def _():
        st_s[...] = jnp.zeros_like(st_s)
        b_s[...] = _head_ones(SCAN_W)
        vi = lax.broadcasted_iota(jnp.int32, (rows, SCAN_W), 0)
        li = lax.broadcasted_iota(jnp.int32, (rows, SCAN_W), 1)
        v_idx = (vi // slab) * SUBLANES + vi % SUBLANES
        m1_s[...] = (v_idx == (li % nv)).astype(F32)

    def replicate(g, carry):
        base = pl.multiple_of(g * SUBLANES, SUBLANES)
        for src, dst in ((nkk_ref, nkk3), (w_ref, w3), (kka_ref, kka3), (kp_ref, kp3), (r_ref, r3), (v_ref, v3)):
            for j in range(ntile):
                grp = src[pl.ds(base, SUBLANES), j * SCAN_W:(j + 1) * SCAN_W]
                for i in range(SUBLANES):
                    dst[base + i, j * SUBLANES:(j + 1) * SUBLANES, :] = jnp.broadcast_to(
                        grp[i:i + 1, :], (SUBLANES, SCAN_W))
        return carry

    lax.fori_loop(0, ngrp, replicate, 0)

    def tall(ref, t):
        return jnp.concatenate([ref[t]] * nslab, axis=0)

    def step(t, carry):
        bmat = b_s[...]
        m1 = m1_s[...]
        st = st_s[...]
        p = st * tall(nkk3, t)
        hi = p.astype(BF16)
        lo = (p - hi.astype(F32)).astype(BF16)
        mv = (m1 * tall(v3, t)).astype(BF16)
        seg = jnp.dot(jnp.concatenate([hi, lo, mv], axis=0), bmat, preferred_element_type=F32)
        sa = seg[0:rows] + seg[rows:2 * rows]
        vcol = seg[2 * rows:3 * rows]
        sn = st * tall(w3, t) + sa * tall(kka3, t) + vcol * tall(kp3, t)
        st_s[...] = sn
        part = m1 * jnp.dot((sn * tall(r3, t)).astype(BF16), bmat, preferred_element_type=F32)
        acc = part[0:slab]
        for i in range(1, nslab):
            acc = acc + part[i * slab:(i + 1) * slab]
        y3[t] = acc
        return carry

    lax.fori_loop(0, tb, step, 0)

    def collect(g, carry):
        base = pl.multiple_of(g * SUBLANES, SUBLANES)
        for j in range(ntile):
            got = [jnp.sum(y3[base + i, j * SUBLANES:(j + 1) * SUBLANES, :], axis=0, keepdims=True)
                   for i in range(SUBLANES)]
            y_ref[pl.ds(base, SUBLANES), j * SCAN_W:(j + 1) * SCAN_W] = jnp.concatenate(got, axis=0)
        return carry

    lax.fori_loop(0, ngrp, collect, 0)


def rwkv_scan(nkk, w, kka, kp, r, v, tb=32):
    s, d = r.shape
    blk = pl.BlockSpec((tb, d), lambda i: (i, 0))
    slab = (d // SCAN_W) * SUBLANES
    state_rows = (RWKV_HEAD // SUBLANES) * slab
    return pl.pallas_call(
        _rwkv_scan_body,
        grid=(s // tb,),
        in_specs=[blk] * 6,
        out_specs=blk,
        out_shape=jax.ShapeDtypeStruct((s, d), F32),
        scratch_shapes=[pltpu.VMEM((state_rows, SCAN_W), F32), pltpu.VMEM((SCAN_W, SCAN_W), BF16),
                        pltpu.VMEM((state_rows, SCAN_W), F32)]
                       + [pltpu.VMEM((tb, slab, SCAN_W), F32)] * 7,
        compiler_params=_params("arbitrary"),
        name="rwkv_scan",
    )(nkk, w, kka, kp, r, v)


def rwkv_mixer(x, gain, shift, scale, mu, w_rkv, w0, w1, w2, a0, a1, a2, g1, g2, k_k, k_a, r_k,
               ln_gain, ln_bias, w_o, gate_row):
    xr, xk, xv, xw, xa, xg = rwkv_prep(x, gain, shift, scale, mu)
    r = matmul(xr, w_rkv[0])
    k = matmul(xk, w_rkv[1])
    v = matmul(xv, w_rkv[2])
    wl = lora(xw, w1, w2, "tanh")
    al = lora(xa, a1, a2, "none")
    gt = lora(xg, g1, g2, "sigmoid")
    nkk, w, kka, kp = rwkv_pre(k, wl, al, w0, a0, k_k, k_a)
    y = rwkv_scan(nkk, w, kka, kp, r, v)
    y = rwkv_post(y, r, kp, v, gt, r_k.reshape(-1), ln_gain, ln_bias)
    return matmul(y, w_o, res=x, gate=gate_row)


def _rope_body(p_ref, pos_ref, o_ref, *, n_rot_tiles, q_tiles):
    tm, n = p_ref.shape
    half = ROPE_DIM // 2
    lane = lax.broadcasted_iota(jnp.int32, (tm, LANES), 1)
    inv_freq = jnp.exp((lane % half).astype(F32) * (-2.0 * math.log(ROPE_THETA) / ROPE_DIM))
    ang = pos_ref[...].astype(F32) * inv_freq
    roped = lane < ROPE_DIM
    cos = jnp.where(roped, jnp.cos(ang), 1.0)
    sin = jnp.where(roped, jnp.where(lane < half, -jnp.sin(ang), jnp.sin(ang)), 0.0)
    for j in range(n // LANES):
        x = p_ref[:, j * LANES:(j + 1) * LANES]
        if j < n_rot_tiles:
            partner = jnp.where(lane < half, pltpu.roll(x, LANES - half, axis=1), pltpu.roll(x, half, axis=1))
            x = x * cos + partner * sin
            if j < q_tiles:
                x = x * (DIFF_DQK ** -0.5)
        o_ref[:, j * LANES:(j + 1) * LANES] = x.astype(o_ref.dtype)


def rope_cast(proj, positions, tm=256):
    s, n = proj.shape
    q_tiles = 2 * DIFF_HEADS
    return pl.pallas_call(
        functools.partial(_rope_body, n_rot_tiles=2 * q_tiles, q_tiles=q_tiles),
        grid=(s // tm,),
        in_specs=[pl.BlockSpec((tm, n), lambda i: (i, 0)),
                  pl.BlockSpec((tm, 1), lambda i: (i, 0))],
        out_specs=pl.BlockSpec((tm, n), lambda i: (i, 0)),
        out_shape=jax.ShapeDtypeStruct((s, n), BF16),
        compiler_params=_params("arbitrary"),
        name="rope_cast",
    )(proj, positions.reshape(s, 1))


def _diff_attn_body(qi_ref, kj_ref, q_ref, k_ref, v_ref, lam_ref, sg_ref, o_ref, m_s, l_s, acc_s, *, lambda_init):
    step = pl.program_id(1)
    qi, kj = qi_ref[step], kj_ref[step]
    tb = q_ref.shape[0]

    @pl.when(kj == 0)
    def _():
        m_s[...] = jnp.full_like(m_s, -jnp.inf)
        l_s[...] = jnp.zeros_like(l_s)
        acc_s[...] = jnp.zeros_like(acc_s)

    def accumulate(diagonal):
        v = v_ref[...]
        for c in range(2):
            cs = slice(c * DIFF_DQK, (c + 1) * DIFF_DQK)
            s = _dot_nt(q_ref[:, cs], k_ref[:, cs])
            if diagonal:
                rows = lax.broadcasted_iota(jnp.int32, (tb, tb), 0)
                cols = lax.broadcasted_iota(jnp.int32, (tb, tb), 1)
                s = jnp.where(cols <= rows, s, -jnp.inf)
            m_old = m_s[c]
            m_new = jnp.maximum(m_old, jnp.max(s, axis=-1, keepdims=True))
            alpha = jnp.exp(m_old - m_new)
            p = jnp.exp(s - m_new)
            l_s[c] = alpha * l_s[c] + jnp.sum(p, axis=-1, keepdims=True)
            acc_s[c] = alpha * acc_s[c] + _bdot(p, v)
            m_s[c] = m_new

    @pl.when(kj < qi)
    def _():
        accumulate(False)

    @pl.when(kj == qi)
    def _():
        accumulate(True)
        lp = lam_ref[...]
        lam = (jnp.exp(jnp.sum(lp[0:1] * lp[1:2], axis=-1, keepdims=True))
               - jnp.exp(jnp.sum(lp[2:3] * lp[3:4], axis=-1, keepdims=True)) + lambda_init)
        out = acc_s[0] / l_s[0] - lam * (acc_s[1] / l_s[1])
        out = out * lax.rsqrt(jnp.mean(out * out, axis=-1, keepdims=True) + DIFF_SUBLN_EPS)
        o_ref[...] = (out * sg_ref[...] * (1.0 - lambda_init)).astype(o_ref.dtype)


def diff_attention(qkv, lam_params, subln_gain, lambda_init, tb=1024):
    s = qkv.shape[0]
    H, dv = DIFF_HEADS, DIFF_DV
    tb = min(tb, s)
    nb = s // tb
    pairs = [(qi, kj) for qi in range(nb) for kj in range(qi + 1)]
    qi_tab = jnp.asarray([p[0] for p in pairs], jnp.int32)
    kj_tab = jnp.asarray([p[1] for p in pairs], jnp.int32)
    grid_spec = pltpu.PrefetchScalarGridSpec(
        num_scalar_prefetch=2,
        grid=(H, len(pairs)),
        in_specs=[pl.BlockSpec((tb, dv), lambda h, t, qi, kj: (qi[t], h)),
                  pl.BlockSpec((tb, dv), lambda h, t, qi, kj: (kj[t], H + h)),
                  pl.BlockSpec((tb, dv), lambda h, t, qi, kj: (kj[t], 2 * H + h)),
                  pl.BlockSpec((4, DIFF_DQK), lambda h, t, qi, kj: (0, 0)),
                  pl.BlockSpec((1, dv), lambda h, t, qi, kj: (0, 0))],
        out_specs=pl.BlockSpec((tb, dv), lambda h, t, qi, kj: (qi[t], h)),
        scratch_shapes=[pltpu.VMEM((2, tb, 1), F32), pltpu.VMEM((2, tb, 1), F32),
                        pltpu.VMEM((2, tb, dv), F32)],
    )
    return pl.pallas_call(
        functools.partial(_diff_attn_body, lambda_init=lambda_init),
        grid_spec=grid_spec,
        out_shape=jax.ShapeDtypeStruct((s, H * dv), BF16),
        compiler_params=_params("arbitrary", "arbitrary"),
        name="diff_attention",
    )(qi_tab, kj_tab, qkv, qkv, qkv, lam_params, subln_gain.reshape(1, dv))


def diff_mixer(x, gain, shift, scale, positions, w_qkv, lam_params, subln_gain, w_o, layer, gate_row):
    hm = norm_mod(x, gain, shift, scale, BF16)
    proj = matmul(hm, w_qkv)
    qkv = rope_cast(proj, positions)
    lambda_init = 0.8 - 0.6 * math.exp(-0.3 * layer)
    att = diff_attention(qkv, lam_params, subln_gain, lambda_init)
    return matmul(att, w_o, res=x, gate=gate_row)


def _router_body(x_ref, g_ref, sh_ref, sc_ref, rw_ref, rb_ref, hf_ref, lg_ref):
    hf = _norm_rows(x_ref[...], g_ref[...], sh_ref[...], sc_ref[...], NORM_EPS)
    hf_ref[...] = hf
    lg_ref[...] = jnp.dot(hf, rw_ref[...], precision=HIGHEST, preferred_element_type=F32) + rb_ref[...]


def router(x, gain, shift, scale, router_w, router_b, tm=256):
    s, d = x.shape
    e = router_w.shape[1]
    rw = jnp.pad(router_w, ((0, 0), (0, LANES - e)))
    rb = jnp.pad(router_b, (0, LANES - e), constant_values=-1e30).reshape(1, LANES)
    row = pl.BlockSpec((1, d), lambda i: (0, 0))
    return pl.pallas_call(
        _router_body,
        grid=(s // tm,),
        in_specs=[pl.BlockSpec((tm, d), lambda i: (i, 0)), row, row, row,
                  pl.BlockSpec((d, LANES), lambda i: (0, 0)), pl.BlockSpec((1, LANES), lambda i: (0, 0))],
        out_specs=[pl.BlockSpec((tm, d), lambda i: (i, 0)), pl.BlockSpec((tm, LANES), lambda i: (i, 0))],
        out_shape=[jax.ShapeDtypeStruct((s, d), F32), jax.ShapeDtypeStruct((s, LANES), F32)],
        compiler_params=_params("arbitrary"),
        name="router",
    )(x, gain.reshape(1, d), shift.reshape(1, d), scale.reshape(1, d), rw, rb)


def _topk_body(lg_ref, eidx_ref, gw_ref, rk_ref, cnt_ref):
    tm = lg_ref.shape[0]

    @pl.when(pl.program_id(0) == 0)
    def _():
        cnt_ref[...] = jnp.zeros_like(cnt_ref)

    lane = lax.broadcasted_iota(jnp.int32, (tm, LANES), 1).astype(F32)
    work = lg_ref[...]
    vals, idxs, hots = [], [], []
    for _ in range(TOP_K):
        m = jnp.max(work, axis=-1, keepdims=True)
        idx = jnp.min(jnp.where(work == m, lane, float(LANES)), axis=-1, keepdims=True)
        hot = lane == idx
        work = jnp.where(hot, -jnp.inf, work)
        vals.append(m)
        idxs.append(idx)
        hots.append(hot)
    exps = [jnp.exp(v - vals[0]) for v in vals]
    den = exps[0] + exps[1] + exps[2] + exps[3]
    sel = jnp.zeros((tm, LANES), F32)
    for hot in hots:
        sel = sel + hot.astype(F32)
    ri = lax.broadcasted_iota(jnp.int32, (tm, tm), 0)
    ci = lax.broadcasted_iota(jnp.int32, (tm, tm), 1)
    before = jnp.dot((ri > ci).astype(BF16), sel.astype(BF16), preferred_element_type=F32)
    rank = before + cnt_ref[...]
    cnt_ref[...] = cnt_ref[...] + jnp.sum(sel, axis=0, keepdims=True)
    eidx = jnp.zeros((tm, LANES), F32)
    gw = jnp.zeros((tm, LANES), F32)
    rk = jnp.zeros((tm, LANES), F32)
    for kslot in range(TOP_K):
        here = lane == kslot
        eidx = jnp.where(here, idxs[kslot], eidx)
        gw = jnp.where(here, exps[kslot] / den, gw)
        rk = jnp.where(here, jnp.sum(jnp.where(hots[kslot], rank, 0.0), axis=-1, keepdims=True), rk)
    eidx_ref[...] = eidx
    gw_ref[...] = gw
    rk_ref[...] = rk


def topk_route(logits, tm=TOK_BLOCK):
    s = logits.shape[0]
    blk = pl.BlockSpec((tm, LANES), lambda i: (i, 0))
    return pl.pallas_call(
        _topk_body,
        grid=(s // tm,),
        in_specs=[blk],
        out_specs=[blk, blk, blk, pl.BlockSpec((1, LANES), lambda i: (0, 0))],
        out_shape=[jax.ShapeDtypeStruct((s, LANES), F32), jax.ShapeDtypeStruct((s, LANES), F32),
                   jax.ShapeDtypeStruct((s, LANES), F32), jax.ShapeDtypeStruct((1, LANES), F32)],
        compiler_params=_params("arbitrary"),
        name="topk_route",
    )(logits)


def _slot_body(eidx_ref, rk_ref, ps_ref, pos_ref):
    tm = eidx_ref.shape[0]
    lane = lax.broadcasted_iota(jnp.int32, (tm, LANES), 1).astype(F32)
    eidx = eidx_ref[...]
    rk = rk_ref[...]
    pos = jnp.zeros((tm, LANES), F32)
    for kslot in range(TOP_K):
        e_k = jnp.sum(jnp.where(lane == kslot, eidx, 0.0), axis=-1, keepdims=True)
        start = jnp.sum(jnp.where(lane == e_k, ps_ref[...], 0.0), axis=-1, keepdims=True)
        r_k = jnp.sum(jnp.where(lane == kslot, rk, 0.0), axis=-1, keepdims=True)
        pos = jnp.where(lane == kslot, start + r_k, pos)
    pos_ref[...] = pos.astype(jnp.int32)


def slot_positions(eidx, rk, padded_start, tm=TOK_BLOCK):
    s = eidx.shape[0]
    blk = pl.BlockSpec((tm, LANES), lambda i: (i, 0))
    return pl.pallas_call(
        _slot_body,
        grid=(s // tm,),
        in_specs=[blk, blk, pl.BlockSpec((1, LANES), lambda i: (0, 0))],
        out_specs=blk,
        out_shape=jax.ShapeDtypeStruct((s, LANES), jnp.int32),
        compiler_params=_params("arbitrary"),
        name="slot_positions",
    )(eidx, rk, padded_start)


def _dispatch_body(pos_ref, hf_ref, xs_in_ref, xs_ref, sem):
    del xs_in_ref
    tm = hf_ref.shape[0]

    def issue(t, carry):
        for kslot in range(TOP_K):
            p = pos_ref[t * TOP_K + kslot]
            pltpu.make_async_copy(hf_ref.at[pl.ds(t, 1)], xs_ref.at[pl.ds(p, 1)], sem.at[0]).start()
        return carry

    lax.fori_loop(0, tm, issue, 0)

    for _ in range(TOP_K):
        pltpu.make_async_copy(hf_ref, xs_ref.at[pl.ds(0, tm)], sem.at[0]).wait()


def dispatch(pos_flat, hf, n_rows, tm=TOK_BLOCK):
    s, d = hf.shape
    return pl.pallas_call(
        _dispatch_body,
        grid=(s // tm,),
        in_specs=[pl.BlockSpec((tm * TOP_K,), lambda i: (i,), memory_space=pltpu.SMEM),
                  pl.BlockSpec((tm, d), lambda i: (i, 0)),
                  pl.BlockSpec(memory_space=pl.ANY)],
        out_specs=pl.BlockSpec(memory_space=pl.ANY),
        out_shape=jax.ShapeDtypeStruct((n_rows, d), hf.dtype),
        scratch_shapes=[pltpu.SemaphoreType.DMA((1,))],
        input_output_aliases={2: 0},
        compiler_params=_params("arbitrary"),
        name="moe_dispatch",
    )(pos_flat, hf, jnp.zeros((n_rows, d), hf.dtype))


def _expert_body(be_ref, tot_ref, xs_ref, wgu_ref, bgu_ref, wdn_ref, bdn_ref, perm_ref, ys_ref):
    i = pl.program_id(0)
    rows = xs_ref.shape[0]
    n_gu = wgu_ref.shape[2]
    grp = perm_ref.shape[0]
    half = grp // 2

    @pl.when(i * rows < tot_ref[0])
    def _():
        gu = _bdot(xs_ref[...], wgu_ref[0]) + bgu_ref[0]
        gub = gu.astype(BF16)
        gs, us = [], []
        for j in range(n_gu // grp):
            pj = jnp.dot(gub[:, j * grp:(j + 1) * grp], perm_ref[...], preferred_element_type=F32)
            gs.append(pj[:, :half])
            us.append(pj[:, half:])
        g = jnp.minimum(jnp.concatenate(gs, axis=1), SWIGLU_LIMIT)
        u = jnp.clip(jnp.concatenate(us, axis=1), -SWIGLU_LIMIT, SWIGLU_LIMIT)
        act = (u + 1.0) * (g * jax.nn.sigmoid(g * SWIGLU_ALPHA))
        ys_ref[...] = _bdot(act, wdn_ref[0]) + bdn_ref[0]

    @pl.when(i * rows >= tot_ref[0])
    def _():
        ys_ref[...] = jnp.zeros_like(ys_ref)


def expert_ffn(blk_expert, total_rows, xs, w_gu, b_gu, w_dn, b_dn, rows=MOE_ROWS):
    n_rows, d = xs.shape
    e, _, n_gu = w_gu.shape
    dh = w_dn.shape[1]
    grp = 2 * LANES
    perm = np.zeros((grp, grp), np.float32)
    perm[2 * np.arange(LANES), np.arange(LANES)] = 1.0
    perm[2 * np.arange(LANES) + 1, LANES + np.arange(LANES)] = 1.0
    grid_spec = pltpu.PrefetchScalarGridSpec(
        num_scalar_prefetch=2,
        grid=(n_rows // rows,),
        in_specs=[pl.BlockSpec((rows, d), lambda i, be, tot: (i, 0)),
                  pl.BlockSpec((1, d, n_gu), lambda i, be, tot: (be[i], 0, 0)),
                  pl.BlockSpec((1, 1, n_gu), lambda i, be, tot: (be[i], 0, 0)),
                  pl.BlockSpec((1, dh, d), lambda i, be, tot: (be[i], 0, 0)),
                  pl.BlockSpec((1, 1, d), lambda i, be, tot: (be[i], 0, 0)),
                  pl.BlockSpec((grp, grp), lambda i, be, tot: (0, 0))],
        out_specs=pl.BlockSpec((rows, d), lambda i, be, tot: (i, 0)),
    )
    return pl.pallas_call(
        _expert_body,
        grid_spec=grid_spec,
        out_shape=jax.ShapeDtypeStruct((n_rows, d), F32),
        compiler_params=_params("arbitrary"),
        name="moe_experts",
    )(blk_expert, total_rows, xs, w_gu, b_gu.reshape(e, 1, n_gu), w_dn, b_dn.reshape(e, 1, d),
      jnp.asarray(perm, BF16))


def _combine_body(pos_ref, x_ref, gw_ref, g2_ref, ys_ref, o_ref, buf, sem):
    tm = x_ref.shape[0]

    def issue(t, carry):
        for kslot in range(TOP_K):
            p = pos_ref[t * TOP_K + kslot]
            pltpu.make_async_copy(ys_ref.at[pl.ds(p, 1)], buf.at[kslot, pl.ds(t, 1)], sem.at[0]).start()
        return carry

    lax.fori_loop(0, tm, issue, 0)

    for kslot in range(TOP_K):
        pltpu.make_async_copy(ys_ref.at[pl.ds(0, tm)], buf.at[kslot], sem.at[0]).wait()

    gw = gw_ref[...]
    acc = gw[:, 0:1] * buf[0]
    for kslot in range(1, TOP_K):
        acc = acc + gw[:, kslot:kslot + 1] * buf[kslot]
    o_ref[...] = x_ref[...] + g2_ref[...] * acc


def combine(pos_flat, x, gw, g2_row, ys, tm=TOK_BLOCK):
    s, d = x.shape
    return pl.pallas_call(
        _combine_body,
        grid=(s // tm,),
        in_specs=[pl.BlockSpec((tm * TOP_K,), lambda i: (i,), memory_space=pltpu.SMEM),
                  pl.BlockSpec((tm, d), lambda i: (i, 0)),
                  pl.BlockSpec((tm, LANES), lambda i: (i, 0)),
                  pl.BlockSpec((1, d), lambda i: (0, 0)),
                  pl.BlockSpec(memory_space=pl.ANY)],
        out_specs=pl.BlockSpec((tm, d), lambda i: (i, 0)),
        out_shape=jax.ShapeDtypeStruct((s, d), F32),
        scratch_shapes=[pltpu.VMEM((TOP_K, tm, d), F32), pltpu.SemaphoreType.DMA((1,))],
        compiler_params=_params("arbitrary"),
        name="moe_combine",
    )(pos_flat, x, gw, g2_row.reshape(1, d), ys)


def moe_layer(x, gain, shift, scale, g2_row, router_w, router_b, w_gu, b_gu, w_dn, b_dn):
    s, d = x.shape
    e = router_w.shape[1]
    hf, logits = router(x, gain, shift, scale, router_w, router_b)
    eidx, gw, rk, counts = topk_route(logits)
    cnt = counts[0, :e].astype(jnp.int32)
    padded = (cnt + MOE_ROWS - 1) // MOE_ROWS * MOE_ROWS
    padded_end = jnp.cumsum(padded)
    padded_start = jnp.pad((padded_end - padded).astype(F32), (0, LANES - e)).reshape(1, LANES)
    n_rows = s * TOP_K + e * MOE_ROWS
    blk_first_row = jnp.arange(n_rows // MOE_ROWS, dtype=jnp.int32) * MOE_ROWS
    blk_expert = jnp.minimum(
        jnp.sum((padded_end[None, :] <= blk_first_row[:, None]).astype(jnp.int32), axis=1), e - 1)
    total_rows = padded_end[-1:].astype(jnp.int32)
    pos = slot_positions(eidx, rk, padded_start)
    pos_flat = pos[:, :TOP_K].reshape(-1)
    xs = dispatch(pos_flat, hf, n_rows)
    ys = expert_ffn(blk_expert, total_rows, xs, w_gu.astype(BF16), b_gu, w_dn.astype(BF16), b_dn)
    return combine(pos_flat, x, gw, g2_row, ys)


def kernel(x, c, positions, norm_gain, ada_w, ada_b, mlstm_w_in, mlstm_b_gates, mlstm_norm_gain, mlstm_w_out, rwkv_mu, rwkv_w_rkv, rwkv_w0, rwkv_w1, rwkv_w2, rwkv_a0, rwkv_a1, rwkv_a2, rwkv_g1, rwkv_g2, rwkv_k_k, rwkv_k_a, rwkv_r_k, rwkv_ln_gain, rwkv_ln_bias, rwkv_w_o, diff_w_qkv, diff_lambda, diff_subln_gain, diff_w_o, moe_router_w, moe_router_b, moe_w_gate_up, moe_b_gate_up, moe_w_down, moe_b_down, final_gain):
    b, s, d = x.shape
    assert b == 1
    depth = norm_gain.shape[0]
    xt = x.reshape(s, d)
    pos = positions.reshape(s)
    mod = ada_mod(c, ada_w, ada_b).reshape(depth, 6, d)
    seen = [0, 0, 0]
    for layer in range(depth):
        kind = layer % 3
        j = seen[kind]
        seen[kind] += 1
        sh1, sc1, g1, sh2, sc2, g2 = (mod[layer, i] for i in range(6))
        gain = norm_gain[layer, 0]
        if kind == 0:
            xt = mlstm_mixer(xt, gain, sh1, sc1, mlstm_w_in[j], mlstm_b_gates[j], mlstm_norm_gain[j],
                             mlstm_w_out[j], g1)
        elif kind == 1:
            xt = rwkv_mixer(xt, gain, sh1, sc1, rwkv_mu[j], rwkv_w_rkv[j], rwkv_w0[j], rwkv_w1[j], rwkv_w2[j],
                            rwkv_a0[j], rwkv_a1[j], rwkv_a2[j], rwkv_g1[j], rwkv_g2[j], rwkv_k_k[j],
                            rwkv_k_a[j], rwkv_r_k[j], rwkv_ln_gain[j], rwkv_ln_bias[j], rwkv_w_o[j], g1)
        else:
            xt = diff_mixer(xt, gain, sh1, sc1, pos, diff_w_qkv[j], diff_lambda[j], diff_subln_gain[j],
                            diff_w_o[j], layer, g1)
        xt = moe_layer(xt, norm_gain[layer, 1], sh2, sc2, g2, moe_router_w[layer], moe_router_b[layer],
                       moe_w_gate_up[layer], moe_b_gate_up[layer], moe_w_down[layer], moe_b_down[layer])
    zeros = jnp.zeros((d,), F32)
    return norm_mod(xt, final_gain, zeros, zeros, F32).reshape(b, s, d)
```

```python
import functools
import math

import numpy as np
import jax
import jax.numpy as jnp
from jax import lax
from jax.experimental import pallas as pl
from jax.experimental.pallas import tpu as pltpu

F32 = jnp.float32
BF16 = jnp.bfloat16
HIGHEST = lax.Precision.HIGHEST

LANES = 128
VMEM_LIMIT = 52 * 1024 * 1024

NORM_EPS = 1e-6
MLSTM_HEADS = 4
MLSTM_CHUNK = 128
GATE_SOFTCAP = 15.0
RWKV_HEAD = 64
RWKV_GN_EPS = 64e-5
DIFF_HEADS = 8
DIFF_DQK = 128
DIFF_DV = 256
DIFF_SUBLN_EPS = 1e-5
ROPE_DIM = 32
ROPE_THETA = 500000.0
N_EXPERTS = 32
TOP_K = 4
SWIGLU_LIMIT = 7.0
SWIGLU_ALPHA = 1.702
MOE_ROWS = 256
TOK_BLOCK = 256


def _params(*sem):
    return pltpu.CompilerParams(dimension_semantics=sem, vmem_limit_bytes=VMEM_LIMIT)


def _bdot(a, b):
    return jnp.dot(a.astype(BF16), b.astype(BF16), preferred_element_type=F32)


def _dot_nt(a, b):
    return lax.dot_general(a.astype(BF16), b.astype(BF16), (((1,), (1,)), ((), ())),
                           preferred_element_type=F32)


def _dot_tn(a, b):
    return lax.dot_general(a.astype(BF16), b.astype(BF16), (((0,), (0,)), ((), ())),
                           preferred_element_type=F32)


def _ada_body(c_ref, w_ref, b_ref, o_ref, *, kc):
    d = w_ref.shape[1]

    def step(i, acc):
        cv = c_ref[pl.ds(i * kc, kc), :]
        sc = cv * jax.nn.sigmoid(cv)
        return acc + jnp.sum(sc * w_ref[0, pl.ds(i * kc, kc), :], axis=0, keepdims=True)

    acc = lax.fori_loop(0, d // kc, step, jnp.zeros((1, w_ref.shape[2]), F32))
    o_ref[0] = acc + b_ref[0]


def ada_mod(c, ada_w, ada_b, tn=1024, kc=128):
    depth, d, n = ada_w.shape
    return pl.pallas_call(
        functools.partial(_ada_body, kc=kc),
        grid=(depth, n // tn),
        in_specs=[pl.BlockSpec((d, 1), lambda l, j: (0, 0)),
                  pl.BlockSpec((1, d, tn), lambda l, j: (l, 0, j)),
                  pl.BlockSpec((1, 1, tn), lambda l, j: (l, 0, j))],
        out_specs=pl.BlockSpec((1, 1, tn), lambda l, j: (l, 0, j)),
        out_shape=jax.ShapeDtypeStruct((depth, 1, n), F32),
        compiler_params=_params("arbitrary", "arbitrary"),
        name="ada_mod",
    )(c.reshape(d, 1), ada_w, ada_b.reshape(depth, 1, n))


def _norm_rows(x, gain, shift, scale, eps):
    y = x * lax.rsqrt(jnp.mean(x * x, axis=-1, keepdims=True) + eps)
    return (y * gain) * (1.0 + scale) + shift


def _norm_mod_body(x_ref, g_ref, sh_ref, sc_ref, o_ref):
    o_ref[...] = _norm_rows(x_ref[...], g_ref[...], sh_ref[...], sc_ref[...], NORM_EPS).astype(o_ref.dtype)


def norm_mod(x, gain, shift, scale, out_dtype, tm=256):
    s, d = x.shape
    row = pl.BlockSpec((1, d), lambda i: (0, 0))
    return pl.pallas_call(
        _norm_mod_body,
        grid=(s // tm,),
        in_specs=[pl.BlockSpec((tm, d), lambda i: (i, 0)), row, row, row],
        out_specs=pl.BlockSpec((tm, d), lambda i: (i, 0)),
        out_shape=jax.ShapeDtypeStruct((s, d), out_dtype),
        compiler_params=_params("arbitrary"),
        name="norm_mod",
    )(x, gain.reshape(1, d), shift.reshape(1, d), scale.reshape(1, d))


def _mm_body(*refs, fused):
    if fused:
        x_ref, w_ref, res_ref, gate_ref, o_ref, wb_ref = refs
    else:
        x_ref, w_ref, o_ref, wb_ref = refs

    @pl.when(pl.program_id(1) == 0)
    def _():
        wb_ref[...] = w_ref[...].astype(BF16)

    acc = jnp.dot(x_ref[...].astype(BF16), wb_ref[...], preferred_element_type=F32)
    if fused:
        acc = res_ref[...] + gate_ref[...] * acc
    o_ref[...] = acc.astype(o_ref.dtype)


def matmul(x, w, out_dtype=F32, res=None, gate=None, tm=512, tn=1024):
    m, k = x.shape
    n = w.shape[1]
    tn = min(tn, n)
    tm = min(tm, m)
    fused = res is not None
    in_specs = [pl.BlockSpec((tm, k), lambda j, i: (i, 0)),
                pl.BlockSpec((k, tn), lambda j, i: (0, j))]
    args = [x, w]
    if fused:
        in_specs += [pl.BlockSpec((tm, tn), lambda j, i: (i, j)),
                     pl.BlockSpec((1, tn), lambda j, i: (0, j))]
        args += [res, gate.reshape(1, n)]
    return pl.pallas_call(
        functools.partial(_mm_body, fused=fused),
        grid=(n // tn, m // tm),
        in_specs=in_specs,
        out_specs=pl.BlockSpec((tm, tn), lambda j, i: (i, j)),
        out_shape=jax.ShapeDtypeStruct((m, n), out_dtype),
        scratch_shapes=[pltpu.VMEM((k, tn), BF16)],
        compiler_params=_params("arbitrary", "arbitrary"),
        name="matmul",
    )(*args)


def _mlstm_body(q_ref, k_ref, v_ref, o_ref, gt_ref, bg_ref, ng_ref, out_ref, c_s, n_s, m_s):
    h = pl.program_id(0)
    L = q_ref.shape[0]
    dk = q_ref.shape[1]
    dv = v_ref.shape[1]

    @pl.when(pl.program_id(1) == 0)
    def _():
        c_s[...] = jnp.zeros_like(c_s)
        n_s[...] = jnp.zeros_like(n_s)
        m_s[...] = jnp.zeros_like(m_s)

    lane = lax.broadcasted_iota(jnp.int32, (L, LANES), 1)
    row = lax.broadcasted_iota(jnp.int32, (L, L), 0)
    col = lax.broadcasted_iota(jnp.int32, (L, L), 1)

    capped = GATE_SOFTCAP * jnp.tanh((gt_ref[...] + bg_ref[...]) / GATE_SOFTCAP)
    li = jnp.sum(jnp.where(lane == h, capped, 0.0), axis=-1, keepdims=True)
    fpre = jnp.sum(jnp.where(lane == h + MLSTM_HEADS, capped, 0.0), axis=-1, keepdims=True)
    lf = jax.nn.log_sigmoid(fpre)

    tril = (row >= col).astype(F32)
    g_b = jnp.dot(tril, jnp.broadcast_to(lf, (L, L)), precision=HIGHEST,
                  preferred_element_type=F32)
    g = g_b[:, 0:1]
    g_tot = jnp.sum(lf, axis=0, keepdims=True)
    eye = row == col
    g_row = jnp.sum(jnp.where(eye, g_b, 0.0), axis=0, keepdims=True)
    li_row = jnp.sum(jnp.where(eye, jnp.broadcast_to(li, (L, L)), 0.0), axis=0, keepdims=True)

    m_prev = m_s[...]
    c_prev = c_s[...]
    n_prev = n_s[...]

    q = q_ref[...] * (dk ** -0.5)
    k = k_ref[...]
    vb = v_ref[...].astype(BF16)
    qb = q.astype(BF16)

    d_log = jnp.where(row >= col, g_b - g_row + li_row, -jnp.inf)
    inter_log = g + m_prev
    m_t = jnp.maximum(inter_log, jnp.max(d_log, axis=-1, keepdims=True))
    s = _dot_nt(qb, k) * jnp.exp(d_log - m_t)
    inter = jnp.exp(inter_log - m_t)
    num = _bdot(s, vb) + inter * _bdot(qb, c_prev)
    den = jnp.sum(s, axis=-1, keepdims=True) + inter * jnp.sum(q * n_prev, axis=-1, keepdims=True)
    hh = num / jnp.maximum(jnp.abs(den), jnp.exp(-m_t))
    hh = hh * lax.rsqrt(jnp.mean(hh * hh, axis=-1, keepdims=True) + NORM_EPS) * ng_ref[...]
    out_ref[...] = (jax.nn.sigmoid(o_ref[...]) * hh).astype(out_ref.dtype)

    w_state = g_tot - g + li
    m_loc = jnp.max(w_state, axis=0, keepdims=True)
    ks = jnp.exp(w_state - m_loc) * k
    kv = _dot_tn(ks, vb)
    n_c = jnp.sum(ks, axis=0, keepdims=True)
    m_new = jnp.maximum(g_tot + m_prev, m_loc)
    a = jnp.exp(g_tot + m_prev - m_new)
    b = jnp.exp(m_loc - m_new)
    c_s[...] = a * c_prev + b * kv
    n_s[...] = a * n_prev + b * n_c
    m_s[...] = m_new


def mlstm_cells(proj, gates, b_gates_pad, norm_gain):
    s = proj.shape[0]
    d = norm_gain.shape[0]
    H, L = MLSTM_HEADS, MLSTM_CHUNK
    dk = (proj.shape[1] - 2 * d) // (2 * H)
    dv = d // H
    kb, vb, ob = H, (2 * H * dk) // dv, (2 * H * dk + d) // dv
    return pl.pallas_call(
        _mlstm_body,
        grid=(H, s // L),
        in_specs=[pl.BlockSpec((L, dk), lambda h, c: (c, h)),
                  pl.BlockSpec((L, dk), lambda h, c: (c, kb + h)),
                  pl.BlockSpec((L, dv), lambda h, c: (c, vb + h)),
                  pl.BlockSpec((L, dv), lambda h, c: (c, ob + h)),
                  pl.BlockSpec((L, LANES), lambda h, c: (c, 0)),
                  pl.BlockSpec((1, LANES), lambda h, c: (0, 0)),
                  pl.BlockSpec((1, dv), lambda h, c: (0, h))],
        out_specs=pl.BlockSpec((L, dv), lambda h, c: (c, h)),
        out_shape=jax.ShapeDtypeStruct((s, d), BF16),
        scratch_shapes=[pltpu.VMEM((dk, dv), F32), pltpu.VMEM((1, dk), F32), pltpu.VMEM((1, 1), F32)],
        compiler_params=_params("arbitrary", "arbitrary"),
        name="mlstm_cells",
    )(proj, proj, proj, proj, gates, b_gates_pad, norm_gain.reshape(1, d))


def mlstm_mixer(x, gain, shift, scale, w_in, b_gates, norm_gain, w_out, gate_row):
    d = x.shape[1]
    n_main = w_in.shape[1] - 2 * MLSTM_HEADS
    hm = norm_mod(x, gain, shift, scale, BF16)
    proj = matmul(hm, w_in[:, :n_main])
    w_g = jnp.pad(w_in[:, n_main:], ((0, 0), (0, LANES - 2 * MLSTM_HEADS)))
    gates = matmul(hm, w_g)
    b_pad = jnp.pad(b_gates, (0, LANES - 2 * MLSTM_HEADS)).reshape(1, LANES)
    cells = mlstm_cells(proj, gates, b_pad, norm_gain)
    return matmul(cells, w_out, res=x, gate=gate_row)


def _rwkv_prep_body(x_ref, xp_ref, g_ref, sh_ref, sc_ref, mu_ref, *outs):
    tm = x_ref.shape[0]
    hm = _norm_rows(x_ref[...], g_ref[...], sh_ref[...], sc_ref[...], NORM_EPS)
    prev = _norm_rows(xp_ref[7:8, :], g_ref[...], sh_ref[...], sc_ref[...], NORM_EPS)
    prev = jnp.where(pl.program_id(0) == 0, 0.0, prev)
    rows = lax.broadcasted_iota(jnp.int32, hm.shape, 0)
    shifted = jnp.where(rows == 0, prev, pltpu.roll(hm, 1, axis=0))
    xx = shifted - hm
    for j, o_ref in enumerate(outs):
        o_ref[...] = (hm + xx * mu_ref[j:j + 1, :]).astype(o_ref.dtype)


def rwkv_prep(x, gain, shift, scale, mu, tm=256):
    s, d = x.shape
    row = pl.BlockSpec((1, d), lambda i: (0, 0))
    blk = pl.BlockSpec((tm, d), lambda i: (i, 0))
    return pl.pallas_call(
        _rwkv_prep_body,
        grid=(s // tm,),
        in_specs=[blk, pl.BlockSpec((8, d), lambda i: (jnp.maximum(i * (tm // 8) - 1, 0), 0)),
                  row, row, row, pl.BlockSpec((8, d), lambda i: (0, 0))],
        out_specs=[blk] * 6,
        out_shape=[jax.ShapeDtypeStruct((s, d), BF16)] * 6,
        compiler_params=_params("arbitrary"),
        name="rwkv_prep",
    )(x, x, gain.reshape(1, d), shift.reshape(1, d), scale.reshape(1, d), jnp.pad(mu, ((0, 2), (0, 0))))


def _lora_body(x_ref, w1_ref, w2_ref, o_ref, *, act):
    t = _bdot(x_ref[...], w1_ref[...])
    if act == "tanh":
        t = jnp.tanh(t)
    elif act == "sigmoid":
        t = jax.nn.sigmoid(t)
    o_ref[...] = _bdot(t, w2_ref[...])


def lora(x, w1, w2, act, tm=512):
    s, d = x.shape
    r = w1.shape[1]
    rp = -(-r // LANES) * LANES
    w1p = jnp.pad(w1, ((0, 0), (0, rp - r)))
    w2p = jnp.pad(w2, ((0, rp - r), (0, 0)))
    n = w2.shape[1]
    return pl.pallas_call(
        functools.partial(_lora_body, act=act),
        grid=(s // tm,),
        in_specs=[pl.BlockSpec((tm, d), lambda i: (i, 0)),
                  pl.BlockSpec((d, rp), lambda i: (0, 0)),
                  pl.BlockSpec((rp, n), lambda i: (0, 0))],
        out_specs=pl.BlockSpec((tm, n), lambda i: (i, 0)),
        out_shape=jax.ShapeDtypeStruct((s, n), F32),
        compiler_params=_params("arbitrary"),
        name="lora",
    )(x, w1p, w2p)


def _split3(x):
    hi = x.astype(BF16)
    r1 = x - hi.astype(F32)
    mid = r1.astype(BF16)
    lo = (r1 - mid.astype(F32)).astype(BF16)
    return hi, mid, lo


def _segsum(x, bmat, passes):
    parts = _split3(x)[:passes] if passes > 1 else (x.astype(BF16),)
    acc = jnp.dot(parts[0], bmat, preferred_element_type=F32)
    for p in parts[1:]:
        acc = acc + jnp.dot(p, bmat, preferred_element_type=F32)
    return acc


SUBLANES = 8


SCAN_W = 256


def _head_ones(width):
    ri = lax.broadcasted_iota(jnp.int32, (width, width), 0)
    ci = lax.broadcasted_iota(jnp.int32, (width, width), 1)
    return ((ri // RWKV_HEAD) == (ci // RWKV_HEAD)).astype(BF16)


def _rwkv_pre_body(k_ref, wl_ref, al_ref, w0_ref, a0_ref, kk_ref, ka_ref, nkk_ref, w_ref, kka_ref, kp_ref):
    d = k_ref.shape[1]
    bmat = _head_ones(SCAN_W)
    for j in range(d // SCAN_W):
        cs = slice(j * SCAN_W, (j + 1) * SCAN_W)
        kraw = k_ref[:, cs]
        w_log = -jax.nn.softplus(-(w0_ref[:, cs] + wl_ref[:, cs])) - 0.5
        w_ref[:, cs] = jnp.exp(-jnp.exp(w_log))
        a = jax.nn.sigmoid(a0_ref[:, cs] + al_ref[:, cs])
        kk = kraw * kk_ref[:, cs]
        nrm = jnp.sqrt(_segsum(kk * kk, bmat, 3))
        kk = kk / jnp.maximum(nrm, 1e-12)
        nkk_ref[:, cs] = -kk
        kka_ref[:, cs] = kk * a
        kp_ref[:, cs] = kraw * (1.0 + (a - 1.0) * ka_ref[:, cs])


def rwkv_pre(k, wl, al, w0, a0, k_k, k_a, tm=256):
    s, d = k.shape
    blk = pl.BlockSpec((tm, d), lambda i: (i, 0))
    row = pl.BlockSpec((1, d), lambda i: (0, 0))
    return pl.pallas_call(
        _rwkv_pre_body,
        grid=(s // tm,),
        in_specs=[blk] * 3 + [row] * 4,
        out_specs=[blk] * 4,
        out_shape=[jax.ShapeDtypeStruct((s, d), F32)] * 4,
        compiler_params=_params("arbitrary"),
        name="rwkv_pre",
    )(k, wl, al, *[t.reshape(1, d) for t in (w0, a0, k_k, k_a)])


def _rwkv_post_body(y_ref, r_ref, kp_ref, v_ref, gt_ref, rk_ref, lg_ref, lb_ref, o_ref):
    d = y_ref.shape[1]
    bmat = _head_ones(SCAN_W)
    inv = 1.0 / RWKV_HEAD
    for j in range(d // SCAN_W):
        cs = slice(j * SCAN_W, (j + 1) * SCAN_W)
        y = y_ref[:, cs]
        mean = _segsum(y, bmat, 3) * inv
        yc = y - mean
        var = _segsum(yc * yc, bmat, 3) * inv
        yn = yc * lax.rsqrt(var + RWKV_GN_EPS) * lg_ref[:, cs] + lb_ref[:, cs]
        bonus = _segsum(r_ref[:, cs] * kp_ref[:, cs] * rk_ref[:, cs], bmat, 3) * v_ref[:, cs]
        o_ref[:, cs] = ((yn + bonus) * gt_ref[:, cs]).astype(o_ref.dtype)


def rwkv_post(y, r, kp, v, gt, r_k, ln_gain, ln_bias, tm=256):
    s, d = y.shape
    blk = pl.BlockSpec((tm, d), lambda i: (i, 0))
    row = pl.BlockSpec((1, d), lambda i: (0, 0))
    return pl.pallas_call(
        _rwkv_post_body,
        grid=(s // tm,),
        in_specs=[blk] * 5 + [row] * 3,
        out_specs=blk,
        out_shape=jax.ShapeDtypeStruct((s, d), BF16),
        compiler_params=_params("arbitrary"),
        name="rwkv_post",
    )(y, r, kp, v, gt, *[t.reshape(1, d) for t in (r_k, ln_gain, ln_bias)])


def _rwkv_scan_body(nkk_ref, w_ref, kka_ref, kp_ref, r_ref, v_ref, y_ref, st_s, b_s, m1_s,
                    nkk3, w3, kka3, kp3, r3, v3, y3, u3):
    tb, d = r_ref.shape
    nv = RWKV_HEAD
    ntile = d // SCAN_W
    ngrp = tb // SUBLANES
    nslab = nv // SUBLANES
    slab = ntile * SUBLANES
    rows = nslab * slab

    @pl.when(pl.program_id(0) == 0)
    def _():
        st_s[...] = jnp.zeros_like(st_s)
        b_s[...] = _head_ones(SCAN_W)
        vi = lax.broadcasted_iota(jnp.int32, (rows, SCAN_W), 0)
        li = lax.broadcasted_iota(jnp.int32, (rows, SCAN_W), 1)
        v_idx = (vi // slab) * SUBLANES + vi % SUBLANES
        m1_s[...] = (v_idx == (li % nv)).astype(F32)

    def replicate(g, carry):
        base = pl.multiple_of(g * SUBLANES, SUBLANES)
        for src, dst in ((nkk_ref, nkk3), (w_ref, w3), (kka_ref, kka3), (kp_ref, kp3), (r_ref, r3), (v_ref, v3)):
            for j in range(ntile):
                grp = src[pl.ds(base, SUBLANES), j * SCAN_W:(j + 1) * SCAN_W]
                for i in range(SUBLANES):
                    dst[base + i, j * SUBLANES:(j + 1) * SUBLANES, :] = jnp.broadcast_to(
                        grp[i:i + 1, :], (SUBLANES, SCAN_W))
        return carry

    lax.fori_loop(0, ngrp, replicate, 0)

    def tall(ref, t):
        return jnp.concatenate([ref[t]] * nslab, axis=0)

    def outer(t, carry):
        vcol = jnp.dot((m1_s[...] * tall(v3, t)).astype(BF16), b_s[...], preferred_element_type=F32)
        u3[t] = vcol * tall(kp3, t)
        return carry

    lax.fori_loop(0, tb, outer, 0, unroll=4)

    def step(t, carry):
        st = st_s[...]
        sa = jnp.dot((st * tall(nkk3, t)).astype(BF16), b_s[...], preferred_element_type=F32)
        sn = st * tall(w3, t) + sa * tall(kka3, t) + u3[t]
        st_s[...] = sn
        part = m1_s[...] * jnp.dot((sn * tall(r3, t)).astype(BF16), b_s[...], preferred_element_type=F32)
        acc = part[0:slab]
        for i in range(1, nslab):
            acc = acc + part[i * slab:(i + 1) * slab]
        y3[t] = acc
        return carry

    lax.fori_loop(0, tb, step, 0, unroll=4)

    def collect(g, carry):
        base = pl.multiple_of(g * SUBLANES, SUBLANES)
        for j in range(ntile):
            got = [jnp.sum(y3[base + i, j * SUBLANES:(j + 1) * SUBLANES, :], axis=0, keepdims=True)
                   for i in range(SUBLANES)]
            y_ref[pl.ds(base, SUBLANES), j * SCAN_W:(j + 1) * SCAN_W] = jnp.concatenate(got, axis=0)
        return carry

    lax.fori_loop(0, ngrp, collect, 0)


def rwkv_scan(nkk, w, kka, kp, r, v, tb=32):
    s, d = r.shape
    blk = pl.BlockSpec((tb, d), lambda i: (i, 0))
    slab = (d // SCAN_W) * SUBLANES
    state_rows = (RWKV_HEAD // SUBLANES) * slab
    return pl.pallas_call(
        _rwkv_scan_body,
        grid=(s // tb,),
        in_specs=[blk] * 6,
        out_specs=blk,
        out_shape=jax.ShapeDtypeStruct((s, d), F32),
        scratch_shapes=[pltpu.VMEM((state_rows, SCAN_W), F32), pltpu.VMEM((SCAN_W, SCAN_W), BF16),
                        pltpu.VMEM((state_rows, SCAN_W), F32)]
                       + [pltpu.VMEM((tb, slab, SCAN_W), F32)] * 7
                       + [pltpu.VMEM((tb, state_rows, SCAN_W), F32)],
        compiler_params=_params("arbitrary"),
        name="rwkv_scan",
    )(nkk, w, kka, kp, r, v)


def rwkv_mixer(x, gain, shift, scale, mu, w_rkv, w0, w1, w2, a0, a1, a2, g1, g2, k_k, k_a, r_k,
               ln_gain, ln_bias, w_o, gate_row):
    xr, xk, xv, xw, xa, xg = rwkv_prep(x, gain, shift, scale, mu)
    r = matmul(xr, w_rkv[0])
    k = matmul(xk, w_rkv[1])
    v = matmul(xv, w_rkv[2])
    wl = lora(xw, w1, w2, "tanh")
    al = lora(xa, a1, a2, "none")
    gt = lora(xg, g1, g2, "sigmoid")
    nkk, w, kka, kp = rwkv_pre(k, wl, al, w0, a0, k_k, k_a)
    y = rwkv_scan(nkk, w, kka, kp, r, v)
    y = rwkv_post(y, r, kp, v, gt, r_k.reshape(-1), ln_gain, ln_bias)
    return matmul(y, w_o, res=x, gate=gate_row)


def _rope_body(p_ref, pos_ref, o_ref, *, n_rot_tiles, q_tiles):
    tm, n = p_ref.shape
    half = ROPE_DIM // 2
    lane = lax.broadcasted_iota(jnp.int32, (tm, LANES), 1)
    inv_freq = jnp.exp((lane % half).astype(F32) * (-2.0 * math.log(ROPE_THETA) / ROPE_DIM))
    ang = pos_ref[...].astype(F32) * inv_freq
    roped = lane < ROPE_DIM
    cos = jnp.where(roped, jnp.cos(ang), 1.0)
    sin = jnp.where(roped, jnp.where(lane < half, -jnp.sin(ang), jnp.sin(ang)), 0.0)
    for j in range(n // LANES):
        x = p_ref[:, j * LANES:(j + 1) * LANES]
        if j < n_rot_tiles:
            partner = jnp.where(lane < half, pltpu.roll(x, LANES - half, axis=1), pltpu.roll(x, half, axis=1))
            x = x * cos + partner * sin
            if j < q_tiles:
                x = x * (DIFF_DQK ** -0.5)
        o_ref[:, j * LANES:(j + 1) * LANES] = x.astype(o_ref.dtype)


def rope_cast(proj, positions, tm=256):
    s, n = proj.shape
    q_tiles = 2 * DIFF_HEADS
    return pl.pallas_call(
        functools.partial(_rope_body, n_rot_tiles=2 * q_tiles, q_tiles=q_tiles),
        grid=(s // tm,),
        in_specs=[pl.BlockSpec((tm, n), lambda i: (i, 0)),
                  pl.BlockSpec((tm, 1), lambda i: (i, 0))],
        out_specs=pl.BlockSpec((tm, n), lambda i: (i, 0)),
        out_shape=jax.ShapeDtypeStruct((s, n), BF16),
        compiler_params=_params("arbitrary"),
        name="rope_cast",
    )(proj, positions.reshape(s, 1))


def _diff_attn_body(qi_ref, kj_ref, q_ref, k_ref, v_ref, lam_ref, sg_ref, o_ref, m_s, l_s, acc_s, *, lambda_init):
    step = pl.program_id(1)
    qi, kj = qi_ref[step], kj_ref[step]
    tb = q_ref.shape[0]

    @pl.when(kj == 0)
    def _():
        m_s[...] = jnp.full_like(m_s, -jnp.inf)
        l_s[...] = jnp.zeros_like(l_s)
        acc_s[...] = jnp.zeros_like(acc_s)

    def accumulate(diagonal):
        v = v_ref[...]
        for c in range(2):
            cs = slice(c * DIFF_DQK, (c + 1) * DIFF_DQK)
            s = _dot_nt(q_ref[:, cs], k_ref[:, cs])
            if diagonal:
                rows = lax.broadcasted_iota(jnp.int32, (tb, tb), 0)
                cols = lax.broadcasted_iota(jnp.int32, (tb, tb), 1)
                s = jnp.where(cols <= rows, s, -jnp.inf)
            m_old = m_s[c]
            m_new = jnp.maximum(m_old, jnp.max(s, axis=-1, keepdims=True))
            alpha = jnp.exp(m_old - m_new)
            p = jnp.exp(s - m_new)
            l_s[c] = alpha * l_s[c] + jnp.sum(p, axis=-1, keepdims=True)
            acc_s[c] = alpha * acc_s[c] + _bdot(p, v)
            m_s[c] = m_new

    @pl.when(kj < qi)
    def _():
        accumulate(False)

    @pl.when(kj == qi)
    def _():
        accumulate(True)
        lp = lam_ref[...]
        lam = (jnp.exp(jnp.sum(lp[0:1] * lp[1:2], axis=-1, keepdims=True))
               - jnp.exp(jnp.sum(lp[2:3] * lp[3:4], axis=-1, keepdims=True)) + lambda_init)
        out = acc_s[0] / l_s[0] - lam * (acc_s[1] / l_s[1])
        out = out * lax.rsqrt(jnp.mean(out * out, axis=-1, keepdims=True) + DIFF_SUBLN_EPS)
        o_ref[...] = (out * sg_ref[...] * (1.0 - lambda_init)).astype(o_ref.dtype)


def diff_attention(qkv, lam_params, subln_gain, lambda_init, tb=1024):
    s = qkv.shape[0]
    H, dv = DIFF_HEADS, DIFF_DV
    tb = min(tb, s)
    nb = s // tb
    pairs = [(qi, kj) for qi in range(nb) for kj in range(qi + 1)]
    qi_tab = jnp.asarray([p[0] for p in pairs], jnp.int32)
    kj_tab = jnp.asarray([p[1] for p in pairs], jnp.int32)
    grid_spec = pltpu.PrefetchScalarGridSpec(
        num_scalar_prefetch=2,
        grid=(H, len(pairs)),
        in_specs=[pl.BlockSpec((tb, dv), lambda h, t, qi, kj: (qi[t], h)),
                  pl.BlockSpec((tb, dv), lambda h, t, qi, kj: (kj[t], H + h)),
                  pl.BlockSpec((tb, dv), lambda h, t, qi, kj: (kj[t], 2 * H + h)),
                  pl.BlockSpec((4, DIFF_DQK), lambda h, t, qi, kj: (0, 0)),
                  pl.BlockSpec((1, dv), lambda h, t, qi, kj: (0, 0))],
        out_specs=pl.BlockSpec((tb, dv), lambda h, t, qi, kj: (qi[t], h)),
        scratch_shapes=[pltpu.VMEM((2, tb, 1), F32), pltpu.VMEM((2, tb, 1), F32),
                        pltpu.VMEM((2, tb, dv), F32)],
    )
    return pl.pallas_call(
        functools.partial(_diff_attn_body, lambda_init=lambda_init),
        grid_spec=grid_spec,
        out_shape=jax.ShapeDtypeStruct((s, H * dv), BF16),
        compiler_params=_params("arbitrary", "arbitrary"),
        name="diff_attention",
    )(qi_tab, kj_tab, qkv, qkv, qkv, lam_params, subln_gain.reshape(1, dv))


def diff_mixer(x, gain, shift, scale, positions, w_qkv, lam_params, subln_gain, w_o, layer, gate_row):
    hm = norm_mod(x, gain, shift, scale, BF16)
    proj = matmul(hm, w_qkv)
    qkv = rope_cast(proj, positions)
    lambda_init = 0.8 - 0.6 * math.exp(-0.3 * layer)
    att = diff_attention(qkv, lam_params, subln_gain, lambda_init)
    return matmul(att, w_o, res=x, gate=gate_row)


def _router_body(x_ref, g_ref, sh_ref, sc_ref, rw_ref, rb_ref, hf_ref, lg_ref):
    hf = _norm_rows(x_ref[...], g_ref[...], sh_ref[...], sc_ref[...], NORM_EPS)
    hf_ref[...] = hf
    lg_ref[...] = jnp.dot(hf, rw_ref[...], precision=HIGHEST, preferred_element_type=F32) + rb_ref[...]


def router(x, gain, shift, scale, router_w, router_b, tm=256):
    s, d = x.shape
    e = router_w.shape[1]
    rw = jnp.pad(router_w, ((0, 0), (0, LANES - e)))
    rb = jnp.pad(router_b, (0, LANES - e), constant_values=-1e30).reshape(1, LANES)
    row = pl.BlockSpec((1, d), lambda i: (0, 0))
    return pl.pallas_call(
        _router_body,
        grid=(s // tm,),
        in_specs=[pl.BlockSpec((tm, d), lambda i: (i, 0)), row, row, row,
                  pl.BlockSpec((d, LANES), lambda i: (0, 0)), pl.BlockSpec((1, LANES), lambda i: (0, 0))],
        out_specs=[pl.BlockSpec((tm, d), lambda i: (i, 0)), pl.BlockSpec((tm, LANES), lambda i: (i, 0))],
        out_shape=[jax.ShapeDtypeStruct((s, d), F32), jax.ShapeDtypeStruct((s, LANES), F32)],
        compiler_params=_params("arbitrary"),
        name="router",
    )(x, gain.reshape(1, d), shift.reshape(1, d), scale.reshape(1, d), rw, rb)


def _topk_body(lg_ref, eidx_ref, gw_ref, rk_ref, cnt_ref):
    tm = lg_ref.shape[0]

    @pl.when(pl.program_id(0) == 0)
    def _():
        cnt_ref[...] = jnp.zeros_like(cnt_ref)

    lane = lax.broadcasted_iota(jnp.int32, (tm, LANES), 1).astype(F32)
    work = lg_ref[...]
    vals, idxs, hots = [], [], []
    for _ in range(TOP_K):
        m = jnp.max(work, axis=-1, keepdims=True)
        idx = jnp.min(jnp.where(work == m, lane, float(LANES)), axis=-1, keepdims=True)
        hot = lane == idx
        work = jnp.where(hot, -jnp.inf, work)
        vals.append(m)
        idxs.append(idx)
        hots.append(hot)
    exps = [jnp.exp(v - vals[0]) for v in vals]
    den = exps[0] + exps[1] + exps[2] + exps[3]
    sel = jnp.zeros((tm, LANES), F32)
    for hot in hots:
        sel = sel + hot.astype(F32)
    ri = lax.broadcasted_iota(jnp.int32, (tm, tm), 0)
    ci = lax.broadcasted_iota(jnp.int32, (tm, tm), 1)
    before = jnp.dot((ri > ci).astype(BF16), sel.astype(BF16), preferred_element_type=F32)
    rank = before + cnt_ref[...]
    cnt_ref[...] = cnt_ref[...] + jnp.sum(sel, axis=0, keepdims=True)
    eidx = jnp.zeros((tm, LANES), F32)
    gw = jnp.zeros((tm, LANES), F32)
    rk = jnp.zeros((tm, LANES), F32)
    for kslot in range(TOP_K):
        here = lane == kslot
        eidx = jnp.where(here, idxs[kslot], eidx)
        gw = jnp.where(here, exps[kslot] / den, gw)
        rk = jnp.where(here, jnp.sum(jnp.where(hots[kslot], rank, 0.0), axis=-1, keepdims=True), rk)
    eidx_ref[...] = eidx
    gw_ref[...] = gw
    rk_ref[...] = rk


def topk_route(logits, tm=TOK_BLOCK):
    s = logits.shape[0]
    blk = pl.BlockSpec((tm, LANES), lambda i: (i, 0))
    return pl.pallas_call(
        _topk_body,
        grid=(s // tm,),
        in_specs=[blk],
        out_specs=[blk, blk, blk, pl.BlockSpec((1, LANES), lambda i: (0, 0))],
        out_shape=[jax.ShapeDtypeStruct((s, LANES), F32), jax.ShapeDtypeStruct((s, LANES), F32),
                   jax.ShapeDtypeStruct((s, LANES), F32), jax.ShapeDtypeStruct((1, LANES), F32)],
        compiler_params=_params("arbitrary"),
        name="topk_route",
    )(logits)


def _slot_body(eidx_ref, rk_ref, ps_ref, pos_ref):
    tm = eidx_ref.shape[0]
    lane = lax.broadcasted_iota(jnp.int32, (tm, LANES), 1).astype(F32)
    eidx = eidx_ref[...]
    rk = rk_ref[...]
    pos = jnp.zeros((tm, LANES), F32)
    for kslot in range(TOP_K):
        e_k = jnp.sum(jnp.where(lane == kslot, eidx, 0.0), axis=-1, keepdims=True)
        start = jnp.sum(jnp.where(lane == e_k, ps_ref[...], 0.0), axis=-1, keepdims=True)
        r_k = jnp.sum(jnp.where(lane == kslot, rk, 0.0), axis=-1, keepdims=True)
        pos = jnp.where(lane == kslot, start + r_k, pos)
    pos_ref[...] = pos.astype(jnp.int32)


def slot_positions(eidx, rk, padded_start, tm=TOK_BLOCK):
    s = eidx.shape[0]
    blk = pl.BlockSpec((tm, LANES), lambda i: (i, 0))
    return pl.pallas_call(
        _slot_body,
        grid=(s // tm,),
        in_specs=[blk, blk, pl.BlockSpec((1, LANES), lambda i: (0, 0))],
        out_specs=blk,
        out_shape=jax.ShapeDtypeStruct((s, LANES), jnp.int32),
        compiler_params=_params("arbitrary"),
        name="slot_positions",
    )(eidx, rk, padded_start)


def _dispatch_body(pos_ref, hf_ref, xs_in_ref, xs_ref, sem):
    del xs_in_ref
    tm = hf_ref.shape[0]

    def issue(t, carry):
        for kslot in range(TOP_K):
            p = pos_ref[t * TOP_K + kslot]
            pltpu.make_async_copy(hf_ref.at[pl.ds(t, 1)], xs_ref.at[pl.ds(p, 1)], sem.at[0]).start()
        return carry

    lax.fori_loop(0, tm, issue, 0)

    for _ in range(TOP_K):
        pltpu.make_async_copy(hf_ref, xs_ref.at[pl.ds(0, tm)], sem.at[0]).wait()


def dispatch(pos_flat, hf, n_rows, tm=TOK_BLOCK):
    s, d = hf.shape
    return pl.pallas_call(
        _dispatch_body,
        grid=(s // tm,),
        in_specs=[pl.BlockSpec((tm * TOP_K,), lambda i: (i,), memory_space=pltpu.SMEM),
                  pl.BlockSpec((tm, d), lambda i: (i, 0)),
                  pl.BlockSpec(memory_space=pl.ANY)],
        out_specs=pl.BlockSpec(memory_space=pl.ANY),
        out_shape=jax.ShapeDtypeStruct((n_rows, d), hf.dtype),
        scratch_shapes=[pltpu.SemaphoreType.DMA((1,))],
        input_output_aliases={2: 0},
        compiler_params=_params("arbitrary"),
        name="moe_dispatch",
    )(pos_flat, hf, jnp.zeros((n_rows, d), hf.dtype))


def _expert_body(be_ref, first_ref, nxt_ref, tot_ref, xs_ref, wgu_hbm, bgu_ref, wdn_hbm, bdn_ref, perm_ref, ys_ref,
                 stg_gu, stg_dn, wgu_s, wdn_s, sem, *, layer, chunk):
    i = pl.program_id(0)
    rows = xs_ref.shape[0]
    d, n_gu = wgu_s.shape
    dh = wdn_s.shape[0]
    grp = perm_ref.shape[0]
    half = grp // 2

    def fetch(e):
        return (pltpu.make_async_copy(wgu_hbm.at[layer, e], stg_gu, sem.at[0]),
                pltpu.make_async_copy(wdn_hbm.at[layer, e], stg_dn, sem.at[1]))

    @pl.when(first_ref[i] == 1)
    def _():
        @pl.when(i == 0)
        def _():
            for cp in fetch(be_ref[0]):
                cp.start()

        for cp in fetch(be_ref[i]):
            cp.wait()
        for c in range(d // chunk):
            wgu_s[c * chunk:(c + 1) * chunk, :] = stg_gu[c * chunk:(c + 1) * chunk, :].astype(BF16)
        for c in range(dh // chunk):
            wdn_s[c * chunk:(c + 1) * chunk, :] = stg_dn[c * chunk:(c + 1) * chunk, :].astype(BF16)

        @pl.when(nxt_ref[i] >= 0)
        def _():
            for cp in fetch(nxt_ref[i]):
                cp.start()

    @pl.when(i * rows < tot_ref[0])
    def _():
        gu = _bdot(xs_ref[...], wgu_s[...]) + bgu_ref[0]
        gub = gu.astype(BF16)
        gs, us = [], []
        for j in range(n_gu // grp):
            pj = jnp.dot(gub[:, j * grp:(j + 1) * grp], perm_ref[...], preferred_element_type=F32)
            gs.append(pj[:, :half])
            us.append(pj[:, half:])
        g = jnp.minimum(jnp.concatenate(gs, axis=1), SWIGLU_LIMIT)
        u = jnp.clip(jnp.concatenate(us, axis=1), -SWIGLU_LIMIT, SWIGLU_LIMIT)
        act = (u + 1.0) * (g * jax.nn.sigmoid(g * SWIGLU_ALPHA))
        ys_ref[...] = _bdot(act, wdn_s[...]) + bdn_ref[0]

    @pl.when(i * rows >= tot_ref[0])
    def _():
        ys_ref[...] = jnp.zeros_like(ys_ref)


def expert_ffn(blk_expert, blk_first, blk_next, total_rows, xs, w_gu, b_gu, w_dn, b_dn, layer, rows=MOE_ROWS):
    n_rows, d = xs.shape
    _, e, _, n_gu = w_gu.shape
    dh = w_dn.shape[2]
    grp = 2 * LANES
    perm = np.zeros((grp, grp), np.float32)
    perm[2 * np.arange(LANES), np.arange(LANES)] = 1.0
    perm[2 * np.arange(LANES) + 1, LANES + np.arange(LANES)] = 1.0
    grid_spec = pltpu.PrefetchScalarGridSpec(
        num_scalar_prefetch=4,
        grid=(n_rows // rows,),
        in_specs=[pl.BlockSpec((rows, d), lambda i, be, fi, nx, tot: (i, 0)),
                  pl.BlockSpec(memory_space=pl.ANY),
                  pl.BlockSpec((1, 1, n_gu), lambda i, be, fi, nx, tot: (be[i], 0, 0)),
                  pl.BlockSpec(memory_space=pl.ANY),
                  pl.BlockSpec((1, 1, d), lambda i, be, fi, nx, tot: (be[i], 0, 0)),
                  pl.BlockSpec((grp, grp), lambda i, be, fi, nx, tot: (0, 0))],
        out_specs=pl.BlockSpec((rows, d), lambda i, be, fi, nx, tot: (i, 0)),
        scratch_shapes=[pltpu.VMEM((d, n_gu), F32), pltpu.VMEM((dh, d), F32),
                        pltpu.VMEM((d, n_gu), BF16), pltpu.VMEM((dh, d), BF16),
                        pltpu.SemaphoreType.DMA((2,))],
    )
    return pl.pallas_call(
        functools.partial(_expert_body, layer=layer, chunk=256),
        grid_spec=grid_spec,
        out_shape=jax.ShapeDtypeStruct((n_rows, d), F32),
        compiler_params=_params("arbitrary"),
        name="moe_experts",
    )(blk_expert, blk_first, blk_next, total_rows, xs, w_gu, b_gu.reshape(e, 1, n_gu), w_dn,
      b_dn.reshape(e, 1, d), jnp.asarray(perm, BF16))


def _combine_body(pos_ref, x_ref, gw_ref, g2_ref, ys_ref, o_ref, buf, sem):
    tm = x_ref.shape[0]

    def issue(t, carry):
        for kslot in range(TOP_K):
            p = pos_ref[t * TOP_K + kslot]
            pltpu.make_async_copy(ys_ref.at[pl.ds(p, 1)], buf.at[kslot, pl.ds(t, 1)], sem.at[0]).start()
        return carry

    lax.fori_loop(0, tm, issue, 0)

    for kslot in range(TOP_K):
        pltpu.make_async_copy(ys_ref.at[pl.ds(0, tm)], buf.at[kslot], sem.at[0]).wait()

    gw = gw_ref[...]
    acc = gw[:, 0:1] * buf[0]
    for kslot in range(1, TOP_K):
        acc = acc + gw[:, kslot:kslot + 1] * buf[kslot]
    o_ref[...] = x_ref[...] + g2_ref[...] * acc


def combine(pos_flat, x, gw, g2_row, ys, tm=TOK_BLOCK):
    s, d = x.shape
    return pl.pallas_call(
        _combine_body,
        grid=(s // tm,),
        in_specs=[pl.BlockSpec((tm * TOP_K,), lambda i: (i,), memory_space=pltpu.SMEM),
                  pl.BlockSpec((tm, d), lambda i: (i, 0)),
                  pl.BlockSpec((tm, LANES), lambda i: (i, 0)),
                  pl.BlockSpec((1, d), lambda i: (0, 0)),
                  pl.BlockSpec(memory_space=pl.ANY)],
        out_specs=pl.BlockSpec((tm, d), lambda i: (i, 0)),
        out_shape=jax.ShapeDtypeStruct((s, d), F32),
        scratch_shapes=[pltpu.VMEM((TOP_K, tm, d), F32), pltpu.SemaphoreType.DMA((1,))],
        compiler_params=_params("arbitrary"),
        name="moe_combine",
    )(pos_flat, x, gw, g2_row.reshape(1, d), ys)


def moe_layer(x, gain, shift, scale, g2_row, router_w, router_b, w_gu, b_gu, w_dn, b_dn, layer):
    s, d = x.shape
    e = router_w.shape[1]
    hf, logits = router(x, gain, shift, scale, router_w, router_b)
    eidx, gw, rk, counts = topk_route(logits)
    cnt = counts[0, :e].astype(jnp.int32)
    padded = (cnt + MOE_ROWS - 1) // MOE_ROWS * MOE_ROWS
    padded_end = jnp.cumsum(padded)
    padded_start = jnp.pad((padded_end - padded).astype(F32), (0, LANES - e)).reshape(1, LANES)
    n_rows = s * TOP_K + e * MOE_ROWS
    blk_first_row = jnp.arange(n_rows // MOE_ROWS, dtype=jnp.int32) * MOE_ROWS
    blk_expert = jnp.minimum(
        jnp.sum((padded_end[None, :] <= blk_first_row[:, None]).astype(jnp.int32), axis=1), e - 1)
    total_rows = padded_end[-1:].astype(jnp.int32)
    n_blk = n_rows // MOE_ROWS
    blk_id = jnp.arange(n_blk, dtype=jnp.int32)
    prev_expert = jnp.concatenate([jnp.full((1,), -1, jnp.int32), blk_expert[:-1]])
    blk_first = ((blk_first_row < total_rows[0]) & (blk_expert != prev_expert)).astype(jnp.int32)
    first_at = jnp.where(blk_first == 1, blk_id, n_blk)
    next_first = jnp.concatenate([lax.cummin(first_at, reverse=True)[1:], jnp.full((1,), n_blk, jnp.int32)])
    blk_next = jnp.where(next_first < n_blk, blk_expert[jnp.minimum(next_first, n_blk - 1)], -1).astype(jnp.int32)
    pos = slot_positions(eidx, rk, padded_start)
    pos_flat = pos[:, :TOP_K].reshape(-1)
    xs = dispatch(pos_flat, hf, n_rows)
    ys = expert_ffn(blk_expert, blk_first, blk_next, total_rows, xs, w_gu, b_gu, w_dn, b_dn, layer)
    return combine(pos_flat, x, gw, g2_row, ys)


def kernel(x, c, positions, norm_gain, ada_w, ada_b, mlstm_w_in, mlstm_b_gates, mlstm_norm_gain, mlstm_w_out, rwkv_mu, rwkv_w_rkv, rwkv_w0, rwkv_w1, rwkv_w2, rwkv_a0, rwkv_a1, rwkv_a2, rwkv_g1, rwkv_g2, rwkv_k_k, rwkv_k_a, rwkv_r_k, rwkv_ln_gain, rwkv_ln_bias, rwkv_w_o, diff_w_qkv, diff_lambda, diff_subln_gain, diff_w_o, moe_router_w, moe_router_b, moe_w_gate_up, moe_b_gate_up, moe_w_down, moe_b_down, final_gain):
    b, s, d = x.shape
    assert b == 1
    depth = norm_gain.shape[0]
    xt = x.reshape(s, d)
    pos = positions.reshape(s)
    mod = ada_mod(c, ada_w, ada_b).reshape(depth, 6, d)
    seen = [0, 0, 0]
    for layer in range(depth):
        kind = layer % 3
        j = seen[kind]
        seen[kind] += 1
        sh1, sc1, g1, sh2, sc2, g2 = (mod[layer, i] for i in range(6))
        gain = norm_gain[layer, 0]
        if kind == 0:
            xt = mlstm_mixer(xt, gain, sh1, sc1, mlstm_w_in[j], mlstm_b_gates[j], mlstm_norm_gain[j],
                             mlstm_w_out[j], g1)
        elif kind == 1:
            xt = rwkv_mixer(xt, gain, sh1, sc1, rwkv_mu[j], rwkv_w_rkv[j], rwkv_w0[j], rwkv_w1[j], rwkv_w2[j],
                            rwkv_a0[j], rwkv_a1[j], rwkv_a2[j], rwkv_g1[j], rwkv_g2[j], rwkv_k_k[j],
                            rwkv_k_a[j], rwkv_r_k[j], rwkv_ln_gain[j], rwkv_ln_bias[j], rwkv_w_o[j], g1)
        else:
            xt = diff_mixer(xt, gain, sh1, sc1, pos, diff_w_qkv[j], diff_lambda[j], diff_subln_gain[j],
                            diff_w_o[j], layer, g1)
        xt = moe_layer(xt, norm_gain[layer, 1], sh2, sc2, g2, moe_router_w[layer], moe_router_b[layer],
                       moe_w_gate_up, moe_b_gate_up[layer], moe_w_down, moe_b_down[layer], layer)
    zeros = jnp.zeros((d,), F32)
    return norm_mod(xt, final_gain, zeros, zeros, F32).reshape(b, s, d)
```

```python
import functools
import math

import numpy as np
import jax
import jax.numpy as jnp
from jax import lax
from jax.experimental import pallas as pl
from jax.experimental.pallas import tpu as pltpu

F32 = jnp.float32
BF16 = jnp.bfloat16
HIGHEST = lax.Precision.HIGHEST

LANES = 128
VMEM_LIMIT = 52 * 1024 * 1024

NORM_EPS = 1e-6
MLSTM_HEADS = 4
MLSTM_CHUNK = 128
GATE_SOFTCAP = 15.0
RWKV_HEAD = 64
RWKV_GN_EPS = 64e-5
DIFF_HEADS = 8
DIFF_DQK = 128
DIFF_DV = 256
DIFF_SUBLN_EPS = 1e-5
ROPE_DIM = 32
ROPE_THETA = 500000.0
N_EXPERTS = 32
TOP_K = 4
SWIGLU_LIMIT = 7.0
SWIGLU_ALPHA = 1.702
MOE_ROWS = 256
TOK_BLOCK = 256


def _params(*sem):
    return pltpu.CompilerParams(dimension_semantics=sem, vmem_limit_bytes=VMEM_LIMIT)


def _bdot(a, b):
    return jnp.dot(a.astype(BF16), b.astype(BF16), preferred_element_type=F32)


def _dot_nt(a, b):
    return lax.dot_general(a.astype(BF16), b.astype(BF16), (((1,), (1,)), ((), ())),
                           preferred_element_type=F32)


def _dot_tn(a, b):
    return lax.dot_general(a.astype(BF16), b.astype(BF16), (((0,), (0,)), ((), ())),
                           preferred_element_type=F32)


def _ada_body(c_ref, w_ref, b_ref, o_ref, *, kc):
    d = w_ref.shape[1]

    def step(i, acc):
        cv = c_ref[pl.ds(i * kc, kc), :]
        sc = cv * jax.nn.sigmoid(cv)
        return acc + jnp.sum(sc * w_ref[0, pl.ds(i * kc, kc), :], axis=0, keepdims=True)

    acc = lax.fori_loop(0, d // kc, step, jnp.zeros((1, w_ref.shape[2]), F32))
    o_ref[0] = acc + b_ref[0]


def ada_mod(c, ada_w, ada_b, tn=1024, kc=128):
    depth, d, n = ada_w.shape
    return pl.pallas_call(
        functools.partial(_ada_body, kc=kc),
        grid=(depth, n // tn),
        in_specs=[pl.BlockSpec((d, 1), lambda l, j: (0, 0)),
                  pl.BlockSpec((1, d, tn), lambda l, j: (l, 0, j)),
                  pl.BlockSpec((1, 1, tn), lambda l, j: (l, 0, j))],
        out_specs=pl.BlockSpec((1, 1, tn), lambda l, j: (l, 0, j)),
        out_shape=jax.ShapeDtypeStruct((depth, 1, n), F32),
        compiler_params=_params("arbitrary", "arbitrary"),
        name="ada_mod",
    )(c.reshape(d, 1), ada_w, ada_b.reshape(depth, 1, n))


def _norm_rows(x, gain, shift, scale, eps):
    y = x * lax.rsqrt(jnp.mean(x * x, axis=-1, keepdims=True) + eps)
    return (y * gain) * (1.0 + scale) + shift


def _norm_mod_body(x_ref, g_ref, sh_ref, sc_ref, o_ref):
    o_ref[...] = _norm_rows(x_ref[...], g_ref[...], sh_ref[...], sc_ref[...], NORM_EPS).astype(o_ref.dtype)


def norm_mod(x, gain, shift, scale, out_dtype, tm=256):
    s, d = x.shape
    row = pl.BlockSpec((1, d), lambda i: (0, 0))
    return pl.pallas_call(
        _norm_mod_body,
        grid=(s // tm,),
        in_specs=[pl.BlockSpec((tm, d), lambda i: (i, 0)), row, row, row],
        out_specs=pl.BlockSpec((tm, d), lambda i: (i, 0)),
        out_shape=jax.ShapeDtypeStruct((s, d), out_dtype),
        compiler_params=_params("arbitrary"),
        name="norm_mod",
    )(x, gain.reshape(1, d), shift.reshape(1, d), scale.reshape(1, d))


def _mm_body(*refs, fused):
    if fused:
        x_ref, w_ref, res_ref, gate_ref, o_ref, wb_ref = refs
    else:
        x_ref, w_ref, o_ref, wb_ref = refs

    @pl.when(pl.program_id(1) == 0)
    def _():
        wb_ref[...] = w_ref[...].astype(BF16)

    acc = jnp.dot(x_ref[...].astype(BF16), wb_ref[...], preferred_element_type=F32)
    if fused:
        acc = res_ref[...] + gate_ref[...] * acc
    o_ref[...] = acc.astype(o_ref.dtype)


def matmul(x, w, out_dtype=F32, res=None, gate=None, tm=512, tn=1024):
    m, k = x.shape
    n = w.shape[1]
    tn = min(tn, n)
    tm = min(tm, m)
    fused = res is not None
    in_specs = [pl.BlockSpec((tm, k), lambda j, i: (i, 0)),
                pl.BlockSpec((k, tn), lambda j, i: (0, j))]
    args = [x, w]
    if fused:
        in_specs += [pl.BlockSpec((tm, tn), lambda j, i: (i, j)),
                     pl.BlockSpec((1, tn), lambda j, i: (0, j))]
        args += [res, gate.reshape(1, n)]
    return pl.pallas_call(
        functools.partial(_mm_body, fused=fused),
        grid=(n // tn, m // tm),
        in_specs=in_specs,
        out_specs=pl.BlockSpec((tm, tn), lambda j, i: (i, j)),
        out_shape=jax.ShapeDtypeStruct((m, n), out_dtype),
        scratch_shapes=[pltpu.VMEM((k, tn), BF16)],
        compiler_params=_params("arbitrary", "arbitrary"),
        name="matmul",
    )(*args)


def _mlstm_body(proj_ref, gt_ref, bg_ref, ng_ref, out_ref, c_s, n_s, m_s, *, dk, dv):
    H = MLSTM_HEADS

    @pl.when(pl.program_id(0) == 0)
    def _():
        c_s[...] = jnp.zeros_like(c_s)
        n_s[...] = jnp.zeros_like(n_s)
        m_s[...] = jnp.zeros_like(m_s)

    for h in range(H):
        _mlstm_head(h,
                    proj_ref.at[:, pl.ds(h * dk, dk)],
                    proj_ref.at[:, pl.ds(H * dk + h * dk, dk)],
                    proj_ref.at[:, pl.ds(2 * H * dk + h * dv, dv)],
                    proj_ref.at[:, pl.ds(2 * H * dk + H * dv + h * dv, dv)],
                    gt_ref, bg_ref, ng_ref.at[:, pl.ds(h * dv, dv)], out_ref.at[:, pl.ds(h * dv, dv)],
                    c_s.at[h], n_s.at[h], m_s.at[h])


def _mlstm_head(h, q_ref, k_ref, v_ref, o_ref, gt_ref, bg_ref, ng_ref, out_ref, c_s, n_s, m_s):
    L = q_ref.shape[0]
    dk = q_ref.shape[1]
    dv = v_ref.shape[1]

    lane = lax.broadcasted_iota(jnp.int32, (L, LANES), 1)
    row = lax.broadcasted_iota(jnp.int32, (L, L), 0)
    col = lax.broadcasted_iota(jnp.int32, (L, L), 1)

    capped = GATE_SOFTCAP * jnp.tanh((gt_ref[...] + bg_ref[...]) / GATE_SOFTCAP)
    li = jnp.sum(jnp.where(lane == h, capped, 0.0), axis=-1, keepdims=True)
    fpre = jnp.sum(jnp.where(lane == h + MLSTM_HEADS, capped, 0.0), axis=-1, keepdims=True)
    lf = jax.nn.log_sigmoid(fpre)

    tril = (row >= col).astype(F32)
    g_b = jnp.dot(tril, jnp.broadcast_to(lf, (L, L)), precision=HIGHEST,
                  preferred_element_type=F32)
    g = g_b[:, 0:1]
    g_tot = jnp.sum(lf, axis=0, keepdims=True)
    eye = row == col
    g_row = jnp.sum(jnp.where(eye, g_b, 0.0), axis=0, keepdims=True)
    li_row = jnp.sum(jnp.where(eye, jnp.broadcast_to(li, (L, L)), 0.0), axis=0, keepdims=True)

    m_prev = m_s[...]
    c_prev = c_s[...]
    n_prev = n_s[...]

    q = q_ref[...] * (dk ** -0.5)
    k = k_ref[...]
    vb = v_ref[...].astype(BF16)
    qb = q.astype(BF16)

    d_log = jnp.where(row >= col, g_b - g_row + li_row, -jnp.inf)
    inter_log = g + m_prev
    m_t = jnp.maximum(inter_log, jnp.max(d_log, axis=-1, keepdims=True))
    s = _dot_nt(qb, k) * jnp.exp(d_log - m_t)
    inter = jnp.exp(inter_log - m_t)
    num = _bdot(s, vb) + inter * _bdot(qb, c_prev)
    den = jnp.sum(s, axis=-1, keepdims=True) + inter * jnp.sum(q * n_prev, axis=-1, keepdims=True)
    hh = num / jnp.maximum(jnp.abs(den), jnp.exp(-m_t))
    hh = hh * lax.rsqrt(jnp.mean(hh * hh, axis=-1, keepdims=True) + NORM_EPS) * ng_ref[...]
    out_ref[...] = (jax.nn.sigmoid(o_ref[...]) * hh).astype(out_ref.dtype)

    w_state = g_tot - g + li
    m_loc = jnp.max(w_state, axis=0, keepdims=True)
    ks = jnp.exp(w_state - m_loc) * k
    kv = _dot_tn(ks, vb)
    n_c = jnp.sum(ks, axis=0, keepdims=True)
    m_new = jnp.maximum(g_tot + m_prev, m_loc)
    a = jnp.exp(g_tot + m_prev - m_new)
    b = jnp.exp(m_loc - m_new)
    c_s[...] = a * c_prev + b * kv
    n_s[...] = a * n_prev + b * n_c
    m_s[...] = m_new


def mlstm_cells(proj, gates, b_gates_pad, norm_gain):
    s = proj.shape[0]
    d = norm_gain.shape[0]
    H, L = MLSTM_HEADS, MLSTM_CHUNK
    dk = (proj.shape[1] - 2 * d) // (2 * H)
    dv = d // H
    n = proj.shape[1]
    return pl.pallas_call(
        functools.partial(_mlstm_body, dk=dk, dv=dv),
        grid=(s // L,),
        in_specs=[pl.BlockSpec((L, n), lambda c: (c, 0)),
                  pl.BlockSpec((L, LANES), lambda c: (c, 0)),
                  pl.BlockSpec((1, LANES), lambda c: (0, 0)),
                  pl.BlockSpec((1, d), lambda c: (0, 0))],
        out_specs=pl.BlockSpec((L, d), lambda c: (c, 0)),
        out_shape=jax.ShapeDtypeStruct((s, d), BF16),
        scratch_shapes=[pltpu.VMEM((H, dk, dv), F32), pltpu.VMEM((H, 1, dk), F32), pltpu.VMEM((H, 1, 1), F32)],
        compiler_params=_params("arbitrary"),
        name="mlstm_cells",
    )(proj, gates, b_gates_pad, norm_gain.reshape(1, d))


def mlstm_mixer(x, gain, shift, scale, w_in, b_gates, norm_gain, w_out, gate_row):
    d = x.shape[1]
    n_main = w_in.shape[1] - 2 * MLSTM_HEADS
    hm = norm_mod(x, gain, shift, scale, BF16)
    proj = matmul(hm, w_in[:, :n_main])
    w_g = jnp.pad(w_in[:, n_main:], ((0, 0), (0, LANES - 2 * MLSTM_HEADS)))
    gates = matmul(hm, w_g)
    b_pad = jnp.pad(b_gates, (0, LANES - 2 * MLSTM_HEADS)).reshape(1, LANES)
    cells = mlstm_cells(proj, gates, b_pad, norm_gain)
    return matmul(cells, w_out, res=x, gate=gate_row)


def _rwkv_prep_body(x_ref, xp_ref, g_ref, sh_ref, sc_ref, mu_ref, *outs):
    tm = x_ref.shape[0]
    hm = _norm_rows(x_ref[...], g_ref[...], sh_ref[...], sc_ref[...], NORM_EPS)
    prev = _norm_rows(xp_ref[7:8, :], g_ref[...], sh_ref[...], sc_ref[...], NORM_EPS)
    prev = jnp.where(pl.program_id(0) == 0, 0.0, prev)
    rows = lax.broadcasted_iota(jnp.int32, hm.shape, 0)
    shifted = jnp.where(rows == 0, prev, pltpu.roll(hm, 1, axis=0))
    xx = shifted - hm
    for j, o_ref in enumerate(outs):
        o_ref[...] = (hm + xx * mu_ref[j:j + 1, :]).astype(o_ref.dtype)


def rwkv_prep(x, gain, shift, scale, mu, tm=256):
    s, d = x.shape
    row = pl.BlockSpec((1, d), lambda i: (0, 0))
    blk = pl.BlockSpec((tm, d), lambda i: (i, 0))
    return pl.pallas_call(
        _rwkv_prep_body,
        grid=(s // tm,),
        in_specs=[blk, pl.BlockSpec((8, d), lambda i: (jnp.maximum(i * (tm // 8) - 1, 0), 0)),
                  row, row, row, pl.BlockSpec((8, d), lambda i: (0, 0))],
        out_specs=[blk] * 6,
        out_shape=[jax.ShapeDtypeStruct((s, d), BF16)] * 6,
        compiler_params=_params("arbitrary"),
        name="rwkv_prep",
    )(x, x, gain.reshape(1, d), shift.reshape(1, d), scale.reshape(1, d), jnp.pad(mu, ((0, 2), (0, 0))))


def _lora_body(x_ref, w1_ref, w2_ref, o_ref, *, act):
    t = _bdot(x_ref[...], w1_ref[...])
    if act == "tanh":
        t = jnp.tanh(t)
    elif act == "sigmoid":
        t = jax.nn.sigmoid(t)
    o_ref[...] = _bdot(t, w2_ref[...])


def lora(x, w1, w2, act, tm=512):
    s, d = x.shape
    r = w1.shape[1]
    rp = -(-r // LANES) * LANES
    w1p = jnp.pad(w1, ((0, 0), (0, rp - r)))
    w2p = jnp.pad(w2, ((0, rp - r), (0, 0)))
    n = w2.shape[1]
    return pl.pallas_call(
        functools.partial(_lora_body, act=act),
        grid=(s // tm,),
        in_specs=[pl.BlockSpec((tm, d), lambda i: (i, 0)),
                  pl.BlockSpec((d, rp), lambda i: (0, 0)),
                  pl.BlockSpec((rp, n), lambda i: (0, 0))],
        out_specs=pl.BlockSpec((tm, n), lambda i: (i, 0)),
        out_shape=jax.ShapeDtypeStruct((s, n), F32),
        compiler_params=_params("arbitrary"),
        name="lora",
    )(x, w1p, w2p)


def _split3(x):
    hi = x.astype(BF16)
    r1 = x - hi.astype(F32)
    mid = r1.astype(BF16)
    lo = (r1 - mid.astype(F32)).astype(BF16)
    return hi, mid, lo


def _segsum(x, bmat, passes):
    parts = _split3(x)[:passes] if passes > 1 else (x.astype(BF16),)
    acc = jnp.dot(parts[0], bmat, preferred_element_type=F32)
    for p in parts[1:]:
        acc = acc + jnp.dot(p, bmat, preferred_element_type=F32)
    return acc


SUBLANES = 8


SCAN_W = 256


def _head_ones(width):
    ri = lax.broadcasted_iota(jnp.int32, (width, width), 0)
    ci = lax.broadcasted_iota(jnp.int32, (width, width), 1)
    return ((ri // RWKV_HEAD) == (ci // RWKV_HEAD)).astype(BF16)


def _rwkv_pre_body(k_ref, wl_ref, al_ref, w0_ref, a0_ref, kk_ref, ka_ref, nkk_ref, w_ref, kka_ref, kp_ref):
    d = k_ref.shape[1]
    bmat = _head_ones(SCAN_W)
    for j in range(d // SCAN_W):
        cs = slice(j * SCAN_W, (j + 1) * SCAN_W)
        kraw = k_ref[:, cs]
        w_log = -jax.nn.softplus(-(w0_ref[:, cs] + wl_ref[:, cs])) - 0.5
        w_ref[:, cs] = jnp.exp(-jnp.exp(w_log))
        a = jax.nn.sigmoid(a0_ref[:, cs] + al_ref[:, cs])
        kk = kraw * kk_ref[:, cs]
        nrm = jnp.sqrt(_segsum(kk * kk, bmat, 3))
        kk = kk / jnp.maximum(nrm, 1e-12)
        nkk_ref[:, cs] = -kk
        kka_ref[:, cs] = kk * a
        kp_ref[:, cs] = kraw * (1.0 + (a - 1.0) * ka_ref[:, cs])


def rwkv_pre(k, wl, al, w0, a0, k_k, k_a, tm=256):
    s, d = k.shape
    blk = pl.BlockSpec((tm, d), lambda i: (i, 0))
    row = pl.BlockSpec((1, d), lambda i: (0, 0))
    return pl.pallas_call(
        _rwkv_pre_body,
        grid=(s // tm,),
        in_specs=[blk] * 3 + [row] * 4,
        out_specs=[blk] * 4,
        out_shape=[jax.ShapeDtypeStruct((s, d), F32)] * 4,
        compiler_params=_params("arbitrary"),
        name="rwkv_pre",
    )(k, wl, al, *[t.reshape(1, d) for t in (w0, a0, k_k, k_a)])


def _rwkv_post_body(y_ref, r_ref, kp_ref, v_ref, gt_ref, rk_ref, lg_ref, lb_ref, o_ref):
    d = y_ref.shape[1]
    bmat = _head_ones(SCAN_W)
    inv = 1.0 / RWKV_HEAD
    for j in range(d // SCAN_W):
        cs = slice(j * SCAN_W, (j + 1) * SCAN_W)
        y = y_ref[:, cs]
        mean = _segsum(y, bmat, 3) * inv
        yc = y - mean
        var = _segsum(yc * yc, bmat, 3) * inv
        yn = yc * lax.rsqrt(var + RWKV_GN_EPS) * lg_ref[:, cs] + lb_ref[:, cs]
        bonus = _segsum(r_ref[:, cs] * kp_ref[:, cs] * rk_ref[:, cs], bmat, 3) * v_ref[:, cs]
        o_ref[:, cs] = ((yn + bonus) * gt_ref[:, cs]).astype(o_ref.dtype)


def rwkv_post(y, r, kp, v, gt, r_k, ln_gain, ln_bias, tm=256):
    s, d = y.shape
    blk = pl.BlockSpec((tm, d), lambda i: (i, 0))
    row = pl.BlockSpec((1, d), lambda i: (0, 0))
    return pl.pallas_call(
        _rwkv_post_body,
        grid=(s // tm,),
        in_specs=[blk] * 5 + [row] * 3,
        out_specs=blk,
        out_shape=jax.ShapeDtypeStruct((s, d), BF16),
        compiler_params=_params("arbitrary"),
        name="rwkv_post",
    )(y, r, kp, v, gt, *[t.reshape(1, d) for t in (r_k, ln_gain, ln_bias)])


def _rwkv_scan_body(nkk_ref, w_ref, kka_ref, kp_ref, r_ref, v_ref, y_ref, st_s, b_s, m1_s,
                    nkk3, w3, kka3, kp3, r3, v3, y3, u_a, u_b):
    tb, d = r_ref.shape
    nv = RWKV_HEAD
    ntile = d // SCAN_W
    ngrp = tb // SUBLANES
    nslab = nv // SUBLANES
    slab = ntile * SUBLANES
    rows = nslab * slab

    @pl.when(pl.program_id(0) == 0)
    def _():
        st_s[...] = jnp.zeros_like(st_s)
        b_s[...] = _head_ones(SCAN_W)
        vi = lax.broadcasted_iota(jnp.int32, (rows, SCAN_W), 0)
        li = lax.broadcasted_iota(jnp.int32, (rows, SCAN_W), 1)
        v_idx = (vi // slab) * SUBLANES + vi % SUBLANES
        m1_s[...] = (v_idx == (li % nv)).astype(F32)

    def replicate(g, carry):
        base = pl.multiple_of(g * SUBLANES, SUBLANES)
        for src, dst in ((nkk_ref, nkk3), (w_ref, w3), (kka_ref, kka3), (kp_ref, kp3), (r_ref, r3), (v_ref, v3)):
            for j in range(ntile):
                grp = src[pl.ds(base, SUBLANES), j * SCAN_W:(j + 1) * SCAN_W]
                for i in range(SUBLANES):
                    dst[base + i, j * SUBLANES:(j + 1) * SUBLANES, :] = jnp.broadcast_to(
                        grp[i:i + 1, :], (SUBLANES, SCAN_W))
        return carry

    lax.fori_loop(0, ngrp, replicate, 0)

    def tall(ref, t):
        return jnp.concatenate([ref[t]] * nslab, axis=0)

    def outer(t, u_ref):
        vcol = jnp.dot((m1_s[...] * tall(v3, t)).astype(BF16), b_s[...], preferred_element_type=F32)
        u_ref[...] = vcol * tall(kp3, t)

    def step(t, u_ref):
        st = st_s[...]
        sa = jnp.dot((st * tall(nkk3, t)).astype(BF16), b_s[...], preferred_element_type=F32)
        sn = st * tall(w3, t) + sa * tall(kka3, t) + u_ref[...]
        st_s[...] = sn
        part = m1_s[...] * jnp.dot((sn * tall(r3, t)).astype(BF16), b_s[...], preferred_element_type=F32)
        acc = part[0:slab]
        for i in range(1, nslab):
            acc = acc + part[i * slab:(i + 1) * slab]
        y3[t] = acc

    outer(0, u_a)

    def pair(i, carry):
        t = 2 * i
        outer(t + 1, u_b)
        step(t, u_a)
        outer(jnp.minimum(t + 2, tb - 1), u_a)
        step(t + 1, u_b)
        return carry

    lax.fori_loop(0, tb // 2, pair, 0, unroll=2)

    def collect(g, carry):
        base = pl.multiple_of(g * SUBLANES, SUBLANES)
        for j in range(ntile):
            got = [jnp.sum(y3[base + i, j * SUBLANES:(j + 1) * SUBLANES, :], axis=0, keepdims=True)
                   for i in range(SUBLANES)]
            y_ref[pl.ds(base, SUBLANES), j * SCAN_W:(j + 1) * SCAN_W] = jnp.concatenate(got, axis=0)
        return carry

    lax.fori_loop(0, ngrp, collect, 0)


def rwkv_scan(nkk, w, kka, kp, r, v, tb=32):
    s, d = r.shape
    blk = pl.BlockSpec((tb, d), lambda i: (i, 0))
    slab = (d // SCAN_W) * SUBLANES
    state_rows = (RWKV_HEAD // SUBLANES) * slab
    return pl.pallas_call(
        _rwkv_scan_body,
        grid=(s // tb,),
        in_specs=[blk] * 6,
        out_specs=blk,
        out_shape=jax.ShapeDtypeStruct((s, d), F32),
        scratch_shapes=[pltpu.VMEM((state_rows, SCAN_W), F32), pltpu.VMEM((SCAN_W, SCAN_W), BF16),
                        pltpu.VMEM((state_rows, SCAN_W), F32)]
                       + [pltpu.VMEM((tb, slab, SCAN_W), F32)] * 7
                       + [pltpu.VMEM((state_rows, SCAN_W), F32)] * 2,
        compiler_params=_params("arbitrary"),
        name="rwkv_scan",
    )(nkk, w, kka, kp, r, v)


def rwkv_mixer(x, gain, shift, scale, mu, w_rkv, w0, w1, w2, a0, a1, a2, g1, g2, k_k, k_a, r_k,
               ln_gain, ln_bias, w_o, gate_row):
    xr, xk, xv, xw, xa, xg = rwkv_prep(x, gain, shift, scale, mu)
    r = matmul(xr, w_rkv[0])
    k = matmul(xk, w_rkv[1])
    v = matmul(xv, w_rkv[2])
    wl = lora(xw, w1, w2, "tanh")
    al = lora(xa, a1, a2, "none")
    gt = lora(xg, g1, g2, "sigmoid")
    nkk, w, kka, kp = rwkv_pre(k, wl, al, w0, a0, k_k, k_a)
    y = rwkv_scan(nkk, w, kka, kp, r, v)
    y = rwkv_post(y, r, kp, v, gt, r_k.reshape(-1), ln_gain, ln_bias)
    return matmul(y, w_o, res=x, gate=gate_row)


def _rope_body(p_ref, pos_ref, o_ref, *, n_rot_tiles, q_tiles):
    tm, n = p_ref.shape
    half = ROPE_DIM // 2
    lane = lax.broadcasted_iota(jnp.int32, (tm, LANES), 1)
    inv_freq = jnp.exp((lane % half).astype(F32) * (-2.0 * math.log(ROPE_THETA) / ROPE_DIM))
    ang = pos_ref[...].astype(F32) * inv_freq
    roped = lane < ROPE_DIM
    cos = jnp.where(roped, jnp.cos(ang), 1.0)
    sin = jnp.where(roped, jnp.where(lane < half, -jnp.sin(ang), jnp.sin(ang)), 0.0)
    for j in range(n // LANES):
        x = p_ref[:, j * LANES:(j + 1) * LANES]
        if j < n_rot_tiles:
            partner = jnp.where(lane < half, pltpu.roll(x, LANES - half, axis=1), pltpu.roll(x, half, axis=1))
            x = x * cos + partner * sin
            if j < q_tiles:
                x = x * (DIFF_DQK ** -0.5)
        o_ref[:, j * LANES:(j + 1) * LANES] = x.astype(o_ref.dtype)


def rope_cast(proj, positions, tm=256):
    s, n = proj.shape
    q_tiles = 2 * DIFF_HEADS
    return pl.pallas_call(
        functools.partial(_rope_body, n_rot_tiles=2 * q_tiles, q_tiles=q_tiles),
        grid=(s // tm,),
        in_specs=[pl.BlockSpec((tm, n), lambda i: (i, 0)),
                  pl.BlockSpec((tm, 1), lambda i: (i, 0))],
        out_specs=pl.BlockSpec((tm, n), lambda i: (i, 0)),
        out_shape=jax.ShapeDtypeStruct((s, n), BF16),
        compiler_params=_params("arbitrary"),
        name="rope_cast",
    )(proj, positions.reshape(s, 1))


def _diff_attn_body(qi_ref, kj_ref, q_ref, k_ref, v_ref, lam_ref, sg_ref, o_ref, m_s, l_s, acc_s, *, lambda_init):
    step = pl.program_id(1)
    qi, kj = qi_ref[step], kj_ref[step]
    tb = q_ref.shape[0]

    @pl.when(kj == 0)
    def _():
        m_s[...] = jnp.full_like(m_s, -jnp.inf)
        l_s[...] = jnp.zeros_like(l_s)
        acc_s[...] = jnp.zeros_like(acc_s)

    def accumulate(diagonal):
        v = v_ref[...]
        for c in range(2):
            cs = slice(c * DIFF_DQK, (c + 1) * DIFF_DQK)
            s = _dot_nt(q_ref[:, cs], k_ref[:, cs])
            if diagonal:
                rows = lax.broadcasted_iota(jnp.int32, (tb, tb), 0)
                cols = lax.broadcasted_iota(jnp.int32, (tb, tb), 1)
                s = jnp.where(cols <= rows, s, -jnp.inf)
            m_old = m_s[c]
            m_new = jnp.maximum(m_old, jnp.max(s, axis=-1, keepdims=True))
            alpha = jnp.exp(m_old - m_new)
            p = jnp.exp(s - m_new)
            l_s[c] = alpha * l_s[c] + jnp.sum(p, axis=-1, keepdims=True)
            acc_s[c] = alpha * acc_s[c] + _bdot(p, v)
            m_s[c] = m_new

    @pl.when(kj < qi)
    def _():
        accumulate(False)

    @pl.when(kj == qi)
    def _():
        accumulate(True)
        lp = lam_ref[...]
        lam = (jnp.exp(jnp.sum(lp[0:1] * lp[1:2], axis=-1, keepdims=True))
               - jnp.exp(jnp.sum(lp[2:3] * lp[3:4], axis=-1, keepdims=True)) + lambda_init)
        out = acc_s[0] / l_s[0] - lam * (acc_s[1] / l_s[1])
        out = out * lax.rsqrt(jnp.mean(out * out, axis=-1, keepdims=True) + DIFF_SUBLN_EPS)
        o_ref[...] = (out * sg_ref[...] * (1.0 - lambda_init)).astype(o_ref.dtype)


def diff_attention(qkv, lam_params, subln_gain, lambda_init, tb=1024):
    s = qkv.shape[0]
    H, dv = DIFF_HEADS, DIFF_DV
    tb = min(tb, s)
    nb = s // tb
    pairs = [(qi, kj) for qi in range(nb) for kj in range(qi + 1)]
    qi_tab = jnp.asarray([p[0] for p in pairs], jnp.int32)
    kj_tab = jnp.asarray([p[1] for p in pairs], jnp.int32)
    grid_spec = pltpu.PrefetchScalarGridSpec(
        num_scalar_prefetch=2,
        grid=(H, len(pairs)),
        in_specs=[pl.BlockSpec((tb, dv), lambda h, t, qi, kj: (qi[t], h)),
                  pl.BlockSpec((tb, dv), lambda h, t, qi, kj: (kj[t], H + h)),
                  pl.BlockSpec((tb, dv), lambda h, t, qi, kj: (kj[t], 2 * H + h)),
                  pl.BlockSpec((4, DIFF_DQK), lambda h, t, qi, kj: (0, 0)),
                  pl.BlockSpec((1, dv), lambda h, t, qi, kj: (0, 0))],
        out_specs=pl.BlockSpec((tb, dv), lambda h, t, qi, kj: (qi[t], h)),
        scratch_shapes=[pltpu.VMEM((2, tb, 1), F32), pltpu.VMEM((2, tb, 1), F32),
                        pltpu.VMEM((2, tb, dv), F32)],
    )
    return pl.pallas_call(
        functools.partial(_diff_attn_body, lambda_init=lambda_init),
        grid_spec=grid_spec,
        out_shape=jax.ShapeDtypeStruct((s, H * dv), BF16),
        compiler_params=_params("arbitrary", "arbitrary"),
        name="diff_attention",
    )(qi_tab, kj_tab, qkv, qkv, qkv, lam_params, subln_gain.reshape(1, dv))


def diff_mixer(x, gain, shift, scale, positions, w_qkv, lam_params, subln_gain, w_o, layer, gate_row):
    hm = norm_mod(x, gain, shift, scale, BF16)
    proj = matmul(hm, w_qkv)
    qkv = rope_cast(proj, positions)
    lambda_init = 0.8 - 0.6 * math.exp(-0.3 * layer)
    att = diff_attention(qkv, lam_params, subln_gain, lambda_init)
    return matmul(att, w_o, res=x, gate=gate_row)


def _router_body(x_ref, g_ref, sh_ref, sc_ref, rw_ref, rb_ref, hf_ref, lg_ref):
    hf = _norm_rows(x_ref[...], g_ref[...], sh_ref[...], sc_ref[...], NORM_EPS)
    hf_ref[...] = hf
    lg_ref[...] = jnp.dot(hf, rw_ref[...], precision=HIGHEST, preferred_element_type=F32) + rb_ref[...]


def router(x, gain, shift, scale, router_w, router_b, tm=256):
    s, d = x.shape
    e = router_w.shape[1]
    rw = jnp.pad(router_w, ((0, 0), (0, LANES - e)))
    rb = jnp.pad(router_b, (0, LANES - e), constant_values=-1e30).reshape(1, LANES)
    row = pl.BlockSpec((1, d), lambda i: (0, 0))
    return pl.pallas_call(
        _router_body,
        grid=(s // tm,),
        in_specs=[pl.BlockSpec((tm, d), lambda i: (i, 0)), row, row, row,
                  pl.BlockSpec((d, LANES), lambda i: (0, 0)), pl.BlockSpec((1, LANES), lambda i: (0, 0))],
        out_specs=[pl.BlockSpec((tm, d), lambda i: (i, 0)), pl.BlockSpec((tm, LANES), lambda i: (i, 0))],
        out_shape=[jax.ShapeDtypeStruct((s, d), F32), jax.ShapeDtypeStruct((s, LANES), F32)],
        compiler_params=_params("arbitrary"),
        name="router",
    )(x, gain.reshape(1, d), shift.reshape(1, d), scale.reshape(1, d), rw, rb)


def _topk_body(lg_ref, eidx_ref, gw_ref, rk_ref, cnt_ref):
    tm = lg_ref.shape[0]

    @pl.when(pl.program_id(0) == 0)
    def _():
        cnt_ref[...] = jnp.zeros_like(cnt_ref)

    lane = lax.broadcasted_iota(jnp.int32, (tm, LANES), 1).astype(F32)
    work = lg_ref[...]
    vals, idxs, hots = [], [], []
    for _ in range(TOP_K):
        m = jnp.max(work, axis=-1, keepdims=True)
        idx = jnp.min(jnp.where(work == m, lane, float(LANES)), axis=-1, keepdims=True)
        hot = lane == idx
        work = jnp.where(hot, -jnp.inf, work)
        vals.append(m)
        idxs.append(idx)
        hots.append(hot)
    exps = [jnp.exp(v - vals[0]) for v in vals]
    den = exps[0] + exps[1] + exps[2] + exps[3]
    sel = jnp.zeros((tm, LANES), F32)
    for hot in hots:
        sel = sel + hot.astype(F32)
    ri = lax.broadcasted_iota(jnp.int32, (tm, tm), 0)
    ci = lax.broadcasted_iota(jnp.int32, (tm, tm), 1)
    before = jnp.dot((ri > ci).astype(BF16), sel.astype(BF16), preferred_element_type=F32)
    rank = before + cnt_ref[...]
    cnt_ref[...] = cnt_ref[...] + jnp.sum(sel, axis=0, keepdims=True)
    eidx = jnp.zeros((tm, LANES), F32)
    gw = jnp.zeros((tm, LANES), F32)
    rk = jnp.zeros((tm, LANES), F32)
    for kslot in range(TOP_K):
        here = lane == kslot
        eidx = jnp.where(here, idxs[kslot], eidx)
        gw = jnp.where(here, exps[kslot] / den, gw)
        rk = jnp.where(here, jnp.sum(jnp.where(hots[kslot], rank, 0.0), axis=-1, keepdims=True), rk)
    eidx_ref[...] = eidx
    gw_ref[...] = gw
    rk_ref[...] = rk


def topk_route(logits, tm=TOK_BLOCK):
    s = logits.shape[0]
    blk = pl.BlockSpec((tm, LANES), lambda i: (i, 0))
    return pl.pallas_call(
        _topk_body,
        grid=(s // tm,),
        in_specs=[blk],
        out_specs=[blk, blk, blk, pl.BlockSpec((1, LANES), lambda i: (0, 0))],
        out_shape=[jax.ShapeDtypeStruct((s, LANES), F32), jax.ShapeDtypeStruct((s, LANES), F32),
                   jax.ShapeDtypeStruct((s, LANES), F32), jax.ShapeDtypeStruct((1, LANES), F32)],
        compiler_params=_params("arbitrary"),
        name="topk_route",
    )(logits)


def _slot_body(eidx_ref, rk_ref, ps_ref, pos_ref):
    tm = eidx_ref.shape[0]
    lane = lax.broadcasted_iota(jnp.int32, (tm, LANES), 1).astype(F32)
    eidx = eidx_ref[...]
    rk = rk_ref[...]
    pos = jnp.zeros((tm, LANES), F32)
    for kslot in range(TOP_K):
        e_k = jnp.sum(jnp.where(lane == kslot, eidx, 0.0), axis=-1, keepdims=True)
        start = jnp.sum(jnp.where(lane == e_k, ps_ref[...], 0.0), axis=-1, keepdims=True)
        r_k = jnp.sum(jnp.where(lane == kslot, rk, 0.0), axis=-1, keepdims=True)
        pos = jnp.where(lane == kslot, start + r_k, pos)
    pos_ref[...] = pos.astype(jnp.int32)


def slot_positions(eidx, rk, padded_start, tm=TOK_BLOCK):
    s = eidx.shape[0]
    blk = pl.BlockSpec((tm, LANES), lambda i: (i, 0))
    return pl.pallas_call(
        _slot_body,
        grid=(s // tm,),
        in_specs=[blk, blk, pl.BlockSpec((1, LANES), lambda i: (0, 0))],
        out_specs=blk,
        out_shape=jax.ShapeDtypeStruct((s, LANES), jnp.int32),
        compiler_params=_params("arbitrary"),
        name="slot_positions",
    )(eidx, rk, padded_start)


def _dispatch_body(pos_ref, hf_ref, xs_in_ref, xs_ref, sem):
    del xs_in_ref
    tm = hf_ref.shape[0]

    def issue(t, carry):
        for kslot in range(TOP_K):
            p = pos_ref[t * TOP_K + kslot]
            pltpu.make_async_copy(hf_ref.at[pl.ds(t, 1)], xs_ref.at[pl.ds(p, 1)], sem.at[0]).start(
                priority=kslot % 2)
        return carry

    lax.fori_loop(0, tm, issue, 0)

    for _ in range(TOP_K):
        pltpu.make_async_copy(hf_ref, xs_ref.at[pl.ds(0, tm)], sem.at[0]).wait()


def dispatch(pos_flat, hf, n_rows, tm=TOK_BLOCK):
    s, d = hf.shape
    return pl.pallas_call(
        _dispatch_body,
        grid=(s // tm,),
        in_specs=[pl.BlockSpec((tm * TOP_K,), lambda i: (i,), memory_space=pltpu.SMEM),
                  pl.BlockSpec((tm, d), lambda i: (i, 0)),
                  pl.BlockSpec(memory_space=pl.ANY)],
        out_specs=pl.BlockSpec(memory_space=pl.ANY),
        out_shape=jax.ShapeDtypeStruct((n_rows, d), hf.dtype),
        scratch_shapes=[pltpu.SemaphoreType.DMA((1,))],
        input_output_aliases={2: 0},
        compiler_params=_params("arbitrary"),
        name="moe_dispatch",
    )(pos_flat, hf, jnp.zeros((n_rows, d), hf.dtype))


def _expert_body(be_ref, first_ref, nxt_ref, tot_ref, xs_ref, wgu_hbm, bgu_ref, wdn_hbm, bdn_ref, perm_ref, ys_ref,
                 stg_gu, stg_dn, wgu_s, wdn_s, sem, *, layer, chunk):
    i = pl.program_id(0)
    rows = xs_ref.shape[0]
    d, n_gu = wgu_s.shape
    dh = wdn_s.shape[0]
    grp = perm_ref.shape[0]
    half = grp // 2

    def fetch(e):
        return (pltpu.make_async_copy(wgu_hbm.at[layer, e], stg_gu, sem.at[0]),
                pltpu.make_async_copy(wdn_hbm.at[layer, e], stg_dn, sem.at[1]))

    @pl.when(first_ref[i] == 1)
    def _():
        @pl.when(i == 0)
        def _():
            for cp in fetch(be_ref[0]):
                cp.start()

        for cp in fetch(be_ref[i]):
            cp.wait()
        for c in range(d // chunk):
            wgu_s[c * chunk:(c + 1) * chunk, :] = stg_gu[c * chunk:(c + 1) * chunk, :].astype(BF16)
        for c in range(dh // chunk):
            wdn_s[c * chunk:(c + 1) * chunk, :] = stg_dn[c * chunk:(c + 1) * chunk, :].astype(BF16)

        @pl.when(nxt_ref[i] >= 0)
        def _():
            for cp in fetch(nxt_ref[i]):
                cp.start()

    @pl.when(i * rows < tot_ref[0])
    def _():
        gu = _bdot(xs_ref[...], wgu_s[...]) + bgu_ref[0]
        gub = gu.astype(BF16)
        gs, us = [], []
        for j in range(n_gu // grp):
            pj = jnp.dot(gub[:, j * grp:(j + 1) * grp], perm_ref[...], preferred_element_type=F32)
            gs.append(pj[:, :half])
            us.append(pj[:, half:])
        g = jnp.minimum(jnp.concatenate(gs, axis=1), SWIGLU_LIMIT)
        u = jnp.clip(jnp.concatenate(us, axis=1), -SWIGLU_LIMIT, SWIGLU_LIMIT)
        act = (u + 1.0) * (g * jax.nn.sigmoid(g * SWIGLU_ALPHA))
        ys_ref[...] = _bdot(act, wdn_s[...]) + bdn_ref[0]

    @pl.when(i * rows >= tot_ref[0])
    def _():
        ys_ref[...] = jnp.zeros_like(ys_ref)


def expert_ffn(blk_expert, blk_first, blk_next, total_rows, xs, w_gu, b_gu, w_dn, b_dn, layer, rows=MOE_ROWS):
    n_rows, d = xs.shape
    _, e, _, n_gu = w_gu.shape
    dh = w_dn.shape[2]
    grp = 2 * LANES
    perm = np.zeros((grp, grp), np.float32)
    perm[2 * np.arange(LANES), np.arange(LANES)] = 1.0
    perm[2 * np.arange(LANES) + 1, LANES + np.arange(LANES)] = 1.0
    grid_spec = pltpu.PrefetchScalarGridSpec(
        num_scalar_prefetch=4,
        grid=(n_rows // rows,),
        in_specs=[pl.BlockSpec((rows, d), lambda i, be, fi, nx, tot: (i, 0)),
                  pl.BlockSpec(memory_space=pl.ANY),
                  pl.BlockSpec((1, 1, n_gu), lambda i, be, fi, nx, tot: (be[i], 0, 0)),
                  pl.BlockSpec(memory_space=pl.ANY),
                  pl.BlockSpec((1, 1, d), lambda i, be, fi, nx, tot: (be[i], 0, 0)),
                  pl.BlockSpec((grp, grp), lambda i, be, fi, nx, tot: (0, 0))],
        out_specs=pl.BlockSpec((rows, d), lambda i, be, fi, nx, tot: (i, 0)),
        scratch_shapes=[pltpu.VMEM((d, n_gu), F32), pltpu.VMEM((dh, d), F32),
                        pltpu.VMEM((d, n_gu), BF16), pltpu.VMEM((dh, d), BF16),
                        pltpu.SemaphoreType.DMA((2,))],
    )
    return pl.pallas_call(
        functools.partial(_expert_body, layer=layer, chunk=256),
        grid_spec=grid_spec,
        out_shape=jax.ShapeDtypeStruct((n_rows, d), F32),
        compiler_params=_params("arbitrary"),
        name="moe_experts",
    )(blk_expert, blk_first, blk_next, total_rows, xs, w_gu, b_gu.reshape(e, 1, n_gu), w_dn,
      b_dn.reshape(e, 1, d), jnp.asarray(perm, BF16))


def _combine_body(pos_ref, x_ref, gw_ref, g2_ref, ys_ref, o_ref, buf, sem):
    tm = x_ref.shape[0]

    def issue(t, carry):
        for kslot in range(TOP_K):
            p = pos_ref[t * TOP_K + kslot]
            pltpu.make_async_copy(ys_ref.at[pl.ds(p, 1)], buf.at[kslot, pl.ds(t, 1)], sem.at[0]).start(
                priority=kslot % 2)
        return carry

    lax.fori_loop(0, tm, issue, 0)

    for kslot in range(TOP_K):
        pltpu.make_async_copy(ys_ref.at[pl.ds(0, tm)], buf.at[kslot], sem.at[0]).wait()

    gw = gw_ref[...]
    acc = gw[:, 0:1] * buf[0]
    for kslot in range(1, TOP_K):
        acc = acc + gw[:, kslot:kslot + 1] * buf[kslot]
    o_ref[...] = x_ref[...] + g2_ref[...] * acc


def combine(pos_flat, x, gw, g2_row, ys, tm=TOK_BLOCK):
    s, d = x.shape
    return pl.pallas_call(
        _combine_body,
        grid=(s // tm,),
        in_specs=[pl.BlockSpec((tm * TOP_K,), lambda i: (i,), memory_space=pltpu.SMEM),
                  pl.BlockSpec((tm, d), lambda i: (i, 0)),
                  pl.BlockSpec((tm, LANES), lambda i: (i, 0)),
                  pl.BlockSpec((1, d), lambda i: (0, 0)),
                  pl.BlockSpec(memory_space=pl.ANY)],
        out_specs=pl.BlockSpec((tm, d), lambda i: (i, 0)),
        out_shape=jax.ShapeDtypeStruct((s, d), F32),
        scratch_shapes=[pltpu.VMEM((TOP_K, tm, d), F32), pltpu.SemaphoreType.DMA((1,))],
        compiler_params=_params("arbitrary"),
        name="moe_combine",
    )(pos_flat, x, gw, g2_row.reshape(1, d), ys)


def moe_layer(x, gain, shift, scale, g2_row, router_w, router_b, w_gu, b_gu, w_dn, b_dn, layer):
    s, d = x.shape
    e = router_w.shape[1]
    hf, logits = router(x, gain, shift, scale, router_w, router_b)
    eidx, gw, rk, counts = topk_route(logits)
    cnt = counts[0, :e].astype(jnp.int32)
    padded = (cnt + MOE_ROWS - 1) // MOE_ROWS * MOE_ROWS
    padded_end = jnp.cumsum(padded)
    padded_start = jnp.pad((padded_end - padded).astype(F32), (0, LANES - e)).reshape(1, LANES)
    n_rows = s * TOP_K + e * MOE_ROWS
    blk_first_row = jnp.arange(n_rows // MOE_ROWS, dtype=jnp.int32) * MOE_ROWS
    blk_expert = jnp.minimum(
        jnp.sum((padded_end[None, :] <= blk_first_row[:, None]).astype(jnp.int32), axis=1), e - 1)
    total_rows = padded_end[-1:].astype(jnp.int32)
    n_blk = n_rows // MOE_ROWS
    blk_id = jnp.arange(n_blk, dtype=jnp.int32)
    prev_expert = jnp.concatenate([jnp.full((1,), -1, jnp.int32), blk_expert[:-1]])
    blk_first = ((blk_first_row < total_rows[0]) & (blk_expert != prev_expert)).astype(jnp.int32)
    first_at = jnp.where(blk_first == 1, blk_id, n_blk)
    next_first = jnp.concatenate([lax.cummin(first_at, reverse=True)[1:], jnp.full((1,), n_blk, jnp.int32)])
    blk_next = jnp.where(next_first < n_blk, blk_expert[jnp.minimum(next_first, n_blk - 1)], -1).astype(jnp.int32)
    pos = slot_positions(eidx, rk, padded_start)
    pos_flat = pos[:, :TOP_K].reshape(-1)
    xs = dispatch(pos_flat, hf, n_rows)
    ys = expert_ffn(blk_expert, blk_first, blk_next, total_rows, xs, w_gu, b_gu, w_dn, b_dn, layer)
    return combine(pos_flat, x, gw, g2_row, ys)


def kernel(x, c, positions, norm_gain, ada_w, ada_b, mlstm_w_in, mlstm_b_gates, mlstm_norm_gain, mlstm_w_out, rwkv_mu, rwkv_w_rkv, rwkv_w0, rwkv_w1, rwkv_w2, rwkv_a0, rwkv_a1, rwkv_a2, rwkv_g1, rwkv_g2, rwkv_k_k, rwkv_k_a, rwkv_r_k, rwkv_ln_gain, rwkv_ln_bias, rwkv_w_o, diff_w_qkv, diff_lambda, diff_subln_gain, diff_w_o, moe_router_w, moe_router_b, moe_w_gate_up, moe_b_gate_up, moe_w_down, moe_b_down, final_gain):
    b, s, d = x.shape
    assert b == 1
    depth = norm_gain.shape[0]
    xt = x.reshape(s, d)
    pos = positions.reshape(s)
    mod = ada_mod(c, ada_w, ada_b).reshape(depth, 6, d)
    seen = [0, 0, 0]
    for layer in range(depth):
        kind = layer % 3
        j = seen[kind]
        seen[kind] += 1
        sh1, sc1, g1, sh2, sc2, g2 = (mod[layer, i] for i in range(6))
        gain = norm_gain[layer, 0]
        if kind == 0:
            xt = mlstm_mixer(xt, gain, sh1, sc1, mlstm_w_in[j], mlstm_b_gates[j], mlstm_norm_gain[j],
                             mlstm_w_out[j], g1)
        elif kind == 1:
            xt = rwkv_mixer(xt, gain, sh1, sc1, rwkv_mu[j], rwkv_w_rkv[j], rwkv_w0[j], rwkv_w1[j], rwkv_w2[j],
                            rwkv_a0[j], rwkv_a1[j], rwkv_a2[j], rwkv_g1[j], rwkv_g2[j], rwkv_k_k[j],
                            rwkv_k_a[j], rwkv_r_k[j], rwkv_ln_gain[j], rwkv_ln_bias[j], rwkv_w_o[j], g1)
        else:
            xt = diff_mixer(xt, gain, sh1, sc1, pos, diff_w_qkv[j], diff_lambda[j], diff_subln_gain[j],
                            diff_w_o[j], layer, g1)
        xt = moe_layer(xt, norm_gain[layer, 1], sh2, sc2, g2, moe_router_w[layer], moe_router_b[layer],
                       moe_w_gate_up, moe_b_gate_up[layer], moe_w_down, moe_b_down[layer], layer)
    zeros = jnp.zeros((d,), F32)
    return norm_mod(xt, final_gain, zeros, zeros, F32).reshape(b, s, d)
```

```python
import functools
import math

import numpy as np
import jax
import jax.numpy as jnp
from jax import lax
from jax.experimental import pallas as pl
from jax.experimental.pallas import tpu as pltpu

F32 = jnp.float32
BF16 = jnp.bfloat16
HIGHEST = lax.Precision.HIGHEST

LANES = 128
VMEM_LIMIT = 52 * 1024 * 1024

NORM_EPS = 1e-6
MLSTM_HEADS = 4
MLSTM_CHUNK = 128
GATE_SOFTCAP = 15.0
RWKV_HEAD = 64
RWKV_GN_EPS = 64e-5
DIFF_HEADS = 8
DIFF_DQK = 128
DIFF_DV = 256
DIFF_SUBLN_EPS = 1e-5
ROPE_DIM = 32
ROPE_THETA = 500000.0
N_EXPERTS = 32
TOP_K = 4
SWIGLU_LIMIT = 7.0
SWIGLU_ALPHA = 1.702
MOE_ROWS = 256
TOK_BLOCK = 256


def _params(*sem):
    return pltpu.CompilerParams(dimension_semantics=sem, vmem_limit_bytes=VMEM_LIMIT)


def _bdot(a, b):
    return jnp.dot(a.astype(BF16), b.astype(BF16), preferred_element_type=F32)


def _dot_nt(a, b):
    return lax.dot_general(a.astype(BF16), b.astype(BF16), (((1,), (1,)), ((), ())),
                           preferred_element_type=F32)


def _dot_tn(a, b):
    return lax.dot_general(a.astype(BF16), b.astype(BF16), (((0,), (0,)), ((), ())),
                           preferred_element_type=F32)


def _ada_body(c_ref, w_ref, b_ref, o_ref, *, kc):
    d = w_ref.shape[1]

    def step(i, acc):
        cv = c_ref[pl.ds(i * kc, kc), :]
        sc = cv * jax.nn.sigmoid(cv)
        return acc + jnp.sum(sc * w_ref[0, pl.ds(i * kc, kc), :], axis=0, keepdims=True)

    acc = lax.fori_loop(0, d // kc, step, jnp.zeros((1, w_ref.shape[2]), F32))
    o_ref[0] = acc + b_ref[0]


def ada_mod(c, ada_w, ada_b, tn=1024, kc=128):
    depth, d, n = ada_w.shape
    return pl.pallas_call(
        functools.partial(_ada_body, kc=kc),
        grid=(depth, n // tn),
        in_specs=[pl.BlockSpec((d, 1), lambda l, j: (0, 0)),
                  pl.BlockSpec((1, d, tn), lambda l, j: (l, 0, j)),
                  pl.BlockSpec((1, 1, tn), lambda l, j: (l, 0, j))],
        out_specs=pl.BlockSpec((1, 1, tn), lambda l, j: (l, 0, j)),
        out_shape=jax.ShapeDtypeStruct((depth, 1, n), F32),
        compiler_params=_params("arbitrary", "arbitrary"),
        name="ada_mod",
    )(c.reshape(d, 1), ada_w, ada_b.reshape(depth, 1, n))


def _norm_rows(x, gain, shift, scale, eps):
    y = x * lax.rsqrt(jnp.mean(x * x, axis=-1, keepdims=True) + eps)
    return (y * gain) * (1.0 + scale) + shift


def _norm_mod_body(x_ref, g_ref, sh_ref, sc_ref, o_ref):
    o_ref[...] = _norm_rows(x_ref[...], g_ref[...], sh_ref[...], sc_ref[...], NORM_EPS).astype(o_ref.dtype)


def norm_mod(x, gain, shift, scale, out_dtype, tm=256):
    s, d = x.shape
    row = pl.BlockSpec((1, d), lambda i: (0, 0))
    return pl.pallas_call(
        _norm_mod_body,
        grid=(s // tm,),
        in_specs=[pl.BlockSpec((tm, d), lambda i: (i, 0)), row, row, row],
        out_specs=pl.BlockSpec((tm, d), lambda i: (i, 0)),
        out_shape=jax.ShapeDtypeStruct((s, d), out_dtype),
        compiler_params=_params("arbitrary"),
        name="norm_mod",
    )(x, gain.reshape(1, d), shift.reshape(1, d), scale.reshape(1, d))


def _mm_body(*refs, fused):
    if fused:
        x_ref, w_ref, res_ref, gate_ref, o_ref, wb_ref = refs
    else:
        x_ref, w_ref, o_ref, wb_ref = refs

    @pl.when(pl.program_id(1) == 0)
    def _():
        wb_ref[...] = w_ref[...].astype(BF16)

    acc = jnp.dot(x_ref[...].astype(BF16), wb_ref[...], preferred_element_type=F32)
    if fused:
        acc = res_ref[...] + gate_ref[...] * acc
    o_ref[...] = acc.astype(o_ref.dtype)


def matmul(x, w, out_dtype=F32, res=None, gate=None, tm=512, tn=1024):
    m, k = x.shape
    n = w.shape[1]
    tn = min(tn, n)
    tm = min(tm, m)
    fused = res is not None
    in_specs = [pl.BlockSpec((tm, k), lambda j, i: (i, 0)),
                pl.BlockSpec((k, tn), lambda j, i: (0, j))]
    args = [x, w]
    if fused:
        in_specs += [pl.BlockSpec((tm, tn), lambda j, i: (i, j)),
                     pl.BlockSpec((1, tn), lambda j, i: (0, j))]
        args += [res, gate.reshape(1, n)]
    return pl.pallas_call(
        functools.partial(_mm_body, fused=fused),
        grid=(n // tn, m // tm),
        in_specs=in_specs,
        out_specs=pl.BlockSpec((tm, tn), lambda j, i: (i, j)),
        out_shape=jax.ShapeDtypeStruct((m, n), out_dtype),
        scratch_shapes=[pltpu.VMEM((k, tn), BF16)],
        compiler_params=_params("arbitrary", "arbitrary"),
        name="matmul",
    )(*args)


def _mlstm_body(proj_ref, gt_ref, bg_ref, ng_ref, out_ref, c_s, n_s, m_s, *, dk, dv):
    H = MLSTM_HEADS

    @pl.when(pl.program_id(0) == 0)
    def _():
        c_s[...] = jnp.zeros_like(c_s)
        n_s[...] = jnp.zeros_like(n_s)
        m_s[...] = jnp.zeros_like(m_s)

    for h in range(H):
        _mlstm_head(h,
                    proj_ref.at[:, pl.ds(h * dk, dk)],
                    proj_ref.at[:, pl.ds(H * dk + h * dk, dk)],
                    proj_ref.at[:, pl.ds(2 * H * dk + h * dv, dv)],
                    proj_ref.at[:, pl.ds(2 * H * dk + H * dv + h * dv, dv)],
                    gt_ref, bg_ref, ng_ref.at[:, pl.ds(h * dv, dv)], out_ref.at[:, pl.ds(h * dv, dv)],
                    c_s.at[h], n_s.at[h], m_s.at[h])


def _mlstm_head(h, q_ref, k_ref, v_ref, o_ref, gt_ref, bg_ref, ng_ref, out_ref, c_s, n_s, m_s):
    L = q_ref.shape[0]
    dk = q_ref.shape[1]
    dv = v_ref.shape[1]

    lane = lax.broadcasted_iota(jnp.int32, (L, LANES), 1)
    row = lax.broadcasted_iota(jnp.int32, (L, L), 0)
    col = lax.broadcasted_iota(jnp.int32, (L, L), 1)

    capped = GATE_SOFTCAP * jnp.tanh((gt_ref[...] + bg_ref[...]) / GATE_SOFTCAP)
    li = jnp.sum(jnp.where(lane == h, capped, 0.0), axis=-1, keepdims=True)
    fpre = jnp.sum(jnp.where(lane == h + MLSTM_HEADS, capped, 0.0), axis=-1, keepdims=True)
    lf = jax.nn.log_sigmoid(fpre)

    tril = (row >= col).astype(F32)
    g_b = jnp.dot(tril, jnp.broadcast_to(lf, (L, L)), precision=HIGHEST,
                  preferred_element_type=F32)
    g = g_b[:, 0:1]
    g_tot = jnp.sum(lf, axis=0, keepdims=True)
    eye = row == col
    g_row = jnp.sum(jnp.where(eye, g_b, 0.0), axis=0, keepdims=True)
    li_row = jnp.sum(jnp.where(eye, jnp.broadcast_to(li, (L, L)), 0.0), axis=0, keepdims=True)

    m_prev = m_s[...]
    c_prev = c_s[...]
    n_prev = n_s[...]

    q = q_ref[...] * (dk ** -0.5)
    k = k_ref[...]
    vb = v_ref[...].astype(BF16)
    qb = q.astype(BF16)

    d_log = jnp.where(row >= col, g_b - g_row + li_row, -jnp.inf)
    inter_log = g + m_prev
    m_t = jnp.maximum(inter_log, jnp.max(d_log, axis=-1, keepdims=True))
    s = _dot_nt(qb, k) * jnp.exp(d_log - m_t)
    inter = jnp.exp(inter_log - m_t)
    num = _bdot(s, vb) + inter * _bdot(qb, c_prev)
    den = jnp.sum(s, axis=-1, keepdims=True) + inter * jnp.sum(q * n_prev, axis=-1, keepdims=True)
    hh = num / jnp.maximum(jnp.abs(den), jnp.exp(-m_t))
    hh = hh * lax.rsqrt(jnp.mean(hh * hh, axis=-1, keepdims=True) + NORM_EPS) * ng_ref[...]
    out_ref[...] = (jax.nn.sigmoid(o_ref[...]) * hh).astype(out_ref.dtype)

    w_state = g_tot - g + li
    m_loc = jnp.max(w_state, axis=0, keepdims=True)
    ks = jnp.exp(w_state - m_loc) * k
    kv = _dot_tn(ks, vb)
    n_c = jnp.sum(ks, axis=0, keepdims=True)
    m_new = jnp.maximum(g_tot + m_prev, m_loc)
    a = jnp.exp(g_tot + m_prev - m_new)
    b = jnp.exp(m_loc - m_new)
    c_s[...] = a * c_prev + b * kv
    n_s[...] = a * n_prev + b * n_c
    m_s[...] = m_new


def mlstm_cells(proj, gates, b_gates_pad, norm_gain):
    s = proj.shape[0]
    d = norm_gain.shape[0]
    H, L = MLSTM_HEADS, MLSTM_CHUNK
    dk = (proj.shape[1] - 2 * d) // (2 * H)
    dv = d // H
    n = proj.shape[1]
    return pl.pallas_call(
        functools.partial(_mlstm_body, dk=dk, dv=dv),
        grid=(s // L,),
        in_specs=[pl.BlockSpec((L, n), lambda c: (c, 0)),
                  pl.BlockSpec((L, LANES), lambda c: (c, 0)),
                  pl.BlockSpec((1, LANES), lambda c: (0, 0)),
                  pl.BlockSpec((1, d), lambda c: (0, 0))],
        out_specs=pl.BlockSpec((L, d), lambda c: (c, 0)),
        out_shape=jax.ShapeDtypeStruct((s, d), BF16),
        scratch_shapes=[pltpu.VMEM((H, dk, dv), F32), pltpu.VMEM((H, 1, dk), F32), pltpu.VMEM((H, 1, 1), F32)],
        compiler_params=_params("arbitrary"),
        name="mlstm_cells",
    )(proj, gates, b_gates_pad, norm_gain.reshape(1, d))


def mlstm_mixer(x, gain, shift, scale, w_in, b_gates, norm_gain, w_out, gate_row):
    d = x.shape[1]
    n_main = w_in.shape[1] - 2 * MLSTM_HEADS
    hm = norm_mod(x, gain, shift, scale, BF16)
    proj = matmul(hm, w_in[:, :n_main])
    w_g = jnp.pad(w_in[:, n_main:], ((0, 0), (0, LANES - 2 * MLSTM_HEADS)))
    gates = matmul(hm, w_g)
    b_pad = jnp.pad(b_gates, (0, LANES - 2 * MLSTM_HEADS)).reshape(1, LANES)
    cells = mlstm_cells(proj, gates, b_pad, norm_gain)
    return matmul(cells, w_out, res=x, gate=gate_row)


def _rwkv_prep_body(x_ref, xp_ref, g_ref, sh_ref, sc_ref, mu_ref, *outs):
    tm = x_ref.shape[0]
    hm = _norm_rows(x_ref[...], g_ref[...], sh_ref[...], sc_ref[...], NORM_EPS)
    prev = _norm_rows(xp_ref[7:8, :], g_ref[...], sh_ref[...], sc_ref[...], NORM_EPS)
    prev = jnp.where(pl.program_id(0) == 0, 0.0, prev)
    rows = lax.broadcasted_iota(jnp.int32, hm.shape, 0)
    shifted = jnp.where(rows == 0, prev, pltpu.roll(hm, 1, axis=0))
    xx = shifted - hm
    for j, o_ref in enumerate(outs):
        o_ref[...] = (hm + xx * mu_ref[j:j + 1, :]).astype(o_ref.dtype)


def rwkv_prep(x, gain, shift, scale, mu, tm=256):
    s, d = x.shape
    row = pl.BlockSpec((1, d), lambda i: (0, 0))
    blk = pl.BlockSpec((tm, d), lambda i: (i, 0))
    return pl.pallas_call(
        _rwkv_prep_body,
        grid=(s // tm,),
        in_specs=[blk, pl.BlockSpec((8, d), lambda i: (jnp.maximum(i * (tm // 8) - 1, 0), 0)),
                  row, row, row, pl.BlockSpec((8, d), lambda i: (0, 0))],
        out_specs=[blk] * 6,
        out_shape=[jax.ShapeDtypeStruct((s, d), BF16)] * 6,
        compiler_params=_params("arbitrary"),
        name="rwkv_prep",
    )(x, x, gain.reshape(1, d), shift.reshape(1, d), scale.reshape(1, d), jnp.pad(mu, ((0, 2), (0, 0))))


def _lora_body(x_ref, w1_ref, w2_ref, o_ref, *, act):
    t = _bdot(x_ref[...], w1_ref[...])
    if act == "tanh":
        t = jnp.tanh(t)
    elif act == "sigmoid":
        t = jax.nn.sigmoid(t)
    o_ref[...] = _bdot(t, w2_ref[...])


def lora(x, w1, w2, act, tm=512):
    s, d = x.shape
    r = w1.shape[1]
    rp = -(-r // LANES) * LANES
    w1p = jnp.pad(w1, ((0, 0), (0, rp - r)))
    w2p = jnp.pad(w2, ((0, rp - r), (0, 0)))
    n = w2.shape[1]
    return pl.pallas_call(
        functools.partial(_lora_body, act=act),
        grid=(s // tm,),
        in_specs=[pl.BlockSpec((tm, d), lambda i: (i, 0)),
                  pl.BlockSpec((d, rp), lambda i: (0, 0)),
                  pl.BlockSpec((rp, n), lambda i: (0, 0))],
        out_specs=pl.BlockSpec((tm, n), lambda i: (i, 0)),
        out_shape=jax.ShapeDtypeStruct((s, n), F32),
        compiler_params=_params("arbitrary"),
        name="lora",
    )(x, w1p, w2p)


def _split3(x):
    hi = x.astype(BF16)
    r1 = x - hi.astype(F32)
    mid = r1.astype(BF16)
    lo = (r1 - mid.astype(F32)).astype(BF16)
    return hi, mid, lo


def _segsum(x, bmat, passes):
    parts = _split3(x)[:passes] if passes > 1 else (x.astype(BF16),)
    acc = jnp.dot(parts[0], bmat, preferred_element_type=F32)
    for p in parts[1:]:
        acc = acc + jnp.dot(p, bmat, preferred_element_type=F32)
    return acc


SUBLANES = 8


SCAN_W = 256


def _head_ones(width):
    ri = lax.broadcasted_iota(jnp.int32, (width, width), 0)
    ci = lax.broadcasted_iota(jnp.int32, (width, width), 1)
    return ((ri // RWKV_HEAD) == (ci // RWKV_HEAD)).astype(BF16)


def _rwkv_pre_body(k_ref, wl_ref, al_ref, w0_ref, a0_ref, kk_ref, ka_ref, nkk_ref, w_ref, kka_ref, kp_ref):
    d = k_ref.shape[1]
    bmat = _head_ones(SCAN_W)
    for j in range(d // SCAN_W):
        cs = slice(j * SCAN_W, (j + 1) * SCAN_W)
        kraw = k_ref[:, cs]
        w_log = -jax.nn.softplus(-(w0_ref[:, cs] + wl_ref[:, cs])) - 0.5
        w_ref[:, cs] = -jnp.exp(w_log)
        a = jax.nn.sigmoid(a0_ref[:, cs] + al_ref[:, cs])
        kk = kraw * kk_ref[:, cs]
        nrm = jnp.sqrt(_segsum(kk * kk, bmat, 3))
        kk = kk / jnp.maximum(nrm, 1e-12)
        nkk_ref[:, cs] = -kk
        kka_ref[:, cs] = kk * a
        kp_ref[:, cs] = kraw * (1.0 + (a - 1.0) * ka_ref[:, cs])


def rwkv_pre(k, wl, al, w0, a0, k_k, k_a, tm=256):
    s, d = k.shape
    blk = pl.BlockSpec((tm, d), lambda i: (i, 0))
    row = pl.BlockSpec((1, d), lambda i: (0, 0))
    return pl.pallas_call(
        _rwkv_pre_body,
        grid=(s // tm,),
        in_specs=[blk] * 3 + [row] * 4,
        out_specs=[blk] * 4,
        out_shape=[jax.ShapeDtypeStruct((s, d), F32)] * 4,
        compiler_params=_params("arbitrary"),
        name="rwkv_pre",
    )(k, wl, al, *[t.reshape(1, d) for t in (w0, a0, k_k, k_a)])


def _rwkv_post_body(y_ref, r_ref, kp_ref, v_ref, gt_ref, rk_ref, lg_ref, lb_ref, o_ref):
    d = y_ref.shape[1]
    bmat = _head_ones(SCAN_W)
    inv = 1.0 / RWKV_HEAD
    for j in range(d // SCAN_W):
        cs = slice(j * SCAN_W, (j + 1) * SCAN_W)
        y = y_ref[:, cs]
        mean = _segsum(y, bmat, 3) * inv
        yc = y - mean
        var = _segsum(yc * yc, bmat, 3) * inv
        yn = yc * lax.rsqrt(var + RWKV_GN_EPS) * lg_ref[:, cs] + lb_ref[:, cs]
        bonus = _segsum(r_ref[:, cs] * kp_ref[:, cs] * rk_ref[:, cs], bmat, 3) * v_ref[:, cs]
        o_ref[:, cs] = ((yn + bonus) * gt_ref[:, cs]).astype(o_ref.dtype)


def rwkv_post(y, r, kp, v, gt, r_k, ln_gain, ln_bias, tm=256):
    s, d = y.shape
    blk = pl.BlockSpec((tm, d), lambda i: (i, 0))
    row = pl.BlockSpec((1, d), lambda i: (0, 0))
    return pl.pallas_call(
        _rwkv_post_body,
        grid=(s // tm,),
        in_specs=[blk] * 5 + [row] * 3,
        out_specs=blk,
        out_shape=jax.ShapeDtypeStruct((s, d), BF16),
        compiler_params=_params("arbitrary"),
        name="rwkv_post",
    )(y, r, kp, v, gt, *[t.reshape(1, d) for t in (r_k, ln_gain, ln_bias)])


RWKV_CHUNK = 32


def _rwkv_chunk_body(a_ref, lw_ref, b_ref, kp_ref, r_ref, v_ref, y_ref, st_s):
    L, d = r_ref.shape
    nv = RWKV_HEAD
    W = SCAN_W
    hpt = W // nv
    ngrp = L // SUBLANES

    @pl.when(pl.program_id(0) == 0)
    def _():
        st_s[...] = jnp.zeros_like(st_s)

    lane_head = lax.broadcasted_iota(jnp.int32, (1, W), 1) // nv
    heads = [lane_head == h for h in range(hpt)]
    ti = lax.broadcasted_iota(jnp.int32, (L, L), 0)
    si = lax.broadcasted_iota(jnp.int32, (L, L), 1)
    tril = (ti >= si).astype(F32)
    t_row = lax.broadcasted_iota(jnp.int32, (2 * L, 2 * hpt * L), 0)
    s_lane = lax.broadcasted_iota(jnp.int32, (2 * L, 2 * hpt * L), 1) % L
    g_mask = s_lane <= jnp.where(t_row < L, t_row - 1, t_row - L)
    sub = lax.broadcasted_iota(jnp.int32, (SUBLANES, W), 0)
    ones = _head_ones(W)

    def stack(x):
        return jnp.concatenate([jnp.where(hm, x, 0.0) for hm in heads], axis=0)

    for j in range(d // W):
        cs = slice(j * W, (j + 1) * W)
        rs = slice(j * nv, (j + 1) * nv)
        lw = lw_ref[:, cs]
        cum = jnp.dot(tril, lw, precision=HIGHEST, preferred_element_type=F32)
        tot = cum[L - 1:L, :]
        w_inv = jnp.exp(-cum)
        w_rest = jnp.exp(tot - cum)
        a_hat = a_ref[:, cs] * jnp.exp(cum - lw)
        r_hat = r_ref[:, cs] * jnp.exp(cum)
        k_hat = kp_ref[:, cs] * w_inv
        b_hat = b_ref[:, cs] * w_inv
        v = v_ref[:, cs]
        s0 = st_s[rs, :]

        ar = jnp.concatenate([a_hat, r_hat], axis=0).astype(BF16)
        uy0 = _dot_nt(ar, stack(s0))
        g = _dot_nt(ar, jnp.concatenate([stack(k_hat), stack(b_hat)], axis=0))
        g = jnp.where(g_mask, g, 0.0)
        v_st = stack(v)
        base = uy0[0:L] + _bdot(g[0:L, 0:hpt * L], v_st)

        lhs = jnp.concatenate([a_hat * b_hat[s:s + 1, :] for s in range(L)], axis=0)
        coef = jnp.dot(lhs.astype(BF16), ones, preferred_element_type=F32)
        u = [base[g8 * SUBLANES:(g8 + 1) * SUBLANES] for g8 in range(ngrp)]
        for s in range(L - 1):
            g0, i0 = divmod(s, SUBLANES)
            us = jnp.broadcast_to(u[g0][i0:i0 + 1, :], (SUBLANES, W))
            for g8 in range(g0, ngrp):
                c = coef[s * L + g8 * SUBLANES:s * L + (g8 + 1) * SUBLANES]
                if g8 == g0:
                    c = jnp.where(sub > i0, c, 0.0)
                u[g8] = u[g8] + c * us
        u = jnp.concatenate(u, axis=0)

        vu = jnp.concatenate([v_st, stack(u)], axis=0)
        y_ref[:, cs] = uy0[L:2 * L] + _bdot(g[L:2 * L], vu)

        kb_rest = jnp.concatenate([kp_ref[:, cs] * w_rest, b_ref[:, cs] * w_rest], axis=0)
        upd = _dot_tn(jnp.concatenate([v, u], axis=0), kb_rest)
        new = s0 * jnp.exp(tot)
        for h, hm in enumerate(heads):
            new = new + jnp.where(hm, upd[h * nv:(h + 1) * nv], 0.0)
        st_s[rs, :] = new


def rwkv_chunked(nkk, lw, kka, kp, r, v):
    s, d = r.shape
    L = RWKV_CHUNK
    blk = pl.BlockSpec((L, d), lambda i: (i, 0))
    return pl.pallas_call(
        _rwkv_chunk_body,
        grid=(s // L,),
        in_specs=[blk] * 6,
        out_specs=blk,
        out_shape=jax.ShapeDtypeStruct((s, d), F32),
        scratch_shapes=[pltpu.VMEM(((d // SCAN_W) * RWKV_HEAD, SCAN_W), F32)],
        compiler_params=_params("arbitrary"),
        name="rwkv_chunked",
    )(nkk, lw, kka, kp, r, v)


def rwkv_mixer(x, gain, shift, scale, mu, w_rkv, w0, w1, w2, a0, a1, a2, g1, g2, k_k, k_a, r_k,
               ln_gain, ln_bias, w_o, gate_row):
    xr, xk, xv, xw, xa, xg = rwkv_prep(x, gain, shift, scale, mu)
    r = matmul(xr, w_rkv[0])
    k = matmul(xk, w_rkv[1])
    v = matmul(xv, w_rkv[2])
    wl = lora(xw, w1, w2, "tanh")
    al = lora(xa, a1, a2, "none")
    gt = lora(xg, g1, g2, "sigmoid")
    nkk, lw, kka, kp = rwkv_pre(k, wl, al, w0, a0, k_k, k_a)
    y = rwkv_chunked(nkk, lw, kka, kp, r, v)
    y = rwkv_post(y, r, kp, v, gt, r_k.reshape(-1), ln_gain, ln_bias)
    return matmul(y, w_o, res=x, gate=gate_row)


def _rope_body(p_ref, pos_ref, o_ref, *, n_rot_tiles, q_tiles):
    tm, n = p_ref.shape
    half = ROPE_DIM // 2
    lane = lax.broadcasted_iota(jnp.int32, (tm, LANES), 1)
    inv_freq = jnp.exp((lane % half).astype(F32) * (-2.0 * math.log(ROPE_THETA) / ROPE_DIM))
    ang = pos_ref[...].astype(F32) * inv_freq
    roped = lane < ROPE_DIM
    cos = jnp.where(roped, jnp.cos(ang), 1.0)
    sin = jnp.where(roped, jnp.where(lane < half, -jnp.sin(ang), jnp.sin(ang)), 0.0)
    for j in range(n // LANES):
        x = p_ref[:, j * LANES:(j + 1) * LANES]
        if j < n_rot_tiles:
            partner = jnp.where(lane < half, pltpu.roll(x, LANES - half, axis=1), pltpu.roll(x, half, axis=1))
            x = x * cos + partner * sin
            if j < q_tiles:
                x = x * (DIFF_DQK ** -0.5)
        o_ref[:, j * LANES:(j + 1) * LANES] = x.astype(o_ref.dtype)


def rope_cast(proj, positions, tm=256):
    s, n = proj.shape
    q_tiles = 2 * DIFF_HEADS
    return pl.pallas_call(
        functools.partial(_rope_body, n_rot_tiles=2 * q_tiles, q_tiles=q_tiles),
        grid=(s // tm,),
        in_specs=[pl.BlockSpec((tm, n), lambda i: (i, 0)),
                  pl.BlockSpec((tm, 1), lambda i: (i, 0))],
        out_specs=pl.BlockSpec((tm, n), lambda i: (i, 0)),
        out_shape=jax.ShapeDtypeStruct((s, n), BF16),
        compiler_params=_params("arbitrary"),
        name="rope_cast",
    )(proj, positions.reshape(s, 1))


def _diff_attn_body(qi_ref, kj_ref, q_ref, k_ref, v_ref, lam_ref, sg_ref, o_ref, m_s, l_s, acc_s, *, lambda_init):
    step = pl.program_id(1)
    qi, kj = qi_ref[step], kj_ref[step]
    tb = q_ref.shape[0]

    @pl.when(kj == 0)
    def _():
        m_s[...] = jnp.full_like(m_s, -jnp.inf)
        l_s[...] = jnp.zeros_like(l_s)
        acc_s[...] = jnp.zeros_like(acc_s)

    def accumulate(diagonal):
        v = v_ref[...]
        for c in range(2):
            cs = slice(c * DIFF_DQK, (c + 1) * DIFF_DQK)
            s = _dot_nt(q_ref[:, cs], k_ref[:, cs])
            if diagonal:
                rows = lax.broadcasted_iota(jnp.int32, (tb, tb), 0)
                cols = lax.broadcasted_iota(jnp.int32, (tb, tb), 1)
                s = jnp.where(cols <= rows, s, -jnp.inf)
            m_old = m_s[c]
            m_new = jnp.maximum(m_old, jnp.max(s, axis=-1, keepdims=True))
            alpha = jnp.exp(m_old - m_new)
            p = jnp.exp(s - m_new)
            l_s[c] = alpha * l_s[c] + jnp.sum(p, axis=-1, keepdims=True)
            acc_s[c] = alpha * acc_s[c] + _bdot(p, v)
            m_s[c] = m_new

    @pl.when(kj < qi)
    def _():
        accumulate(False)

    @pl.when(kj == qi)
    def _():
        accumulate(True)
        lp = lam_ref[...]
        lam = (jnp.exp(jnp.sum(lp[0:1] * lp[1:2], axis=-1, keepdims=True))
               - jnp.exp(jnp.sum(lp[2:3] * lp[3:4], axis=-1, keepdims=True)) + lambda_init)
        out = acc_s[0] / l_s[0] - lam * (acc_s[1] / l_s[1])
        out = out * lax.rsqrt(jnp.mean(out * out, axis=-1, keepdims=True) + DIFF_SUBLN_EPS)
        o_ref[...] = (out * sg_ref[...] * (1.0 - lambda_init)).astype(o_ref.dtype)


def diff_attention(qkv, lam_params, subln_gain, lambda_init, tb=1024):
    s = qkv.shape[0]
    H, dv = DIFF_HEADS, DIFF_DV
    tb = min(tb, s)
    nb = s // tb
    pairs = [(qi, kj) for qi in range(nb) for kj in range(qi + 1)]
    qi_tab = jnp.asarray([p[0] for p in pairs], jnp.int32)
    kj_tab = jnp.asarray([p[1] for p in pairs], jnp.int32)
    grid_spec = pltpu.PrefetchScalarGridSpec(
        num_scalar_prefetch=2,
        grid=(H, len(pairs)),
        in_specs=[pl.BlockSpec((tb, dv), lambda h, t, qi, kj: (qi[t], h)),
                  pl.BlockSpec((tb, dv), lambda h, t, qi, kj: (kj[t], H + h)),
                  pl.BlockSpec((tb, dv), lambda h, t, qi, kj: (kj[t], 2 * H + h)),
                  pl.BlockSpec((4, DIFF_DQK), lambda h, t, qi, kj: (0, 0)),
                  pl.BlockSpec((1, dv), lambda h, t, qi, kj: (0, 0))],
        out_specs=pl.BlockSpec((tb, dv), lambda h, t, qi, kj: (qi[t], h)),
        scratch_shapes=[pltpu.VMEM((2, tb, 1), F32), pltpu.VMEM((2, tb, 1), F32),
                        pltpu.VMEM((2, tb, dv), F32)],
    )
    return pl.pallas_call(
        functools.partial(_diff_attn_body, lambda_init=lambda_init),
        grid_spec=grid_spec,
        out_shape=jax.ShapeDtypeStruct((s, H * dv), BF16),
        compiler_params=_params("arbitrary", "arbitrary"),
        name="diff_attention",
    )(qi_tab, kj_tab, qkv, qkv, qkv, lam_params, subln_gain.reshape(1, dv))


def diff_mixer(x, gain, shift, scale, positions, w_qkv, lam_params, subln_gain, w_o, layer, gate_row):
    hm = norm_mod(x, gain, shift, scale, BF16)
    proj = matmul(hm, w_qkv)
    qkv = rope_cast(proj, positions)
    lambda_init = 0.8 - 0.6 * math.exp(-0.3 * layer)
    att = diff_attention(qkv, lam_params, subln_gain, lambda_init)
    return matmul(att, w_o, res=x, gate=gate_row)


def _router_body(x_ref, g_ref, sh_ref, sc_ref, rw_ref, rb_ref, hf_ref, lg_ref):
    hf = _norm_rows(x_ref[...], g_ref[...], sh_ref[...], sc_ref[...], NORM_EPS)
    hf_ref[...] = hf
    lg_ref[...] = jnp.dot(hf, rw_ref[...], precision=HIGHEST, preferred_element_type=F32) + rb_ref[...]


def router(x, gain, shift, scale, router_w, router_b, tm=256):
    s, d = x.shape
    e = router_w.shape[1]
    rw = jnp.pad(router_w, ((0, 0), (0, LANES - e)))
    rb = jnp.pad(router_b, (0, LANES - e), constant_values=-1e30).reshape(1, LANES)
    row = pl.BlockSpec((1, d), lambda i: (0, 0))
    return pl.pallas_call(
        _router_body,
        grid=(s // tm,),
        in_specs=[pl.BlockSpec((tm, d), lambda i: (i, 0)), row, row, row,
                  pl.BlockSpec((d, LANES), lambda i: (0, 0)), pl.BlockSpec((1, LANES), lambda i: (0, 0))],
        out_specs=[pl.BlockSpec((tm, d), lambda i: (i, 0)), pl.BlockSpec((tm, LANES), lambda i: (i, 0))],
        out_shape=[jax.ShapeDtypeStruct((s, d), F32), jax.ShapeDtypeStruct((s, LANES), F32)],
        compiler_params=_params("arbitrary"),
        name="router",
    )(x, gain.reshape(1, d), shift.reshape(1, d), scale.reshape(1, d), rw, rb)


def _topk_body(lg_ref, eidx_ref, gw_ref, rk_ref, cnt_ref):
    tm = lg_ref.shape[0]

    @pl.when(pl.program_id(0) == 0)
    def _():
        cnt_ref[...] = jnp.zeros_like(cnt_ref)

    lane = lax.broadcasted_iota(jnp.int32, (tm, LANES), 1).astype(F32)
    work = lg_ref[...]
    vals, idxs, hots = [], [], []
    for _ in range(TOP_K):
        m = jnp.max(work, axis=-1, keepdims=True)
        idx = jnp.min(jnp.where(work == m, lane, float(LANES)), axis=-1, keepdims=True)
        hot = lane == idx
        work = jnp.where(hot, -jnp.inf, work)
        vals.append(m)
        idxs.append(idx)
        hots.append(hot)
    exps = [jnp.exp(v - vals[0]) for v in vals]
    den = exps[0] + exps[1] + exps[2] + exps[3]
    sel = jnp.zeros((tm, LANES), F32)
    for hot in hots:
        sel = sel + hot.astype(F32)
    ri = lax.broadcasted_iota(jnp.int32, (tm, tm), 0)
    ci = lax.broadcasted_iota(jnp.int32, (tm, tm), 1)
    before = jnp.dot((ri > ci).astype(BF16), sel.astype(BF16), preferred_element_type=F32)
    rank = before + cnt_ref[...]
    cnt_ref[...] = cnt_ref[...] + jnp.sum(sel, axis=0, keepdims=True)
    eidx = jnp.zeros((tm, LANES), F32)
    gw = jnp.zeros((tm, LANES), F32)
    rk = jnp.zeros((tm, LANES), F32)
    for kslot in range(TOP_K):
        here = lane == kslot
        eidx = jnp.where(here, idxs[kslot], eidx)
        gw = jnp.where(here, exps[kslot] / den, gw)
        rk = jnp.where(here, jnp.sum(jnp.where(hots[kslot], rank, 0.0), axis=-1, keepdims=True), rk)
    eidx_ref[...] = eidx
    gw_ref[...] = gw
    rk_ref[...] = rk


def topk_route(logits, tm=TOK_BLOCK):
    s = logits.shape[0]
    blk = pl.BlockSpec((tm, LANES), lambda i: (i, 0))
    return pl.pallas_call(
        _topk_body,
        grid=(s // tm,),
        in_specs=[blk],
        out_specs=[blk, blk, blk, pl.BlockSpec((1, LANES), lambda i: (0, 0))],
        out_shape=[jax.ShapeDtypeStruct((s, LANES), F32), jax.ShapeDtypeStruct((s, LANES), F32),
                   jax.ShapeDtypeStruct((s, LANES), F32), jax.ShapeDtypeStruct((1, LANES), F32)],
        compiler_params=_params("arbitrary"),
        name="topk_route",
    )(logits)


def _slot_body(eidx_ref, rk_ref, ps_ref, pos_ref):
    tm = eidx_ref.shape[0]
    lane = lax.broadcasted_iota(jnp.int32, (tm, LANES), 1).astype(F32)
    eidx = eidx_ref[...]
    rk = rk_ref[...]
    pos = jnp.zeros((tm, LANES), F32)
    for kslot in range(TOP_K):
        e_k = jnp.sum(jnp.where(lane == kslot, eidx, 0.0), axis=-1, keepdims=True)
        start = jnp.sum(jnp.where(lane == e_k, ps_ref[...], 0.0), axis=-1, keepdims=True)
        r_k = jnp.sum(jnp.where(lane == kslot, rk, 0.0), axis=-1, keepdims=True)
        pos = jnp.where(lane == kslot, start + r_k, pos)
    pos_ref[...] = pos.astype(jnp.int32)


def slot_positions(eidx, rk, padded_start, tm=TOK_BLOCK):
    s = eidx.shape[0]
    blk = pl.BlockSpec((tm, LANES), lambda i: (i, 0))
    return pl.pallas_call(
        _slot_body,
        grid=(s // tm,),
        in_specs=[blk, blk, pl.BlockSpec((1, LANES), lambda i: (0, 0))],
        out_specs=blk,
        out_shape=jax.ShapeDtypeStruct((s, LANES), jnp.int32),
        compiler_params=_params("arbitrary"),
        name="slot_positions",
    )(eidx, rk, padded_start)


def _dispatch_body(pos_ref, hf_ref, xs_in_ref, xs_ref, sem):
    del xs_in_ref
    tm = hf_ref.shape[0]

    def issue(t, carry):
        for kslot in range(TOP_K):
            p = pos_ref[t * TOP_K + kslot]
            pltpu.make_async_copy(hf_ref.at[pl.ds(t, 1)], xs_ref.at[pl.ds(p, 1)], sem.at[0]).start(
                priority=kslot % 2)
        return carry

    lax.fori_loop(0, tm, issue, 0)

    for _ in range(TOP_K):
        pltpu.make_async_copy(hf_ref, xs_ref.at[pl.ds(0, tm)], sem.at[0]).wait()


def dispatch(pos_flat, hf, n_rows, tm=TOK_BLOCK):
    s, d = hf.shape
    return pl.pallas_call(
        _dispatch_body,
        grid=(s // tm,),
        in_specs=[pl.BlockSpec((tm * TOP_K,), lambda i: (i,), memory_space=pltpu.SMEM),
                  pl.BlockSpec((tm, d), lambda i: (i, 0)),
                  pl.BlockSpec(memory_space=pl.ANY)],
        out_specs=pl.BlockSpec(memory_space=pl.ANY),
        out_shape=jax.ShapeDtypeStruct((n_rows, d), hf.dtype),
        scratch_shapes=[pltpu.SemaphoreType.DMA((1,))],
        input_output_aliases={2: 0},
        compiler_params=_params("arbitrary"),
        name="moe_dispatch",
    )(pos_flat, hf, jnp.zeros((n_rows, d), hf.dtype))


def _expert_body(be_ref, first_ref, nxt_ref, tot_ref, xs_ref, wgu_hbm, bgu_ref, wdn_hbm, bdn_ref, perm_ref, ys_ref,
                 stg_gu, stg_dn, wgu_s, wdn_s, sem, *, layer, chunk):
    i = pl.program_id(0)
    rows = xs_ref.shape[0]
    d, n_gu = wgu_s.shape
    dh = wdn_s.shape[0]
    grp = perm_ref.shape[0]
    half = grp // 2

    def fetch(e):
        return (pltpu.make_async_copy(wgu_hbm.at[layer, e], stg_gu, sem.at[0]),
                pltpu.make_async_copy(wdn_hbm.at[layer, e], stg_dn, sem.at[1]))

    @pl.when(first_ref[i] == 1)
    def _():
        @pl.when(i == 0)
        def _():
            for cp in fetch(be_ref[0]):
                cp.start()

        for cp in fetch(be_ref[i]):
            cp.wait()
        for c in range(d // chunk):
            wgu_s[c * chunk:(c + 1) * chunk, :] = stg_gu[c * chunk:(c + 1) * chunk, :].astype(BF16)
        for c in range(dh // chunk):
            wdn_s[c * chunk:(c + 1) * chunk, :] = stg_dn[c * chunk:(c + 1) * chunk, :].astype(BF16)

        @pl.when(nxt_ref[i] >= 0)
        def _():
            for cp in fetch(nxt_ref[i]):
                cp.start()

    @pl.when(i * rows < tot_ref[0])
    def _():
        gu = _bdot(xs_ref[...], wgu_s[...]) + bgu_ref[0]
        gub = gu.astype(BF16)
        gs, us = [], []
        for j in range(n_gu // grp):
            pj = jnp.dot(gub[:, j * grp:(j + 1) * grp], perm_ref[...], preferred_element_type=F32)
            gs.append(pj[:, :half])
            us.append(pj[:, half:])
        g = jnp.minimum(jnp.concatenate(gs, axis=1), SWIGLU_LIMIT)
        u = jnp.clip(jnp.concatenate(us, axis=1), -SWIGLU_LIMIT, SWIGLU_LIMIT)
        act = (u + 1.0) * (g * jax.nn.sigmoid(g * SWIGLU_ALPHA))
        ys_ref[...] = _bdot(act, wdn_s[...]) + bdn_ref[0]

    @pl.when(i * rows >= tot_ref[0])
    def _():
        ys_ref[...] = jnp.zeros_like(ys_ref)


def expert_ffn(blk_expert, blk_first, blk_next, total_rows, xs, w_gu, b_gu, w_dn, b_dn, layer, rows=MOE_ROWS):
    n_rows, d = xs.shape
    _, e, _, n_gu = w_gu.shape
    dh = w_dn.shape[2]
    grp = 2 * LANES
    perm = np.zeros((grp, grp), np.float32)
    perm[2 * np.arange(LANES), np.arange(LANES)] = 1.0
    perm[2 * np.arange(LANES) + 1, LANES + np.arange(LANES)] = 1.0
    grid_spec = pltpu.PrefetchScalarGridSpec(
        num_scalar_prefetch=4,
        grid=(n_rows // rows,),
        in_specs=[pl.BlockSpec((rows, d), lambda i, be, fi, nx, tot: (i, 0)),
                  pl.BlockSpec(memory_space=pl.ANY),
                  pl.BlockSpec((1, 1, n_gu), lambda i, be, fi, nx, tot: (be[i], 0, 0)),
                  pl.BlockSpec(memory_space=pl.ANY),
                  pl.BlockSpec((1, 1, d), lambda i, be, fi, nx, tot: (be[i], 0, 0)),
                  pl.BlockSpec((grp, grp), lambda i, be, fi, nx, tot: (0, 0))],
        out_specs=pl.BlockSpec((rows, d), lambda i, be, fi, nx, tot: (i, 0)),
        scratch_shapes=[pltpu.VMEM((d, n_gu), F32), pltpu.VMEM((dh, d), F32),
                        pltpu.VMEM((d, n_gu), BF16), pltpu.VMEM((dh, d), BF16),
                        pltpu.SemaphoreType.DMA((2,))],
    )
    return pl.pallas_call(
        functools.partial(_expert_body, layer=layer, chunk=256),
        grid_spec=grid_spec,
        out_shape=jax.ShapeDtypeStruct((n_rows, d), F32),
        compiler_params=_params("arbitrary"),
        name="moe_experts",
    )(blk_expert, blk_first, blk_next, total_rows, xs, w_gu, b_gu.reshape(e, 1, n_gu), w_dn,
      b_dn.reshape(e, 1, d), jnp.asarray(perm, BF16))


def _combine_body(pos_ref, x_ref, gw_ref, g2_ref, ys_ref, o_ref, buf, sem):
    tm = x_ref.shape[0]

    def issue(t, carry):
        for kslot in range(TOP_K):
            p = pos_ref[t * TOP_K + kslot]
            pltpu.make_async_copy(ys_ref.at[pl.ds(p, 1)], buf.at[kslot, pl.ds(t, 1)], sem.at[0]).start(
                priority=kslot % 2)
        return carry

    lax.fori_loop(0, tm, issue, 0)

    for kslot in range(TOP_K):
        pltpu.make_async_copy(ys_ref.at[pl.ds(0, tm)], buf.at[kslot], sem.at[0]).wait()

    gw = gw_ref[...]
    acc = gw[:, 0:1] * buf[0]
    for kslot in range(1, TOP_K):
        acc = acc + gw[:, kslot:kslot + 1] * buf[kslot]
    o_ref[...] = x_ref[...] + g2_ref[...] * acc


def combine(pos_flat, x, gw, g2_row, ys, tm=TOK_BLOCK):
    s, d = x.shape
    return pl.pallas_call(
        _combine_body,
        grid=(s // tm,),
        in_specs=[pl.BlockSpec((tm * TOP_K,), lambda i: (i,), memory_space=pltpu.SMEM),
                  pl.BlockSpec((tm, d), lambda i: (i, 0)),
                  pl.BlockSpec((tm, LANES), lambda i: (i, 0)),
                  pl.BlockSpec((1, d), lambda i: (0, 0)),
                  pl.BlockSpec(memory_space=pl.ANY)],
        out_specs=pl.BlockSpec((tm, d), lambda i: (i, 0)),
        out_shape=jax.ShapeDtypeStruct((s, d), F32),
        scratch_shapes=[pltpu.VMEM((TOP_K, tm, d), F32), pltpu.SemaphoreType.DMA((1,))],
        compiler_params=_params("arbitrary"),
        name="moe_combine",
    )(pos_flat, x, gw, g2_row.reshape(1, d), ys)


def moe_layer(x, gain, shift, scale, g2_row, router_w, router_b, w_gu, b_gu, w_dn, b_dn, layer):
    s, d = x.shape
    e = router_w.shape[1]
    hf, logits = router(x, gain, shift, scale, router_w, router_b)
    eidx, gw, rk, counts = topk_route(logits)
    cnt = counts[0, :e].astype(jnp.int32)
    padded = (cnt + MOE_ROWS - 1) // MOE_ROWS * MOE_ROWS
    padded_end = jnp.cumsum(padded)
    padded_start = jnp.pad((padded_end - padded).astype(F32), (0, LANES - e)).reshape(1, LANES)
    n_rows = s * TOP_K + e * MOE_ROWS
    blk_first_row = jnp.arange(n_rows // MOE_ROWS, dtype=jnp.int32) * MOE_ROWS
    blk_expert = jnp.minimum(
        jnp.sum((padded_end[None, :] <= blk_first_row[:, None]).astype(jnp.int32), axis=1), e - 1)
    total_rows = padded_end[-1:].astype(jnp.int32)
    n_blk = n_rows // MOE_ROWS
    blk_id = jnp.arange(n_blk, dtype=jnp.int32)
    prev_expert = jnp.concatenate([jnp.full((1,), -1, jnp.int32), blk_expert[:-1]])
    blk_first = ((blk_first_row < total_rows[0]) & (blk_expert != prev_expert)).astype(jnp.int32)
    first_at = jnp.where(blk_first == 1, blk_id, n_blk)
    next_first = jnp.concatenate([lax.cummin(first_at, reverse=True)[1:], jnp.full((1,), n_blk, jnp.int32)])
    blk_next = jnp.where(next_first < n_blk, blk_expert[jnp.minimum(next_first, n_blk - 1)], -1).astype(jnp.int32)
    pos = slot_positions(eidx, rk, padded_start)
    pos_flat = pos[:, :TOP_K].reshape(-1)
    xs = dispatch(pos_flat, hf, n_rows)
    ys = expert_ffn(blk_expert, blk_first, blk_next, total_rows, xs, w_gu, b_gu, w_dn, b_dn, layer)
    return combine(pos_flat, x, gw, g2_row, ys)


def kernel(x, c, positions, norm_gain, ada_w, ada_b, mlstm_w_in, mlstm_b_gates, mlstm_norm_gain, mlstm_w_out, rwkv_mu, rwkv_w_rkv, rwkv_w0, rwkv_w1, rwkv_w2, rwkv_a0, rwkv_a1, rwkv_a2, rwkv_g1, rwkv_g2, rwkv_k_k, rwkv_k_a, rwkv_r_k, rwkv_ln_gain, rwkv_ln_bias, rwkv_w_o, diff_w_qkv, diff_lambda, diff_subln_gain, diff_w_o, moe_router_w, moe_router_b, moe_w_gate_up, moe_b_gate_up, moe_w_down, moe_b_down, final_gain):
    b, s, d = x.shape
    assert b == 1
    depth = norm_gain.shape[0]
    xt = x.reshape(s, d)
    pos = positions.reshape(s)
    mod = ada_mod(c, ada_w, ada_b).reshape(depth, 6, d)
    seen = [0, 0, 0]
    for layer in range(depth):
        kind = layer % 3
        j = seen[kind]
        seen[kind] += 1
        sh1, sc1, g1, sh2, sc2, g2 = (mod[layer, i] for i in range(6))
        gain = norm_gain[layer, 0]
        if kind == 0:
            xt = mlstm_mixer(xt, gain, sh1, sc1, mlstm_w_in[j], mlstm_b_gates[j], mlstm_norm_gain[j],
                             mlstm_w_out[j], g1)
        elif kind == 1:
            xt = rwkv_mixer(xt, gain, sh1, sc1, rwkv_mu[j], rwkv_w_rkv[j], rwkv_w0[j], rwkv_w1[j], rwkv_w2[j],
                            rwkv_a0[j], rwkv_a1[j], rwkv_a2[j], rwkv_g1[j], rwkv_g2[j], rwkv_k_k[j],
                            rwkv_k_a[j], rwkv_r_k[j], rwkv_ln_gain[j], rwkv_ln_bias[j], rwkv_w_o[j], g1)
        else:
            xt = diff_mixer(xt, gain, sh1, sc1, pos, diff_w_qkv[j], diff_lambda[j], diff_subln_gain[j],
                            diff_w_o[j], layer, g1)
        xt = moe_layer(xt, norm_gain[layer, 1], sh2, sc2, g2, moe_router_w[layer], moe_router_b[layer],
                       moe_w_gate_up, moe_b_gate_up[layer], moe_w_down, moe_b_down[layer], layer)
    zeros = jnp.zeros((d,), F32)
    return norm_mod(xt, final_gain, zeros, zeros, F32).reshape(b, s, d)
```

```python
import functools
import math

import numpy as np
import jax
import jax.numpy as jnp
from jax import lax
from jax.experimental import pallas as pl
from jax.experimental.pallas import tpu as pltpu

F32 = jnp.float32
BF16 = jnp.bfloat16
HIGHEST = lax.Precision.HIGHEST

LANES = 128
VMEM_LIMIT = 52 * 1024 * 1024

NORM_EPS = 1e-6
MLSTM_HEADS = 4
MLSTM_CHUNK = 128
GATE_SOFTCAP = 15.0
RWKV_HEAD = 64
RWKV_GN_EPS = 64e-5
DIFF_HEADS = 8
DIFF_DQK = 128
DIFF_DV = 256
DIFF_SUBLN_EPS = 1e-5
ROPE_DIM = 32
ROPE_THETA = 500000.0
N_EXPERTS = 32
TOP_K = 4
SWIGLU_LIMIT = 7.0
SWIGLU_ALPHA = 1.702
MOE_ROWS = 256
TOK_BLOCK = 256


def _params(*sem):
    return pltpu.CompilerParams(dimension_semantics=sem, vmem_limit_bytes=VMEM_LIMIT)


def _bdot(a, b):
    return jnp.dot(a.astype(BF16), b.astype(BF16), preferred_element_type=F32)


def _dot_nt(a, b):
    return lax.dot_general(a.astype(BF16), b.astype(BF16), (((1,), (1,)), ((), ())),
                           preferred_element_type=F32)


def _dot_tn(a, b):
    return lax.dot_general(a.astype(BF16), b.astype(BF16), (((0,), (0,)), ((), ())),
                           preferred_element_type=F32)


def _ada_body(c_ref, w_ref, b_ref, o_ref, *, kc):
    d = w_ref.shape[1]

    def step(i, acc):
        cv = c_ref[pl.ds(i * kc, kc), :]
        sc = cv * jax.nn.sigmoid(cv)
        prod = sc * w_ref[0, pl.ds(i * kc, kc), :]
        for g in range(kc // 8):
            acc = acc + prod[g * 8:(g + 1) * 8]
        return acc

    acc = lax.fori_loop(0, d // kc, step, jnp.zeros((8, w_ref.shape[2]), F32))
    o_ref[0] = jnp.sum(acc, axis=0, keepdims=True) + b_ref[0]


def ada_mod(c, ada_w, ada_b, tn=1024, kc=128):
    depth, d, n = ada_w.shape
    return pl.pallas_call(
        functools.partial(_ada_body, kc=kc),
        grid=(depth, n // tn),
        in_specs=[pl.BlockSpec((d, 1), lambda l, j: (0, 0)),
                  pl.BlockSpec((1, d, tn), lambda l, j: (l, 0, j)),
                  pl.BlockSpec((1, 1, tn), lambda l, j: (l, 0, j))],
        out_specs=pl.BlockSpec((1, 1, tn), lambda l, j: (l, 0, j)),
        out_shape=jax.ShapeDtypeStruct((depth, 1, n), F32),
        compiler_params=_params("arbitrary", "arbitrary"),
        name="ada_mod",
    )(c.reshape(d, 1), ada_w, ada_b.reshape(depth, 1, n))


def _norm_rows(x, gain, shift, scale, eps):
    y = x * lax.rsqrt(jnp.mean(x * x, axis=-1, keepdims=True) + eps)
    return (y * gain) * (1.0 + scale) + shift


def _norm_mod_body(x_ref, g_ref, sh_ref, sc_ref, o_ref):
    o_ref[...] = _norm_rows(x_ref[...], g_ref[...], sh_ref[...], sc_ref[...], NORM_EPS).astype(o_ref.dtype)


def norm_mod(x, gain, shift, scale, out_dtype, tm=256):
    s, d = x.shape
    row = pl.BlockSpec((1, d), lambda i: (0, 0))
    return pl.pallas_call(
        _norm_mod_body,
        grid=(s // tm,),
        in_specs=[pl.BlockSpec((tm, d), lambda i: (i, 0)), row, row, row],
        out_specs=pl.BlockSpec((tm, d), lambda i: (i, 0)),
        out_shape=jax.ShapeDtypeStruct((s, d), out_dtype),
        compiler_params=_params("arbitrary"),
        name="norm_mod",
    )(x, gain.reshape(1, d), shift.reshape(1, d), scale.reshape(1, d))


def _mm_body(*refs, fused):
    if fused:
        x_ref, w_ref, res_ref, gate_ref, o_ref, wb_ref = refs
    else:
        x_ref, w_ref, o_ref, wb_ref = refs

    @pl.when(pl.program_id(1) == 0)
    def _():
        wb_ref[...] = w_ref[...].astype(BF16)

    acc = jnp.dot(x_ref[...].astype(BF16), wb_ref[...], preferred_element_type=F32)
    if fused:
        acc = res_ref[...] + gate_ref[...] * acc
    o_ref[...] = acc.astype(o_ref.dtype)


def matmul(x, w, out_dtype=F32, res=None, gate=None, tm=512, tn=1024):
    m, k = x.shape
    n = w.shape[1]
    tn = min(tn, n)
    tm = min(tm, m)
    fused = res is not None
    in_specs = [pl.BlockSpec((tm, k), lambda j, i: (i, 0)),
                pl.BlockSpec((k, tn), lambda j, i: (0, j))]
    args = [x, w]
    if fused:
        in_specs += [pl.BlockSpec((tm, tn), lambda j, i: (i, j)),
                     pl.BlockSpec((1, tn), lambda j, i: (0, j))]
        args += [res, gate.reshape(1, n)]
    return pl.pallas_call(
        functools.partial(_mm_body, fused=fused),
        grid=(n // tn, m // tm),
        in_specs=in_specs,
        out_specs=pl.BlockSpec((tm, tn), lambda j, i: (i, j)),
        out_shape=jax.ShapeDtypeStruct((m, n), out_dtype),
        scratch_shapes=[pltpu.VMEM((k, tn), BF16)],
        compiler_params=_params("arbitrary", "arbitrary"),
        name="matmul",
    )(*args)


def _mlstm_body(proj_ref, gt_ref, bg_ref, ng_ref, out_ref, c_s, n_s, m_s, *, dk, dv):
    H = MLSTM_HEADS

    @pl.when(pl.program_id(0) == 0)
    def _():
        c_s[...] = jnp.zeros_like(c_s)
        n_s[...] = jnp.zeros_like(n_s)
        m_s[...] = jnp.zeros_like(m_s)

    for h in range(H):
        _mlstm_head(h,
                    proj_ref.at[:, pl.ds(h * dk, dk)],
                    proj_ref.at[:, pl.ds(H * dk + h * dk, dk)],
                    proj_ref.at[:, pl.ds(2 * H * dk + h * dv, dv)],
                    proj_ref.at[:, pl.ds(2 * H * dk + H * dv + h * dv, dv)],
                    gt_ref, bg_ref, ng_ref.at[:, pl.ds(h * dv, dv)], out_ref.at[:, pl.ds(h * dv, dv)],
                    c_s.at[h], n_s.at[h], m_s.at[h])


def _mlstm_head(h, q_ref, k_ref, v_ref, o_ref, gt_ref, bg_ref, ng_ref, out_ref, c_s, n_s, m_s):
    L = q_ref.shape[0]
    dk = q_ref.shape[1]
    dv = v_ref.shape[1]

    lane = lax.broadcasted_iota(jnp.int32, (L, LANES), 1)
    row = lax.broadcasted_iota(jnp.int32, (L, L), 0)
    col = lax.broadcasted_iota(jnp.int32, (L, L), 1)

    capped = GATE_SOFTCAP * jnp.tanh((gt_ref[...] + bg_ref[...]) / GATE_SOFTCAP)
    li = jnp.sum(jnp.where(lane == h, capped, 0.0), axis=-1, keepdims=True)
    fpre = jnp.sum(jnp.where(lane == h + MLSTM_HEADS, capped, 0.0), axis=-1, keepdims=True)
    lf = jax.nn.log_sigmoid(fpre)

    tril = (row >= col).astype(F32)
    g_b = jnp.dot(tril, jnp.broadcast_to(lf, (L, L)), precision=HIGHEST,
                  preferred_element_type=F32)
    g = g_b[:, 0:1]
    g_tot = jnp.sum(lf, axis=0, keepdims=True)
    eye = row == col
    g_row = jnp.sum(jnp.where(eye, g_b, 0.0), axis=0, keepdims=True)
    li_row = jnp.sum(jnp.where(eye, jnp.broadcast_to(li, (L, L)), 0.0), axis=0, keepdims=True)

    m_prev = m_s[...]
    c_prev = c_s[...]
    n_prev = n_s[...]

    q = q_ref[...] * (dk ** -0.5)
    k = k_ref[...]
    vb = v_ref[...].astype(BF16)
    qb = q.astype(BF16)

    d_log = jnp.where(row >= col, g_b - g_row + li_row, -jnp.inf)
    inter_log = g + m_prev
    m_t = jnp.maximum(inter_log, jnp.max(d_log, axis=-1, keepdims=True))
    s = _dot_nt(qb, k) * jnp.exp(d_log - m_t)
    inter = jnp.exp(inter_log - m_t)
    num = _bdot(s, vb) + inter * _bdot(qb, c_prev)
    den = jnp.sum(s, axis=-1, keepdims=True) + inter * jnp.sum(q * n_prev, axis=-1, keepdims=True)
    hh = num / jnp.maximum(jnp.abs(den), jnp.exp(-m_t))
    hh = hh * lax.rsqrt(jnp.mean(hh * hh, axis=-1, keepdims=True) + NORM_EPS) * ng_ref[...]
    out_ref[...] = (jax.nn.sigmoid(o_ref[...]) * hh).astype(out_ref.dtype)

    w_state = g_tot - g + li
    m_loc = jnp.max(w_state, axis=0, keepdims=True)
    ks = jnp.exp(w_state - m_loc) * k
    kv = _dot_tn(ks, vb)
    n_c = jnp.sum(ks, axis=0, keepdims=True)
    m_new = jnp.maximum(g_tot + m_prev, m_loc)
    a = jnp.exp(g_tot + m_prev - m_new)
    b = jnp.exp(m_loc - m_new)
    c_s[...] = a * c_prev + b * kv
    n_s[...] = a * n_prev + b * n_c
    m_s[...] = m_new


def mlstm_cells(proj, gates, b_gates_pad, norm_gain):
    s = proj.shape[0]
    d = norm_gain.shape[0]
    H, L = MLSTM_HEADS, MLSTM_CHUNK
    dk = (proj.shape[1] - 2 * d) // (2 * H)
    dv = d // H
    n = proj.shape[1]
    return pl.pallas_call(
        functools.partial(_mlstm_body, dk=dk, dv=dv),
        grid=(s // L,),
        in_specs=[pl.BlockSpec((L, n), lambda c: (c, 0)),
                  pl.BlockSpec((L, LANES), lambda c: (c, 0)),
                  pl.BlockSpec((1, LANES), lambda c: (0, 0)),
                  pl.BlockSpec((1, d), lambda c: (0, 0))],
        out_specs=pl.BlockSpec((L, d), lambda c: (c, 0)),
        out_shape=jax.ShapeDtypeStruct((s, d), BF16),
        scratch_shapes=[pltpu.VMEM((H, dk, dv), F32), pltpu.VMEM((H, 1, dk), F32), pltpu.VMEM((H, 1, 1), F32)],
        compiler_params=_params("arbitrary"),
        name="mlstm_cells",
    )(proj, gates, b_gates_pad, norm_gain.reshape(1, d))


def mlstm_mixer(x, gain, shift, scale, w_in, b_gates, norm_gain, w_out, gate_row):
    d = x.shape[1]
    n_main = w_in.shape[1] - 2 * MLSTM_HEADS
    hm = norm_mod(x, gain, shift, scale, BF16)
    proj = matmul(hm, w_in[:, :n_main])
    w_g = jnp.pad(w_in[:, n_main:], ((0, 0), (0, LANES - 2 * MLSTM_HEADS)))
    gates = matmul(hm, w_g)
    b_pad = jnp.pad(b_gates, (0, LANES - 2 * MLSTM_HEADS)).reshape(1, LANES)
    cells = mlstm_cells(proj, gates, b_pad, norm_gain)
    return matmul(cells, w_out, res=x, gate=gate_row)


def _rwkv_prep_body(x_ref, xp_ref, g_ref, sh_ref, sc_ref, mu_ref, *outs):
    tm = x_ref.shape[0]
    hm = _norm_rows(x_ref[...], g_ref[...], sh_ref[...], sc_ref[...], NORM_EPS)
    prev = _norm_rows(xp_ref[7:8, :], g_ref[...], sh_ref[...], sc_ref[...], NORM_EPS)
    prev = jnp.where(pl.program_id(0) == 0, 0.0, prev)
    rows = lax.broadcasted_iota(jnp.int32, hm.shape, 0)
    shifted = jnp.where(rows == 0, prev, pltpu.roll(hm, 1, axis=0))
    xx = shifted - hm
    for j, o_ref in enumerate(outs):
        o_ref[...] = (hm + xx * mu_ref[j:j + 1, :]).astype(o_ref.dtype)


def rwkv_prep(x, gain, shift, scale, mu, tm=256):
    s, d = x.shape
    row = pl.BlockSpec((1, d), lambda i: (0, 0))
    blk = pl.BlockSpec((tm, d), lambda i: (i, 0))
    return pl.pallas_call(
        _rwkv_prep_body,
        grid=(s // tm,),
        in_specs=[blk, pl.BlockSpec((8, d), lambda i: (jnp.maximum(i * (tm // 8) - 1, 0), 0)),
                  row, row, row, pl.BlockSpec((8, d), lambda i: (0, 0))],
        out_specs=[blk] * 6,
        out_shape=[jax.ShapeDtypeStruct((s, d), BF16)] * 6,
        compiler_params=_params("arbitrary"),
        name="rwkv_prep",
    )(x, x, gain.reshape(1, d), shift.reshape(1, d), scale.reshape(1, d), jnp.pad(mu, ((0, 2), (0, 0))))


def _lora_body(x_ref, w1_ref, w2_ref, o_ref, *, act):
    t = _bdot(x_ref[...], w1_ref[...])
    if act == "tanh":
        t = jnp.tanh(t)
    elif act == "sigmoid":
        t = jax.nn.sigmoid(t)
    o_ref[...] = _bdot(t, w2_ref[...])


def lora(x, w1, w2, act, tm=512):
    s, d = x.shape
    r = w1.shape[1]
    rp = -(-r // LANES) * LANES
    w1p = jnp.pad(w1, ((0, 0), (0, rp - r)))
    w2p = jnp.pad(w2, ((0, rp - r), (0, 0)))
    n = w2.shape[1]
    return pl.pallas_call(
        functools.partial(_lora_body, act=act),
        grid=(s // tm,),
        in_specs=[pl.BlockSpec((tm, d), lambda i: (i, 0)),
                  pl.BlockSpec((d, rp), lambda i: (0, 0)),
                  pl.BlockSpec((rp, n), lambda i: (0, 0))],
        out_specs=pl.BlockSpec((tm, n), lambda i: (i, 0)),
        out_shape=jax.ShapeDtypeStruct((s, n), F32),
        compiler_params=_params("arbitrary"),
        name="lora",
    )(x, w1p, w2p)


def _split3(x):
    hi = x.astype(BF16)
    r1 = x - hi.astype(F32)
    mid = r1.astype(BF16)
    lo = (r1 - mid.astype(F32)).astype(BF16)
    return hi, mid, lo


def _segsum(x, bmat, passes):
    parts = _split3(x)[:passes] if passes > 1 else (x.astype(BF16),)
    acc = jnp.dot(parts[0], bmat, preferred_element_type=F32)
    for p in parts[1:]:
        acc = acc + jnp.dot(p, bmat, preferred_element_type=F32)
    return acc


SUBLANES = 8


SCAN_W = 256


def _head_ones(width):
    ri = lax.broadcasted_iota(jnp.int32, (width, width), 0)
    ci = lax.broadcasted_iota(jnp.int32, (width, width), 1)
    return ((ri // RWKV_HEAD) == (ci // RWKV_HEAD)).astype(BF16)


def _rwkv_pre_body(k_ref, wl_ref, al_ref, w0_ref, a0_ref, kk_ref, ka_ref, nkk_ref, w_ref, kka_ref, kp_ref):
    d = k_ref.shape[1]
    bmat = _head_ones(SCAN_W)
    for j in range(d // SCAN_W):
        cs = slice(j * SCAN_W, (j + 1) * SCAN_W)
        kraw = k_ref[:, cs]
        w_log = -jax.nn.softplus(-(w0_ref[:, cs] + wl_ref[:, cs])) - 0.5
        w_ref[:, cs] = -jnp.exp(w_log)
        a = jax.nn.sigmoid(a0_ref[:, cs] + al_ref[:, cs])
        kk = kraw * kk_ref[:, cs]
        nrm = jnp.sqrt(_segsum(kk * kk, bmat, 3))
        kk = kk / jnp.maximum(nrm, 1e-12)
        nkk_ref[:, cs] = -kk
        kka_ref[:, cs] = kk * a
        kp_ref[:, cs] = kraw * (1.0 + (a - 1.0) * ka_ref[:, cs])


def rwkv_pre(k, wl, al, w0, a0, k_k, k_a, tm=256):
    s, d = k.shape
    blk = pl.BlockSpec((tm, d), lambda i: (i, 0))
    row = pl.BlockSpec((1, d), lambda i: (0, 0))
    return pl.pallas_call(
        _rwkv_pre_body,
        grid=(s // tm,),
        in_specs=[blk] * 3 + [row] * 4,
        out_specs=[blk] * 4,
        out_shape=[jax.ShapeDtypeStruct((s, d), F32)] * 4,
        compiler_params=_params("arbitrary"),
        name="rwkv_pre",
    )(k, wl, al, *[t.reshape(1, d) for t in (w0, a0, k_k, k_a)])


def _rwkv_post_body(y_ref, r_ref, kp_ref, v_ref, gt_ref, rk_ref, lg_ref, lb_ref, o_ref):
    d = y_ref.shape[1]
    bmat = _head_ones(SCAN_W)
    inv = 1.0 / RWKV_HEAD
    for j in range(d // SCAN_W):
        cs = slice(j * SCAN_W, (j + 1) * SCAN_W)
        y = y_ref[:, cs]
        mean = _segsum(y, bmat, 3) * inv
        yc = y - mean
        var = _segsum(yc * yc, bmat, 3) * inv
        yn = yc * lax.rsqrt(var + RWKV_GN_EPS) * lg_ref[:, cs] + lb_ref[:, cs]
        bonus = _segsum(r_ref[:, cs] * kp_ref[:, cs] * rk_ref[:, cs], bmat, 3) * v_ref[:, cs]
        o_ref[:, cs] = ((yn + bonus) * gt_ref[:, cs]).astype(o_ref.dtype)


def rwkv_post(y, r, kp, v, gt, r_k, ln_gain, ln_bias, tm=256):
    s, d = y.shape
    blk = pl.BlockSpec((tm, d), lambda i: (i, 0))
    row = pl.BlockSpec((1, d), lambda i: (0, 0))
    return pl.pallas_call(
        _rwkv_post_body,
        grid=(s // tm,),
        in_specs=[blk] * 5 + [row] * 3,
        out_specs=blk,
        out_shape=jax.ShapeDtypeStruct((s, d), BF16),
        compiler_params=_params("arbitrary"),
        name="rwkv_post",
    )(y, r, kp, v, gt, *[t.reshape(1, d) for t in (r_k, ln_gain, ln_bias)])


RWKV_CHUNK = 32


def _rwkv_chunk_body(a_ref, lw_ref, b_ref, kp_ref, r_ref, v_ref, y_ref, st_s):
    L, d = r_ref.shape
    nv = RWKV_HEAD
    W = SCAN_W
    hpt = W // nv
    ngrp = L // SUBLANES

    @pl.when(pl.program_id(0) == 0)
    def _():
        st_s[...] = jnp.zeros_like(st_s)

    lane_head = lax.broadcasted_iota(jnp.int32, (1, W), 1) // nv
    heads = [lane_head == h for h in range(hpt)]
    ti = lax.broadcasted_iota(jnp.int32, (L, L), 0)
    si = lax.broadcasted_iota(jnp.int32, (L, L), 1)
    tril = (ti >= si).astype(F32)
    t_row = lax.broadcasted_iota(jnp.int32, (2 * L, 2 * hpt * L), 0)
    s_lane = lax.broadcasted_iota(jnp.int32, (2 * L, 2 * hpt * L), 1) % L
    g_mask = s_lane <= jnp.where(t_row < L, t_row - 1, t_row - L)
    sub = lax.broadcasted_iota(jnp.int32, (SUBLANES, W), 0)
    ones = _head_ones(W)

    def stack(x):
        return jnp.concatenate([jnp.where(hm, x, 0.0) for hm in heads], axis=0)

    for j in range(d // W):
        cs = slice(j * W, (j + 1) * W)
        rs = slice(j * nv, (j + 1) * nv)
        lw = lw_ref[:, cs]
        cum = jnp.dot(tril, lw, precision=HIGHEST, preferred_element_type=F32)
        tot = cum[L - 1:L, :]
        w_inv = jnp.exp(-cum)
        w_rest = jnp.exp(tot - cum)
        a_hat = a_ref[:, cs] * jnp.exp(cum - lw)
        r_hat = r_ref[:, cs] * jnp.exp(cum)
        k_hat = kp_ref[:, cs] * w_inv
        b_hat = b_ref[:, cs] * w_inv
        v = v_ref[:, cs]
        s0 = st_s[rs, :]

        ar = jnp.concatenate([a_hat, r_hat], axis=0).astype(BF16)
        uy0 = _dot_nt(ar, stack(s0))
        g = _dot_nt(ar, jnp.concatenate([stack(k_hat), stack(b_hat)], axis=0))
        g = jnp.where(g_mask, g, 0.0)
        v_st = stack(v)
        base = uy0[0:L] + _bdot(g[0:L, 0:hpt * L], v_st)

        need = [(s, g8) for s in range(L - 1) for g8 in range(s // SUBLANES, ngrp)]
        parts = [a_hat[g8 * SUBLANES:(g8 + 1) * SUBLANES] * b_hat[s:s + 1, :] for s, g8 in need]
        if len(parts) % 2:
            parts.append(jnp.zeros((SUBLANES, W), F32))
        lhs = jnp.concatenate(parts, axis=0)
        coef = jnp.dot(lhs.astype(BF16), ones, preferred_element_type=F32)
        at = {sg: n * SUBLANES for n, sg in enumerate(need)}
        u = [base[g8 * SUBLANES:(g8 + 1) * SUBLANES] for g8 in range(ngrp)]
        for s in range(L - 1):
            g0, i0 = divmod(s, SUBLANES)
            us = jnp.broadcast_to(u[g0][i0:i0 + 1, :], (SUBLANES, W))
            for g8 in range(g0, ngrp):
                c = coef[at[s, g8]:at[s, g8] + SUBLANES]
                if g8 == g0:
                    c = jnp.where(sub > i0, c, 0.0)
                u[g8] = u[g8] + c * us
        u = jnp.concatenate(u, axis=0)

        vu = jnp.concatenate([v_st, stack(u)], axis=0)
        y_ref[:, cs] = uy0[L:2 * L] + _bdot(g[L:2 * L], vu)

        kb_rest = jnp.concatenate([kp_ref[:, cs] * w_rest, b_ref[:, cs] * w_rest], axis=0)
        upd = _dot_tn(jnp.concatenate([v, u], axis=0), kb_rest)
        new = s0 * jnp.exp(tot)
        for h, hm in enumerate(heads):
            new = new + jnp.where(hm, upd[h * nv:(h + 1) * nv], 0.0)
        st_s[rs, :] = new


def rwkv_chunked(nkk, lw, kka, kp, r, v):
    s, d = r.shape
    L = RWKV_CHUNK
    blk = pl.BlockSpec((L, d), lambda i: (i, 0))
    return pl.pallas_call(
        _rwkv_chunk_body,
        grid=(s // L,),
        in_specs=[blk] * 6,
        out_specs=blk,
        out_shape=jax.ShapeDtypeStruct((s, d), F32),
        scratch_shapes=[pltpu.VMEM(((d // SCAN_W) * RWKV_HEAD, SCAN_W), F32)],
        compiler_params=_params("arbitrary"),
        name="rwkv_chunked",
    )(nkk, lw, kka, kp, r, v)


def rwkv_mixer(x, gain, shift, scale, mu, w_rkv, w0, w1, w2, a0, a1, a2, g1, g2, k_k, k_a, r_k,
               ln_gain, ln_bias, w_o, gate_row):
    xr, xk, xv, xw, xa, xg = rwkv_prep(x, gain, shift, scale, mu)
    r = matmul(xr, w_rkv[0])
    k = matmul(xk, w_rkv[1])
    v = matmul(xv, w_rkv[2])
    wl = lora(xw, w1, w2, "tanh")
    al = lora(xa, a1, a2, "none")
    gt = lora(xg, g1, g2, "sigmoid")
    nkk, lw, kka, kp = rwkv_pre(k, wl, al, w0, a0, k_k, k_a)
    y = rwkv_chunked(nkk, lw, kka, kp, r, v)
    y = rwkv_post(y, r, kp, v, gt, r_k.reshape(-1), ln_gain, ln_bias)
    return matmul(y, w_o, res=x, gate=gate_row)


def _rope_body(p_ref, pos_ref, o_ref, *, n_rot_tiles, q_tiles):
    tm, n = p_ref.shape
    half = ROPE_DIM // 2
    lane = lax.broadcasted_iota(jnp.int32, (tm, LANES), 1)
    inv_freq = jnp.exp((lane % half).astype(F32) * (-2.0 * math.log(ROPE_THETA) / ROPE_DIM))
    ang = pos_ref[...].astype(F32) * inv_freq
    roped = lane < ROPE_DIM
    cos = jnp.where(roped, jnp.cos(ang), 1.0)
    sin = jnp.where(roped, jnp.where(lane < half, -jnp.sin(ang), jnp.sin(ang)), 0.0)
    for j in range(n // LANES):
        x = p_ref[:, j * LANES:(j + 1) * LANES]
        if j < n_rot_tiles:
            partner = jnp.where(lane < half, pltpu.roll(x, LANES - half, axis=1), pltpu.roll(x, half, axis=1))
            x = x * cos + partner * sin
            if j < q_tiles:
                x = x * (DIFF_DQK ** -0.5)
        o_ref[:, j * LANES:(j + 1) * LANES] = x.astype(o_ref.dtype)


def rope_cast(proj, positions, tm=256):
    s, n = proj.shape
    q_tiles = 2 * DIFF_HEADS
    return pl.pallas_call(
        functools.partial(_rope_body, n_rot_tiles=2 * q_tiles, q_tiles=q_tiles),
        grid=(s // tm,),
        in_specs=[pl.BlockSpec((tm, n), lambda i: (i, 0)),
                  pl.BlockSpec((tm, 1), lambda i: (i, 0))],
        out_specs=pl.BlockSpec((tm, n), lambda i: (i, 0)),
        out_shape=jax.ShapeDtypeStruct((s, n), BF16),
        compiler_params=_params("arbitrary"),
        name="rope_cast",
    )(proj, positions.reshape(s, 1))


def _diff_attn_body(qi_ref, kj_ref, q_ref, k_ref, v_ref, lam_ref, sg_ref, o_ref, m_s, l_s, acc_s, *, lambda_init):
    step = pl.program_id(1)
    qi, kj = qi_ref[step], kj_ref[step]
    tb = q_ref.shape[0]

    @pl.when(kj == 0)
    def _():
        m_s[...] = jnp.full_like(m_s, -jnp.inf)
        l_s[...] = jnp.zeros_like(l_s)
        acc_s[...] = jnp.zeros_like(acc_s)

    def accumulate(diagonal):
        v = v_ref[...]
        for c in range(2):
            cs = slice(c * DIFF_DQK, (c + 1) * DIFF_DQK)
            s = _dot_nt(q_ref[:, cs], k_ref[:, cs])
            if diagonal:
                rows = lax.broadcasted_iota(jnp.int32, (tb, tb), 0)
                cols = lax.broadcasted_iota(jnp.int32, (tb, tb), 1)
                s = jnp.where(cols <= rows, s, -jnp.inf)
            m_old = m_s[c]
            m_new = jnp.maximum(m_old, jnp.max(s, axis=-1, keepdims=True))
            alpha = jnp.exp(m_old - m_new)
            p = jnp.exp(s - m_new)
            l_s[c] = alpha * l_s[c] + jnp.sum(p, axis=-1, keepdims=True)
            acc_s[c] = alpha * acc_s[c] + _bdot(p, v)
            m_s[c] = m_new

    @pl.when(kj < qi)
    def _():
        accumulate(False)

    @pl.when(kj == qi)
    def _():
        accumulate(True)
        lp = lam_ref[...]
        lam = (jnp.exp(jnp.sum(lp[0:1] * lp[1:2], axis=-1, keepdims=True))
               - jnp.exp(jnp.sum(lp[2:3] * lp[3:4], axis=-1, keepdims=True)) + lambda_init)
        out = acc_s[0] / l_s[0] - lam * (acc_s[1] / l_s[1])
        out = out * lax.rsqrt(jnp.mean(out * out, axis=-1, keepdims=True) + DIFF_SUBLN_EPS)
        o_ref[...] = (out * sg_ref[...] * (1.0 - lambda_init)).astype(o_ref.dtype)


def diff_attention(qkv, lam_params, subln_gain, lambda_init, tb=1024):
    s = qkv.shape[0]
    H, dv = DIFF_HEADS, DIFF_DV
    tb = min(tb, s)
    nb = s // tb
    pairs = [(qi, kj) for qi in range(nb) for kj in range(qi + 1)]
    qi_tab = jnp.asarray([p[0] for p in pairs], jnp.int32)
    kj_tab = jnp.asarray([p[1] for p in pairs], jnp.int32)
    grid_spec = pltpu.PrefetchScalarGridSpec(
        num_scalar_prefetch=2,
        grid=(H, len(pairs)),
        in_specs=[pl.BlockSpec((tb, dv), lambda h, t, qi, kj: (qi[t], h)),
                  pl.BlockSpec((tb, dv), lambda h, t, qi, kj: (kj[t], H + h)),
                  pl.BlockSpec((tb, dv), lambda h, t, qi, kj: (kj[t], 2 * H + h)),
                  pl.BlockSpec((4, DIFF_DQK), lambda h, t, qi, kj: (0, 0)),
                  pl.BlockSpec((1, dv), lambda h, t, qi, kj: (0, 0))],
        out_specs=pl.BlockSpec((tb, dv), lambda h, t, qi, kj: (qi[t], h)),
        scratch_shapes=[pltpu.VMEM((2, tb, 1), F32), pltpu.VMEM((2, tb, 1), F32),
                        pltpu.VMEM((2, tb, dv), F32)],
    )
    return pl.pallas_call(
        functools.partial(_diff_attn_body, lambda_init=lambda_init),
        grid_spec=grid_spec,
        out_shape=jax.ShapeDtypeStruct((s, H * dv), BF16),
        compiler_params=_params("arbitrary", "arbitrary"),
        name="diff_attention",
    )(qi_tab, kj_tab, qkv, qkv, qkv, lam_params, subln_gain.reshape(1, dv))


def diff_mixer(x, gain, shift, scale, positions, w_qkv, lam_params, subln_gain, w_o, layer, gate_row):
    hm = norm_mod(x, gain, shift, scale, BF16)
    proj = matmul(hm, w_qkv)
    qkv = rope_cast(proj, positions)
    lambda_init = 0.8 - 0.6 * math.exp(-0.3 * layer)
    att = diff_attention(qkv, lam_params, subln_gain, lambda_init)
    return matmul(att, w_o, res=x, gate=gate_row)


HI16 = 0xFFFF0000


def _pack_rows(x):
    m = x.shape[1] // 2
    bits = lax.bitcast_convert_type(x.astype(BF16).astype(F32), jnp.uint32)
    return (bits[:, :m] >> 16) | (bits[:, m:] & jnp.uint32(HI16))


def _unpack_rows(p):
    lo = lax.bitcast_convert_type(p << 16, F32)
    hi = lax.bitcast_convert_type(p & jnp.uint32(HI16), F32)
    return lo, hi


def _router_body(x_ref, g_ref, sh_ref, sc_ref, rw_ref, rb_ref, hf_ref, lg_ref):
    hf = _norm_rows(x_ref[...], g_ref[...], sh_ref[...], sc_ref[...], NORM_EPS)
    hf_ref[...] = _pack_rows(hf)
    lg_ref[...] = jnp.dot(hf, rw_ref[...], precision=HIGHEST, preferred_element_type=F32) + rb_ref[...]


def router(x, gain, shift, scale, router_w, router_b, tm=256):
    s, d = x.shape
    e = router_w.shape[1]
    rw = jnp.pad(router_w, ((0, 0), (0, LANES - e)))
    rb = jnp.pad(router_b, (0, LANES - e), constant_values=-1e30).reshape(1, LANES)
    row = pl.BlockSpec((1, d), lambda i: (0, 0))
    return pl.pallas_call(
        _router_body,
        grid=(s // tm,),
        in_specs=[pl.BlockSpec((tm, d), lambda i: (i, 0)), row, row, row,
                  pl.BlockSpec((d, LANES), lambda i: (0, 0)), pl.BlockSpec((1, LANES), lambda i: (0, 0))],
        out_specs=[pl.BlockSpec((tm, d // 2), lambda i: (i, 0)), pl.BlockSpec((tm, LANES), lambda i: (i, 0))],
        out_shape=[jax.ShapeDtypeStruct((s, d // 2), jnp.uint32), jax.ShapeDtypeStruct((s, LANES), F32)],
        compiler_params=_params("arbitrary"),
        name="router",
    )(x, gain.reshape(1, d), shift.reshape(1, d), scale.reshape(1, d), rw, rb)


def _topk_body(lg_ref, eidx_ref, gw_ref, rk_ref, cnt_ref):
    tm = lg_ref.shape[0]

    @pl.when(pl.program_id(0) == 0)
    def _():
        cnt_ref[...] = jnp.zeros_like(cnt_ref)

    lane = lax.broadcasted_iota(jnp.int32, (tm, LANES), 1).astype(F32)
    work = lg_ref[...]
    vals, idxs, hots = [], [], []
    for _ in range(TOP_K):
        m = jnp.max(work, axis=-1, keepdims=True)
        idx = jnp.min(jnp.where(work == m, lane, float(LANES)), axis=-1, keepdims=True)
        hot = lane == idx
        work = jnp.where(hot, -jnp.inf, work)
        vals.append(m)
        idxs.append(idx)
        hots.append(hot)
    exps = [jnp.exp(v - vals[0]) for v in vals]
    den = exps[0] + exps[1] + exps[2] + exps[3]
    sel = jnp.zeros((tm, LANES), F32)
    for hot in hots:
        sel = sel + hot.astype(F32)
    ri = lax.broadcasted_iota(jnp.int32, (tm, tm), 0)
    ci = lax.broadcasted_iota(jnp.int32, (tm, tm), 1)
    before = jnp.dot((ri > ci).astype(BF16), sel.astype(BF16), preferred_element_type=F32)
    rank = before + cnt_ref[...]
    cnt_ref[...] = cnt_ref[...] + jnp.sum(sel, axis=0, keepdims=True)
    eidx = jnp.zeros((tm, LANES), F32)
    gw = jnp.zeros((tm, LANES), F32)
    rk = jnp.zeros((tm, LANES), F32)
    for kslot in range(TOP_K):
        here = lane == kslot
        eidx = jnp.where(here, idxs[kslot], eidx)
        gw = jnp.where(here, exps[kslot] / den, gw)
        rk = jnp.where(here, jnp.sum(jnp.where(hots[kslot], rank, 0.0), axis=-1, keepdims=True), rk)
    eidx_ref[...] = eidx
    gw_ref[...] = gw
    rk_ref[...] = rk


def topk_route(logits, tm=TOK_BLOCK):
    s = logits.shape[0]
    blk = pl.BlockSpec((tm, LANES), lambda i: (i, 0))
    return pl.pallas_call(
        _topk_body,
        grid=(s // tm,),
        in_specs=[blk],
        out_specs=[blk, blk, blk, pl.BlockSpec((1, LANES), lambda i: (0, 0))],
        out_shape=[jax.ShapeDtypeStruct((s, LANES), F32), jax.ShapeDtypeStruct((s, LANES), F32),
                   jax.ShapeDtypeStruct((s, LANES), F32), jax.ShapeDtypeStruct((1, LANES), F32)],
        compiler_params=_params("arbitrary"),
        name="topk_route",
    )(logits)


def _slot_body(eidx_ref, rk_ref, ps_ref, pos_ref):
    tm = eidx_ref.shape[0]
    lane = lax.broadcasted_iota(jnp.int32, (tm, LANES), 1).astype(F32)
    eidx = eidx_ref[...]
    rk = rk_ref[...]
    pos = jnp.zeros((tm, LANES), F32)
    for kslot in range(TOP_K):
        e_k = jnp.sum(jnp.where(lane == kslot, eidx, 0.0), axis=-1, keepdims=True)
        start = jnp.sum(jnp.where(lane == e_k, ps_ref[...], 0.0), axis=-1, keepdims=True)
        r_k = jnp.sum(jnp.where(lane == kslot, rk, 0.0), axis=-1, keepdims=True)
        pos = jnp.where(lane == kslot, start + r_k, pos)
    pos_ref[...] = pos.astype(jnp.int32)


def slot_positions(eidx, rk, padded_start, tm=TOK_BLOCK):
    s = eidx.shape[0]
    blk = pl.BlockSpec((tm, LANES), lambda i: (i, 0))
    return pl.pallas_call(
        _slot_body,
        grid=(s // tm,),
        in_specs=[blk, blk, pl.BlockSpec((1, LANES), lambda i: (0, 0))],
        out_specs=blk,
        out_shape=jax.ShapeDtypeStruct((s, LANES), jnp.int32),
        compiler_params=_params("arbitrary"),
        name="slot_positions",
    )(eidx, rk, padded_start)


def _dispatch_body(pos_ref, hf_ref, xs_in_ref, xs_ref, sem):
    del xs_in_ref
    tm = hf_ref.shape[0]

    def issue(t, carry):
        for kslot in range(TOP_K):
            p = pos_ref[t * TOP_K + kslot]
            pltpu.make_async_copy(hf_ref.at[pl.ds(t, 1)], xs_ref.at[pl.ds(p, 1)], sem.at[0]).start(
                priority=kslot % 2)
        return carry

    lax.fori_loop(0, tm, issue, 0, unroll=4)

    for _ in range(TOP_K):
        pltpu.make_async_copy(hf_ref, xs_ref.at[pl.ds(0, tm)], sem.at[0]).wait()


def dispatch(pos_flat, hf, n_rows, tm=TOK_BLOCK):
    s, d = hf.shape
    return pl.pallas_call(
        _dispatch_body,
        grid=(s // tm,),
        in_specs=[pl.BlockSpec((tm * TOP_K,), lambda i: (i,), memory_space=pltpu.SMEM),
                  pl.BlockSpec((tm, d), lambda i: (i, 0)),
                  pl.BlockSpec(memory_space=pl.ANY)],
        out_specs=pl.BlockSpec(memory_space=pl.ANY),
        out_shape=jax.ShapeDtypeStruct((n_rows, d), hf.dtype),
        scratch_shapes=[pltpu.SemaphoreType.DMA((1,))],
        input_output_aliases={2: 0},
        compiler_params=_params("arbitrary"),
        name="moe_dispatch",
    )(pos_flat, hf, jnp.zeros((n_rows, d), hf.dtype))


def _expert_body(be_ref, first_ref, nxt_ref, tot_ref, xs_ref, wgu_hbm, bgu_ref, wdn_hbm, bdn_ref, perm_ref, ys_ref,
                 stg_gu, stg_dn, wgu_s, wdn_s, sem, *, layer, chunk):
    i = pl.program_id(0)
    rows = xs_ref.shape[0]
    d, n_gu = wgu_s.shape
    dh = wdn_s.shape[0]
    grp = perm_ref.shape[0]
    half = grp // 2

    def fetch(e):
        return (pltpu.make_async_copy(wgu_hbm.at[layer, e], stg_gu, sem.at[0]),
                pltpu.make_async_copy(wdn_hbm.at[layer, e], stg_dn, sem.at[1]))

    @pl.when(first_ref[i] == 1)
    def _():
        @pl.when(i == 0)
        def _():
            for cp in fetch(be_ref[0]):
                cp.start()

        for cp in fetch(be_ref[i]):
            cp.wait()
        for c in range(d // chunk):
            wgu_s[c * chunk:(c + 1) * chunk, :] = stg_gu[c * chunk:(c + 1) * chunk, :].astype(BF16)
        for c in range(dh // chunk):
            wdn_s[c * chunk:(c + 1) * chunk, :] = stg_dn[c * chunk:(c + 1) * chunk, :].astype(BF16)

        @pl.when(nxt_ref[i] >= 0)
        def _():
            for cp in fetch(nxt_ref[i]):
                cp.start()

    @pl.when(i * rows < tot_ref[0])
    def _():
        x_lo, x_hi = _unpack_rows(xs_ref[...])
        gu = _bdot(x_lo, wgu_s[0:d // 2, :]) + _bdot(x_hi, wgu_s[d // 2:d, :]) + bgu_ref[0]
        gub = gu.astype(BF16)
        gs, us = [], []
        for j in range(n_gu // grp):
            pj = jnp.dot(gub[:, j * grp:(j + 1) * grp], perm_ref[...], preferred_element_type=F32)
            gs.append(pj[:, :half])
            us.append(pj[:, half:])
        g = jnp.minimum(jnp.concatenate(gs, axis=1), SWIGLU_LIMIT)
        u = jnp.clip(jnp.concatenate(us, axis=1), -SWIGLU_LIMIT, SWIGLU_LIMIT)
        act = (u + 1.0) * (g * jax.nn.sigmoid(g * SWIGLU_ALPHA))
        ys_ref[...] = _pack_rows(_bdot(act, wdn_s[...]) + bdn_ref[0])

    @pl.when(i * rows >= tot_ref[0])
    def _():
        ys_ref[...] = jnp.zeros_like(ys_ref)


def expert_ffn(blk_expert, blk_first, blk_next, total_rows, xs, w_gu, b_gu, w_dn, b_dn, layer, rows=MOE_ROWS):
    n_rows, dp = xs.shape
    _, e, d, n_gu = w_gu.shape
    dh = w_dn.shape[2]
    grp = 2 * LANES
    perm = np.zeros((grp, grp), np.float32)
    perm[2 * np.arange(LANES), np.arange(LANES)] = 1.0
    perm[2 * np.arange(LANES) + 1, LANES + np.arange(LANES)] = 1.0
    grid_spec = pltpu.PrefetchScalarGridSpec(
        num_scalar_prefetch=4,
        grid=(n_rows // rows,),
        in_specs=[pl.BlockSpec((rows, dp), lambda i, be, fi, nx, tot: (i, 0)),
                  pl.BlockSpec(memory_space=pl.ANY),
                  pl.BlockSpec((1, 1, n_gu), lambda i, be, fi, nx, tot: (be[i], 0, 0)),
                  pl.BlockSpec(memory_space=pl.ANY),
                  pl.BlockSpec((1, 1, d), lambda i, be, fi, nx, tot: (be[i], 0, 0)),
                  pl.BlockSpec((grp, grp), lambda i, be, fi, nx, tot: (0, 0))],
        out_specs=pl.BlockSpec((rows, dp), lambda i, be, fi, nx, tot: (i, 0)),
        scratch_shapes=[pltpu.VMEM((d, n_gu), F32), pltpu.VMEM((dh, d), F32),
                        pltpu.VMEM((d, n_gu), BF16), pltpu.VMEM((dh, d), BF16),
                        pltpu.SemaphoreType.DMA((2,))],
    )
    return pl.pallas_call(
        functools.partial(_expert_body, layer=layer, chunk=256),
        grid_spec=grid_spec,
        out_shape=jax.ShapeDtypeStruct((n_rows, dp), jnp.uint32),
        compiler_params=_params("arbitrary"),
        name="moe_experts",
    )(blk_expert, blk_first, blk_next, total_rows, xs, w_gu, b_gu.reshape(e, 1, n_gu), w_dn,
      b_dn.reshape(e, 1, d), jnp.asarray(perm, BF16))


def _combine_body(pos_ref, x_ref, gw_ref, g2_ref, ys_ref, o_ref, buf, sem):
    tm = x_ref.shape[0]

    def issue(t, carry):
        for kslot in range(TOP_K):
            p = pos_ref[t * TOP_K + kslot]
            pltpu.make_async_copy(ys_ref.at[pl.ds(p, 1)], buf.at[kslot, pl.ds(t, 1)], sem.at[0]).start(
                priority=kslot % 2)
        return carry

    lax.fori_loop(0, tm, issue, 0, unroll=4)

    for kslot in range(TOP_K):
        pltpu.make_async_copy(ys_ref.at[pl.ds(0, tm)], buf.at[kslot], sem.at[0]).wait()

    gw = gw_ref[...]
    half = buf.shape[2]
    acc_lo = jnp.zeros((tm, half), F32)
    acc_hi = jnp.zeros((tm, half), F32)
    for kslot in range(TOP_K):
        lo, hi = _unpack_rows(buf[kslot])
        acc_lo = acc_lo + gw[:, kslot:kslot + 1] * lo
        acc_hi = acc_hi + gw[:, kslot:kslot + 1] * hi
    o_ref[:, 0:half] = x_ref[:, 0:half] + g2_ref[:, 0:half] * acc_lo
    o_ref[:, half:2 * half] = x_ref[:, half:2 * half] + g2_ref[:, half:2 * half] * acc_hi


def combine(pos_flat, x, gw, g2_row, ys, tm=TOK_BLOCK):
    s, d = x.shape
    return pl.pallas_call(
        _combine_body,
        grid=(s // tm,),
        in_specs=[pl.BlockSpec((tm * TOP_K,), lambda i: (i,), memory_space=pltpu.SMEM),
                  pl.BlockSpec((tm, d), lambda i: (i, 0)),
                  pl.BlockSpec((tm, LANES), lambda i: (i, 0)),
                  pl.BlockSpec((1, d), lambda i: (0, 0)),
                  pl.BlockSpec(memory_space=pl.ANY)],
        out_specs=pl.BlockSpec((tm, d), lambda i: (i, 0)),
        out_shape=jax.ShapeDtypeStruct((s, d), F32),
        scratch_shapes=[pltpu.VMEM((TOP_K, tm, d // 2), jnp.uint32), pltpu.SemaphoreType.DMA((1,))],
        compiler_params=_params("arbitrary"),
        name="moe_combine",
    )(pos_flat, x, gw, g2_row.reshape(1, d), ys)


def moe_layer(x, gain, shift, scale, g2_row, router_w, router_b, w_gu, b_gu, w_dn, b_dn, layer):
    s, d = x.shape
    e = router_w.shape[1]
    hf, logits = router(x, gain, shift, scale, router_w, router_b)
    eidx, gw, rk, counts = topk_route(logits)
    cnt = counts[0, :e].astype(jnp.int32)
    padded = (cnt + MOE_ROWS - 1) // MOE_ROWS * MOE_ROWS
    padded_end = jnp.cumsum(padded)
    padded_start = jnp.pad((padded_end - padded).astype(F32), (0, LANES - e)).reshape(1, LANES)
    n_rows = s * TOP_K + e * MOE_ROWS
    blk_first_row = jnp.arange(n_rows // MOE_ROWS, dtype=jnp.int32) * MOE_ROWS
    blk_expert = jnp.minimum(
        jnp.sum((padded_end[None, :] <= blk_first_row[:, None]).astype(jnp.int32), axis=1), e - 1)
    total_rows = padded_end[-1:].astype(jnp.int32)
    n_blk = n_rows // MOE_ROWS
    blk_id = jnp.arange(n_blk, dtype=jnp.int32)
    prev_expert = jnp.concatenate([jnp.full((1,), -1, jnp.int32), blk_expert[:-1]])
    blk_first = ((blk_first_row < total_rows[0]) & (blk_expert != prev_expert)).astype(jnp.int32)
    first_at = jnp.where(blk_first == 1, blk_id, n_blk)
    next_first = jnp.concatenate([lax.cummin(first_at, reverse=True)[1:], jnp.full((1,), n_blk, jnp.int32)])
    blk_next = jnp.where(next_first < n_blk, blk_expert[jnp.minimum(next_first, n_blk - 1)], -1).astype(jnp.int32)
    pos = slot_positions(eidx, rk, padded_start)
    pos_flat = pos[:, :TOP_K].reshape(-1)
    xs = dispatch(pos_flat, hf, n_rows)
    ys = expert_ffn(blk_expert, blk_first, blk_next, total_rows, xs, w_gu, b_gu, w_dn, b_dn, layer)
    return combine(pos_flat, x, gw, g2_row, ys)


def kernel(x, c, positions, norm_gain, ada_w, ada_b, mlstm_w_in, mlstm_b_gates, mlstm_norm_gain, mlstm_w_out, rwkv_mu, rwkv_w_rkv, rwkv_w0, rwkv_w1, rwkv_w2, rwkv_a0, rwkv_a1, rwkv_a2, rwkv_g1, rwkv_g2, rwkv_k_k, rwkv_k_a, rwkv_r_k, rwkv_ln_gain, rwkv_ln_bias, rwkv_w_o, diff_w_qkv, diff_lambda, diff_subln_gain, diff_w_o, moe_router_w, moe_router_b, moe_w_gate_up, moe_b_gate_up, moe_w_down, moe_b_down, final_gain):
    b, s, d = x.shape
    assert b == 1
    depth = norm_gain.shape[0]
    xt = x.reshape(s, d)
    pos = positions.reshape(s)
    mod = ada_mod(c, ada_w, ada_b).reshape(depth, 6, d)
    seen = [0, 0, 0]
    for layer in range(depth):
        kind = layer % 3
        j = seen[kind]
        seen[kind] += 1
        sh1, sc1, g1, sh2, sc2, g2 = (mod[layer, i] for i in range(6))
        gain = norm_gain[layer, 0]
        if kind == 0:
            xt = mlstm_mixer(xt, gain, sh1, sc1, mlstm_w_in[j], mlstm_b_gates[j], mlstm_norm_gain[j],
                             mlstm_w_out[j], g1)
        elif kind == 1:
            xt = rwkv_mixer(xt, gain, sh1, sc1, rwkv_mu[j], rwkv_w_rkv[j], rwkv_w0[j], rwkv_w1[j], rwkv_w2[j],
                            rwkv_a0[j], rwkv_a1[j], rwkv_a2[j], rwkv_g1[j], rwkv_g2[j], rwkv_k_k[j],
                            rwkv_k_a[j], rwkv_r_k[j], rwkv_ln_gain[j], rwkv_ln_bias[j], rwkv_w_o[j], g1)
        else:
            xt = diff_mixer(xt, gain, sh1, sc1, pos, diff_w_qkv[j], diff_lambda[j], diff_subln_gain[j],
                            diff_w_o[j], layer, g1)
        xt = moe_layer(xt, norm_gain[layer, 1], sh2, sc2, g2, moe_router_w[layer], moe_router_b[layer],
                       moe_w_gate_up, moe_b_gate_up[layer], moe_w_down, moe_b_down[layer], layer)
    zeros = jnp.zeros((d,), F32)
    return norm_mod(xt, final_gain, zeros, zeros, F32).reshape(b, s, d)
```

```python
import functools
import math

import numpy as np
import jax
import jax.numpy as jnp
from jax import lax
from jax.experimental import pallas as pl
from jax.experimental.pallas import tpu as pltpu

F32 = jnp.float32
BF16 = jnp.bfloat16
HIGHEST = lax.Precision.HIGHEST

LANES = 128
VMEM_LIMIT = 52 * 1024 * 1024

NORM_EPS = 1e-6
MLSTM_HEADS = 4
MLSTM_CHUNK = 128
GATE_SOFTCAP = 15.0
RWKV_HEAD = 64
RWKV_GN_EPS = 64e-5
DIFF_HEADS = 8
DIFF_DQK = 128
DIFF_DV = 256
DIFF_SUBLN_EPS = 1e-5
ROPE_DIM = 32
ROPE_THETA = 500000.0
N_EXPERTS = 32
TOP_K = 4
SWIGLU_LIMIT = 7.0
SWIGLU_ALPHA = 1.702
MOE_ROWS = 256
TOK_BLOCK = 256


def _params(*sem):
    return pltpu.CompilerParams(dimension_semantics=sem, vmem_limit_bytes=VMEM_LIMIT)


def _bdot(a, b):
    return jnp.dot(a.astype(BF16), b.astype(BF16), preferred_element_type=F32)


def _dot_nt(a, b):
    return lax.dot_general(a.astype(BF16), b.astype(BF16), (((1,), (1,)), ((), ())),
                           preferred_element_type=F32)


def _dot_tn(a, b):
    return lax.dot_general(a.astype(BF16), b.astype(BF16), (((0,), (0,)), ((), ())),
                           preferred_element_type=F32)


def _ada_body(c_ref, w_ref, b_ref, o_ref, *, kc):
    d = w_ref.shape[1]

    def step(i, acc):
        cv = c_ref[pl.ds(i * kc, kc), :]
        sc = cv * jax.nn.sigmoid(cv)
        prod = sc * w_ref[0, pl.ds(i * kc, kc), :]
        for g in range(kc // 8):
            acc = acc + prod[g * 8:(g + 1) * 8]
        return acc

    acc = lax.fori_loop(0, d // kc, step, jnp.zeros((8, w_ref.shape[2]), F32))
    o_ref[0] = jnp.sum(acc, axis=0, keepdims=True) + b_ref[0]


def ada_mod(c, ada_w, ada_b, tn=2048, kc=64):
    depth, d, n = ada_w.shape
    return pl.pallas_call(
        functools.partial(_ada_body, kc=kc),
        grid=(depth, n // tn),
        in_specs=[pl.BlockSpec((d, 1), lambda l, j: (0, 0)),
                  pl.BlockSpec((1, d, tn), lambda l, j: (l, 0, j)),
                  pl.BlockSpec((1, 1, tn), lambda l, j: (l, 0, j))],
        out_specs=pl.BlockSpec((1, 1, tn), lambda l, j: (l, 0, j)),
        out_shape=jax.ShapeDtypeStruct((depth, 1, n), F32),
        compiler_params=_params("arbitrary", "arbitrary"),
        name="ada_mod",
    )(c.reshape(d, 1), ada_w, ada_b.reshape(depth, 1, n))


def _norm_rows(x, gain, shift, scale, eps):
    y = x * lax.rsqrt(jnp.mean(x * x, axis=-1, keepdims=True) + eps)
    return (y * gain) * (1.0 + scale) + shift


def _norm_mod_body(x_ref, g_ref, sh_ref, sc_ref, o_ref):
    o_ref[...] = _norm_rows(x_ref[...], g_ref[...], sh_ref[...], sc_ref[...], NORM_EPS).astype(o_ref.dtype)


def norm_mod(x, gain, shift, scale, out_dtype, tm=256):
    s, d = x.shape
    row = pl.BlockSpec((1, d), lambda i: (0, 0))
    return pl.pallas_call(
        _norm_mod_body,
        grid=(s // tm,),
        in_specs=[pl.BlockSpec((tm, d), lambda i: (i, 0)), row, row, row],
        out_specs=pl.BlockSpec((tm, d), lambda i: (i, 0)),
        out_shape=jax.ShapeDtypeStruct((s, d), out_dtype),
        compiler_params=_params("arbitrary"),
        name="norm_mod",
    )(x, gain.reshape(1, d), shift.reshape(1, d), scale.reshape(1, d))


def _mm_body(*refs, fused):
    if fused:
        x_ref, w_ref, res_ref, gate_ref, o_ref, wb_ref = refs
    else:
        x_ref, w_ref, o_ref, wb_ref = refs

    @pl.when(pl.program_id(1) == 0)
    def _():
        wb_ref[...] = w_ref[...].astype(BF16)

    acc = jnp.dot(x_ref[...].astype(BF16), wb_ref[...], preferred_element_type=F32)
    if fused:
        acc = res_ref[...] + gate_ref[...] * acc
    o_ref[...] = acc.astype(o_ref.dtype)


def matmul(x, w, out_dtype=F32, res=None, gate=None, tm=512, tn=1024):
    m, k = x.shape
    n = w.shape[1]
    tn = min(tn, n)
    tm = min(tm, m)
    fused = res is not None
    in_specs = [pl.BlockSpec((tm, k), lambda j, i: (i, 0)),
                pl.BlockSpec((k, tn), lambda j, i: (0, j))]
    args = [x, w]
    if fused:
        in_specs += [pl.BlockSpec((tm, tn), lambda j, i: (i, j)),
                     pl.BlockSpec((1, tn), lambda j, i: (0, j))]
        args += [res, gate.reshape(1, n)]
    return pl.pallas_call(
        functools.partial(_mm_body, fused=fused),
        grid=(n // tn, m // tm),
        in_specs=in_specs,
        out_specs=pl.BlockSpec((tm, tn), lambda j, i: (i, j)),
        out_shape=jax.ShapeDtypeStruct((m, n), out_dtype),
        scratch_shapes=[pltpu.VMEM((k, tn), BF16)],
        compiler_params=_params("arbitrary", "arbitrary"),
        name="matmul",
    )(*args)


def _mlstm_body(proj_ref, gt_ref, bg_ref, ng_ref, out_ref, c_s, n_s, m_s, *, dk, dv):
    H = MLSTM_HEADS

    @pl.when(pl.program_id(0) == 0)
    def _():
        c_s[...] = jnp.zeros_like(c_s)
        n_s[...] = jnp.zeros_like(n_s)
        m_s[...] = jnp.zeros_like(m_s)

    for h in range(H):
        _mlstm_head(h,
                    proj_ref.at[:, pl.ds(h * dk, dk)],
                    proj_ref.at[:, pl.ds(H * dk + h * dk, dk)],
                    proj_ref.at[:, pl.ds(2 * H * dk + h * dv, dv)],
                    proj_ref.at[:, pl.ds(2 * H * dk + H * dv + h * dv, dv)],
                    gt_ref, bg_ref, ng_ref.at[:, pl.ds(h * dv, dv)], out_ref.at[:, pl.ds(h * dv, dv)],
                    c_s.at[h], n_s.at[h], m_s.at[h])


def _mlstm_head(h, q_ref, k_ref, v_ref, o_ref, gt_ref, bg_ref, ng_ref, out_ref, c_s, n_s, m_s):
    L = q_ref.shape[0]
    dk = q_ref.shape[1]
    dv = v_ref.shape[1]

    lane = lax.broadcasted_iota(jnp.int32, (L, LANES), 1)
    row = lax.broadcasted_iota(jnp.int32, (L, L), 0)
    col = lax.broadcasted_iota(jnp.int32, (L, L), 1)

    capped = GATE_SOFTCAP * jnp.tanh((gt_ref[...] + bg_ref[...]) / GATE_SOFTCAP)
    li = jnp.sum(jnp.where(lane == h, capped, 0.0), axis=-1, keepdims=True)
    fpre = jnp.sum(jnp.where(lane == h + MLSTM_HEADS, capped, 0.0), axis=-1, keepdims=True)
    lf = jax.nn.log_sigmoid(fpre)

    tril = (row >= col).astype(F32)
    g_b = jnp.dot(tril, jnp.broadcast_to(lf, (L, L)), precision=HIGHEST,
                  preferred_element_type=F32)
    g = g_b[:, 0:1]
    g_tot = jnp.sum(lf, axis=0, keepdims=True)
    eye = row == col
    g_row = jnp.sum(jnp.where(eye, g_b, 0.0), axis=0, keepdims=True)
    li_row = jnp.sum(jnp.where(eye, jnp.broadcast_to(li, (L, L)), 0.0), axis=0, keepdims=True)

    m_prev = m_s[...]
    c_prev = c_s[...]
    n_prev = n_s[...]

    q = q_ref[...] * (dk ** -0.5)
    k = k_ref[...]
    vb = v_ref[...].astype(BF16)
    qb = q.astype(BF16)

    d_log = jnp.where(row >= col, g_b - g_row + li_row, -jnp.inf)
    inter_log = g + m_prev
    m_t = jnp.maximum(inter_log, jnp.max(d_log, axis=-1, keepdims=True))
    s = _dot_nt(qb, k) * jnp.exp(d_log - m_t)
    inter = jnp.exp(inter_log - m_t)
    num = _bdot(s, vb) + inter * _bdot(qb, c_prev)
    den = jnp.sum(s, axis=-1, keepdims=True) + inter * jnp.sum(q * n_prev, axis=-1, keepdims=True)
    hh = num / jnp.maximum(jnp.abs(den), jnp.exp(-m_t))
    hh = hh * lax.rsqrt(jnp.mean(hh * hh, axis=-1, keepdims=True) + NORM_EPS) * ng_ref[...]
    out_ref[...] = (jax.nn.sigmoid(o_ref[...]) * hh).astype(out_ref.dtype)

    w_state = g_tot - g + li
    m_loc = jnp.max(w_state, axis=0, keepdims=True)
    ks = jnp.exp(w_state - m_loc) * k
    kv = _dot_tn(ks, vb)
    n_c = jnp.sum(ks, axis=0, keepdims=True)
    m_new = jnp.maximum(g_tot + m_prev, m_loc)
    a = jnp.exp(g_tot + m_prev - m_new)
    b = jnp.exp(m_loc - m_new)
    c_s[...] = a * c_prev + b * kv
    n_s[...] = a * n_prev + b * n_c
    m_s[...] = m_new


def mlstm_cells(proj, gates, b_gates_pad, norm_gain):
    s = proj.shape[0]
    d = norm_gain.shape[0]
    H, L = MLSTM_HEADS, MLSTM_CHUNK
    dk = (proj.shape[1] - 2 * d) // (2 * H)
    dv = d // H
    n = proj.shape[1]
    return pl.pallas_call(
        functools.partial(_mlstm_body, dk=dk, dv=dv),
        grid=(s // L,),
        in_specs=[pl.BlockSpec((L, n), lambda c: (c, 0)),
                  pl.BlockSpec((L, LANES), lambda c: (c, 0)),
                  pl.BlockSpec((1, LANES), lambda c: (0, 0)),
                  pl.BlockSpec((1, d), lambda c: (0, 0))],
        out_specs=pl.BlockSpec((L, d), lambda c: (c, 0)),
        out_shape=jax.ShapeDtypeStruct((s, d), BF16),
        scratch_shapes=[pltpu.VMEM((H, dk, dv), F32), pltpu.VMEM((H, 1, dk), F32), pltpu.VMEM((H, 1, 1), F32)],
        compiler_params=_params("arbitrary"),
        name="mlstm_cells",
    )(proj, gates, b_gates_pad, norm_gain.reshape(1, d))


def mlstm_mixer(x, gain, shift, scale, w_in, b_gates, norm_gain, w_out, gate_row):
    d = x.shape[1]
    n_main = w_in.shape[1] - 2 * MLSTM_HEADS
    hm = norm_mod(x, gain, shift, scale, BF16)
    proj = matmul(hm, w_in[:, :n_main])
    w_g = jnp.pad(w_in[:, n_main:], ((0, 0), (0, LANES - 2 * MLSTM_HEADS)))
    gates = matmul(hm, w_g)
    b_pad = jnp.pad(b_gates, (0, LANES - 2 * MLSTM_HEADS)).reshape(1, LANES)
    cells = mlstm_cells(proj, gates, b_pad, norm_gain)
    return matmul(cells, w_out, res=x, gate=gate_row)


def _rwkv_prep_body(x_ref, xp_ref, g_ref, sh_ref, sc_ref, mu_ref, *outs):
    tm = x_ref.shape[0]
    hm = _norm_rows(x_ref[...], g_ref[...], sh_ref[...], sc_ref[...], NORM_EPS)
    prev = _norm_rows(xp_ref[7:8, :], g_ref[...], sh_ref[...], sc_ref[...], NORM_EPS)
    prev = jnp.where(pl.program_id(0) == 0, 0.0, prev)
    rows = lax.broadcasted_iota(jnp.int32, hm.shape, 0)
    shifted = jnp.where(rows == 0, prev, pltpu.roll(hm, 1, axis=0))
    xx = shifted - hm
    for j, o_ref in enumerate(outs):
        o_ref[...] = (hm + xx * mu_ref[j:j + 1, :]).astype(o_ref.dtype)


def rwkv_prep(x, gain, shift, scale, mu, tm=256):
    s, d = x.shape
    row = pl.BlockSpec((1, d), lambda i: (0, 0))
    blk = pl.BlockSpec((tm, d), lambda i: (i, 0))
    return pl.pallas_call(
        _rwkv_prep_body,
        grid=(s // tm,),
        in_specs=[blk, pl.BlockSpec((8, d), lambda i: (jnp.maximum(i * (tm // 8) - 1, 0), 0)),
                  row, row, row, pl.BlockSpec((8, d), lambda i: (0, 0))],
        out_specs=[blk] * 6,
        out_shape=[jax.ShapeDtypeStruct((s, d), BF16)] * 6,
        compiler_params=_params("arbitrary"),
        name="rwkv_prep",
    )(x, x, gain.reshape(1, d), shift.reshape(1, d), scale.reshape(1, d), jnp.pad(mu, ((0, 2), (0, 0))))


def _lora_body(x_ref, w1_ref, w2_ref, o_ref, *, act):
    t = _bdot(x_ref[...], w1_ref[...])
    if act == "tanh":
        t = jnp.tanh(t)
    elif act == "sigmoid":
        t = jax.nn.sigmoid(t)
    o_ref[...] = _bdot(t, w2_ref[...])


def lora(x, w1, w2, act, tm=512):
    s, d = x.shape
    r = w1.shape[1]
    rp = -(-r // LANES) * LANES
    w1p = jnp.pad(w1, ((0, 0), (0, rp - r)))
    w2p = jnp.pad(w2, ((0, rp - r), (0, 0)))
    n = w2.shape[1]
    return pl.pallas_call(
        functools.partial(_lora_body, act=act),
        grid=(s // tm,),
        in_specs=[pl.BlockSpec((tm, d), lambda i: (i, 0)),
                  pl.BlockSpec((d, rp), lambda i: (0, 0)),
                  pl.BlockSpec((rp, n), lambda i: (0, 0))],
        out_specs=pl.BlockSpec((tm, n), lambda i: (i, 0)),
        out_shape=jax.ShapeDtypeStruct((s, n), F32),
        compiler_params=_params("arbitrary"),
        name="lora",
    )(x, w1p, w2p)


def _split3(x):
    hi = x.astype(BF16)
    r1 = x - hi.astype(F32)
    mid = r1.astype(BF16)
    lo = (r1 - mid.astype(F32)).astype(BF16)
    return hi, mid, lo


def _segsum(x, bmat, passes):
    parts = _split3(x)[:passes] if passes > 1 else (x.astype(BF16),)
    acc = jnp.dot(parts[0], bmat, preferred_element_type=F32)
    for p in parts[1:]:
        acc = acc + jnp.dot(p, bmat, preferred_element_type=F32)
    return acc


SUBLANES = 8


SCAN_W = 256


def _head_ones(width):
    ri = lax.broadcasted_iota(jnp.int32, (width, width), 0)
    ci = lax.broadcasted_iota(jnp.int32, (width, width), 1)
    return ((ri // RWKV_HEAD) == (ci // RWKV_HEAD)).astype(BF16)


def _rwkv_pre_body(k_ref, wl_ref, al_ref, w0_ref, a0_ref, kk_ref, ka_ref, nkk_ref, w_ref, kka_ref, kp_ref):
    d = k_ref.shape[1]
    bmat = _head_ones(SCAN_W)
    for j in range(d // SCAN_W):
        cs = slice(j * SCAN_W, (j + 1) * SCAN_W)
        kraw = k_ref[:, cs]
        w_log = -jax.nn.softplus(-(w0_ref[:, cs] + wl_ref[:, cs])) - 0.5
        w_ref[:, cs] = -jnp.exp(w_log)
        a = jax.nn.sigmoid(a0_ref[:, cs] + al_ref[:, cs])
        kk = kraw * kk_ref[:, cs]
        nrm = jnp.sqrt(_segsum(kk * kk, bmat, 3))
        kk = kk / jnp.maximum(nrm, 1e-12)
        nkk_ref[:, cs] = -kk
        kka_ref[:, cs] = kk * a
        kp_ref[:, cs] = kraw * (1.0 + (a - 1.0) * ka_ref[:, cs])


def rwkv_pre(k, wl, al, w0, a0, k_k, k_a, tm=256):
    s, d = k.shape
    blk = pl.BlockSpec((tm, d), lambda i: (i, 0))
    row = pl.BlockSpec((1, d), lambda i: (0, 0))
    return pl.pallas_call(
        _rwkv_pre_body,
        grid=(s // tm,),
        in_specs=[blk] * 3 + [row] * 4,
        out_specs=[blk] * 4,
        out_shape=[jax.ShapeDtypeStruct((s, d), F32)] * 4,
        compiler_params=_params("arbitrary"),
        name="rwkv_pre",
    )(k, wl, al, *[t.reshape(1, d) for t in (w0, a0, k_k, k_a)])


def _rwkv_post_body(y_ref, r_ref, kp_ref, v_ref, gt_ref, rk_ref, lg_ref, lb_ref, o_ref):
    d = y_ref.shape[1]
    bmat = _head_ones(SCAN_W)
    inv = 1.0 / RWKV_HEAD
    for j in range(d // SCAN_W):
        cs = slice(j * SCAN_W, (j + 1) * SCAN_W)
        y = y_ref[:, cs]
        mean = _segsum(y, bmat, 3) * inv
        yc = y - mean
        var = _segsum(yc * yc, bmat, 3) * inv
        yn = yc * lax.rsqrt(var + RWKV_GN_EPS) * lg_ref[:, cs] + lb_ref[:, cs]
        bonus = _segsum(r_ref[:, cs] * kp_ref[:, cs] * rk_ref[:, cs], bmat, 3) * v_ref[:, cs]
        o_ref[:, cs] = ((yn + bonus) * gt_ref[:, cs]).astype(o_ref.dtype)


def rwkv_post(y, r, kp, v, gt, r_k, ln_gain, ln_bias, tm=256):
    s, d = y.shape
    blk = pl.BlockSpec((tm, d), lambda i: (i, 0))
    row = pl.BlockSpec((1, d), lambda i: (0, 0))
    return pl.pallas_call(
        _rwkv_post_body,
        grid=(s // tm,),
        in_specs=[blk] * 5 + [row] * 3,
        out_specs=blk,
        out_shape=jax.ShapeDtypeStruct((s, d), BF16),
        compiler_params=_params("arbitrary"),
        name="rwkv_post",
    )(y, r, kp, v, gt, *[t.reshape(1, d) for t in (r_k, ln_gain, ln_bias)])


RWKV_CHUNK = 32


def _rwkv_chunk_body(a_ref, lw_ref, b_ref, kp_ref, r_ref, v_ref, y_ref, st_s):
    L, d = r_ref.shape
    nv = RWKV_HEAD
    W = SCAN_W
    hpt = W // nv
    ngrp = L // SUBLANES

    @pl.when(pl.program_id(0) == 0)
    def _():
        st_s[...] = jnp.zeros_like(st_s)

    lane_head = lax.broadcasted_iota(jnp.int32, (1, W), 1) // nv
    heads = [lane_head == h for h in range(hpt)]
    ti = lax.broadcasted_iota(jnp.int32, (L, L), 0)
    si = lax.broadcasted_iota(jnp.int32, (L, L), 1)
    tril = (ti >= si).astype(F32)
    t_row = lax.broadcasted_iota(jnp.int32, (2 * L, 2 * hpt * L), 0)
    s_lane = lax.broadcasted_iota(jnp.int32, (2 * L, 2 * hpt * L), 1) % L
    g_mask = s_lane <= jnp.where(t_row < L, t_row - 1, t_row - L)
    sub = lax.broadcasted_iota(jnp.int32, (SUBLANES, W), 0)
    ones = _head_ones(W)

    def stack(x):
        return jnp.concatenate([jnp.where(hm, x, 0.0) for hm in heads], axis=0)

    for j in range(d // W):
        cs = slice(j * W, (j + 1) * W)
        rs = slice(j * nv, (j + 1) * nv)
        lw = lw_ref[:, cs]
        cum = jnp.dot(tril, lw, precision=HIGHEST, preferred_element_type=F32)
        tot = cum[L - 1:L, :]
        w_inv = jnp.exp(-cum)
        w_rest = jnp.exp(tot - cum)
        a_hat = a_ref[:, cs] * jnp.exp(cum - lw)
        r_hat = r_ref[:, cs] * jnp.exp(cum)
        k_hat = kp_ref[:, cs] * w_inv
        b_hat = b_ref[:, cs] * w_inv
        v = v_ref[:, cs]
        s0 = st_s[rs, :]

        ar = jnp.concatenate([a_hat, r_hat], axis=0).astype(BF16)
        uy0 = _dot_nt(ar, stack(s0))
        g = _dot_nt(ar, jnp.concatenate([stack(k_hat), stack(b_hat)], axis=0))
        g = jnp.where(g_mask, g, 0.0)
        v_st = stack(v)
        base = uy0[0:L] + _bdot(g[0:L, 0:hpt * L], v_st)

        need = [(s, g8) for s in range(L - 1) for g8 in range(s // SUBLANES, ngrp)]
        parts = [a_hat[g8 * SUBLANES:(g8 + 1) * SUBLANES] * b_hat[s:s + 1, :] for s, g8 in need]
        if len(parts) % 2:
            parts.append(jnp.zeros((SUBLANES, W), F32))
        lhs = jnp.concatenate(parts, axis=0)
        coef = jnp.dot(lhs.astype(BF16), ones, preferred_element_type=F32)
        at = {sg: n * SUBLANES for n, sg in enumerate(need)}
        u = [base[g8 * SUBLANES:(g8 + 1) * SUBLANES] for g8 in range(ngrp)]
        for s in range(L - 1):
            g0, i0 = divmod(s, SUBLANES)
            us = jnp.broadcast_to(u[g0][i0:i0 + 1, :], (SUBLANES, W))
            for g8 in range(g0, ngrp):
                c = coef[at[s, g8]:at[s, g8] + SUBLANES]
                if g8 == g0:
                    c = jnp.where(sub > i0, c, 0.0)
                u[g8] = u[g8] + c * us
        u = jnp.concatenate(u, axis=0)

        vu = jnp.concatenate([v_st, stack(u)], axis=0)
        y_ref[:, cs] = uy0[L:2 * L] + _bdot(g[L:2 * L], vu)

        kb_rest = jnp.concatenate([kp_ref[:, cs] * w_rest, b_ref[:, cs] * w_rest], axis=0)
        upd = _dot_tn(jnp.concatenate([v, u], axis=0), kb_rest)
        new = s0 * jnp.exp(tot)
        for h, hm in enumerate(heads):
            new = new + jnp.where(hm, upd[h * nv:(h + 1) * nv], 0.0)
        st_s[rs, :] = new


def rwkv_chunked(nkk, lw, kka, kp, r, v):
    s, d = r.shape
    L = RWKV_CHUNK
    blk = pl.BlockSpec((L, d), lambda i: (i, 0))
    return pl.pallas_call(
        _rwkv_chunk_body,
        grid=(s // L,),
        in_specs=[blk] * 6,
        out_specs=blk,
        out_shape=jax.ShapeDtypeStruct((s, d), F32),
        scratch_shapes=[pltpu.VMEM(((d // SCAN_W) * RWKV_HEAD, SCAN_W), F32)],
        compiler_params=_params("arbitrary"),
        name="rwkv_chunked",
    )(nkk, lw, kka, kp, r, v)


def rwkv_mixer(x, gain, shift, scale, mu, w_rkv, w0, w1, w2, a0, a1, a2, g1, g2, k_k, k_a, r_k,
               ln_gain, ln_bias, w_o, gate_row):
    xr, xk, xv, xw, xa, xg = rwkv_prep(x, gain, shift, scale, mu)
    r = matmul(xr, w_rkv[0])
    k = matmul(xk, w_rkv[1])
    v = matmul(xv, w_rkv[2])
    wl = lora(xw, w1, w2, "tanh")
    al = lora(xa, a1, a2, "none")
    gt = lora(xg, g1, g2, "sigmoid")
    nkk, lw, kka, kp = rwkv_pre(k, wl, al, w0, a0, k_k, k_a)
    y = rwkv_chunked(nkk, lw, kka, kp, r, v)
    y = rwkv_post(y, r, kp, v, gt, r_k.reshape(-1), ln_gain, ln_bias)
    return matmul(y, w_o, res=x, gate=gate_row)


def _rope_body(p_ref, pos_ref, o_ref, *, n_rot_tiles, q_tiles):
    tm, n = p_ref.shape
    half = ROPE_DIM // 2
    lane = lax.broadcasted_iota(jnp.int32, (tm, LANES), 1)
    inv_freq = jnp.exp((lane % half).astype(F32) * (-2.0 * math.log(ROPE_THETA) / ROPE_DIM))
    ang = pos_ref[...].astype(F32) * inv_freq
    roped = lane < ROPE_DIM
    cos = jnp.where(roped, jnp.cos(ang), 1.0)
    sin = jnp.where(roped, jnp.where(lane < half, -jnp.sin(ang), jnp.sin(ang)), 0.0)
    for j in range(n // LANES):
        x = p_ref[:, j * LANES:(j + 1) * LANES]
        if j < n_rot_tiles:
            partner = jnp.where(lane < half, pltpu.roll(x, LANES - half, axis=1), pltpu.roll(x, half, axis=1))
            x = x * cos + partner * sin
            if j < q_tiles:
                x = x * (DIFF_DQK ** -0.5)
        o_ref[:, j * LANES:(j + 1) * LANES] = x.astype(o_ref.dtype)


def rope_cast(proj, positions, tm=256):
    s, n = proj.shape
    q_tiles = 2 * DIFF_HEADS
    return pl.pallas_call(
        functools.partial(_rope_body, n_rot_tiles=2 * q_tiles, q_tiles=q_tiles),
        grid=(s // tm,),
        in_specs=[pl.BlockSpec((tm, n), lambda i: (i, 0)),
                  pl.BlockSpec((tm, 1), lambda i: (i, 0))],
        out_specs=pl.BlockSpec((tm, n), lambda i: (i, 0)),
        out_shape=jax.ShapeDtypeStruct((s, n), BF16),
        compiler_params=_params("arbitrary"),
        name="rope_cast",
    )(proj, positions.reshape(s, 1))


def _diff_attn_body(qi_ref, kj_ref, q_ref, k_ref, v_ref, lam_ref, sg_ref, o_ref, m_s, l_s, acc_s, *, lambda_init):
    step = pl.program_id(1)
    qi, kj = qi_ref[step], kj_ref[step]
    tb = q_ref.shape[0]

    @pl.when(kj == 0)
    def _():
        m_s[...] = jnp.full_like(m_s, -jnp.inf)
        l_s[...] = jnp.zeros_like(l_s)
        acc_s[...] = jnp.zeros_like(acc_s)

    def accumulate(diagonal):
        v = v_ref[...]
        for c in range(2):
            cs = slice(c * DIFF_DQK, (c + 1) * DIFF_DQK)
            s = _dot_nt(q_ref[:, cs], k_ref[:, cs])
            if diagonal:
                rows = lax.broadcasted_iota(jnp.int32, (tb, tb), 0)
                cols = lax.broadcasted_iota(jnp.int32, (tb, tb), 1)
                s = jnp.where(cols <= rows, s, -jnp.inf)
            m_old = m_s[c]
            m_new = jnp.maximum(m_old, jnp.max(s, axis=-1, keepdims=True))
            alpha = jnp.exp(m_old - m_new)
            p = jnp.exp(s - m_new)
            l_s[c] = alpha * l_s[c] + jnp.sum(p, axis=-1, keepdims=True)
            acc_s[c] = alpha * acc_s[c] + _bdot(p, v)
            m_s[c] = m_new

    @pl.when(kj < qi)
    def _():
        accumulate(False)

    @pl.when(kj == qi)
    def _():
        accumulate(True)
        lp = lam_ref[...]
        lam = (jnp.exp(jnp.sum(lp[0:1] * lp[1:2], axis=-1, keepdims=True))
               - jnp.exp(jnp.sum(lp[2:3] * lp[3:4], axis=-1, keepdims=True)) + lambda_init)
        out = acc_s[0] / l_s[0] - lam * (acc_s[1] / l_s[1])
        out = out * lax.rsqrt(jnp.mean(out * out, axis=-1, keepdims=True) + DIFF_SUBLN_EPS)
        o_ref[...] = (out * sg_ref[...] * (1.0 - lambda_init)).astype(o_ref.dtype)


def diff_attention(qkv, lam_params, subln_gain, lambda_init, tb=1024):
    s = qkv.shape[0]
    H, dv = DIFF_HEADS, DIFF_DV
    tb = min(tb, s)
    nb = s // tb
    pairs = [(qi, kj) for qi in range(nb) for kj in range(qi + 1)]
    qi_tab = jnp.asarray([p[0] for p in pairs], jnp.int32)
    kj_tab = jnp.asarray([p[1] for p in pairs], jnp.int32)
    grid_spec = pltpu.PrefetchScalarGridSpec(
        num_scalar_prefetch=2,
        grid=(H, len(pairs)),
        in_specs=[pl.BlockSpec((tb, dv), lambda h, t, qi, kj: (qi[t], h)),
                  pl.BlockSpec((tb, dv), lambda h, t, qi, kj: (kj[t], H + h)),
                  pl.BlockSpec((tb, dv), lambda h, t, qi, kj: (kj[t], 2 * H + h)),
                  pl.BlockSpec((4, DIFF_DQK), lambda h, t, qi, kj: (0, 0)),
                  pl.BlockSpec((1, dv), lambda h, t, qi, kj: (0, 0))],
        out_specs=pl.BlockSpec((tb, dv), lambda h, t, qi, kj: (qi[t], h)),
        scratch_shapes=[pltpu.VMEM((2, tb, 1), F32), pltpu.VMEM((2, tb, 1), F32),
                        pltpu.VMEM((2, tb, dv), F32)],
    )
    return pl.pallas_call(
        functools.partial(_diff_attn_body, lambda_init=lambda_init),
        grid_spec=grid_spec,
        out_shape=jax.ShapeDtypeStruct((s, H * dv), BF16),
        compiler_params=_params("arbitrary", "arbitrary"),
        name="diff_attention",
    )(qi_tab, kj_tab, qkv, qkv, qkv, lam_params, subln_gain.reshape(1, dv))


def diff_mixer(x, gain, shift, scale, positions, w_qkv, lam_params, subln_gain, w_o, layer, gate_row):
    hm = norm_mod(x, gain, shift, scale, BF16)
    proj = matmul(hm, w_qkv)
    qkv = rope_cast(proj, positions)
    lambda_init = 0.8 - 0.6 * math.exp(-0.3 * layer)
    att = diff_attention(qkv, lam_params, subln_gain, lambda_init)
    return matmul(att, w_o, res=x, gate=gate_row)


HI16 = 0xFFFF0000


def _pack_rows(x):
    m = x.shape[1] // 2
    bits = lax.bitcast_convert_type(x.astype(BF16).astype(F32), jnp.uint32)
    return (bits[:, :m] >> 16) | (bits[:, m:] & jnp.uint32(HI16))


def _unpack_rows(p):
    lo = lax.bitcast_convert_type(p << 16, F32)
    hi = lax.bitcast_convert_type(p & jnp.uint32(HI16), F32)
    return lo, hi


def _router_body(x_ref, g_ref, sh_ref, sc_ref, rw_ref, rb_ref, hf_ref, lg_ref):
    hf = _norm_rows(x_ref[...], g_ref[...], sh_ref[...], sc_ref[...], NORM_EPS)
    hf_ref[...] = _pack_rows(hf)
    rw = rw_ref[...]
    xh = hf.astype(BF16)
    xl = (hf - xh.astype(F32)).astype(BF16)
    wh = rw.astype(BF16)
    wl = (rw - wh.astype(F32)).astype(BF16)
    lg_ref[...] = (jnp.dot(xh, wh, preferred_element_type=F32) + jnp.dot(xh, wl, preferred_element_type=F32)
                   + jnp.dot(xl, wh, preferred_element_type=F32)) + rb_ref[...]


def router(x, gain, shift, scale, router_w, router_b, tm=256):
    s, d = x.shape
    e = router_w.shape[1]
    rw = jnp.pad(router_w, ((0, 0), (0, LANES - e)))
    rb = jnp.pad(router_b, (0, LANES - e), constant_values=-1e30).reshape(1, LANES)
    row = pl.BlockSpec((1, d), lambda i: (0, 0))
    return pl.pallas_call(
        _router_body,
        grid=(s // tm,),
        in_specs=[pl.BlockSpec((tm, d), lambda i: (i, 0)), row, row, row,
                  pl.BlockSpec((d, LANES), lambda i: (0, 0)), pl.BlockSpec((1, LANES), lambda i: (0, 0))],
        out_specs=[pl.BlockSpec((tm, d // 2), lambda i: (i, 0)), pl.BlockSpec((tm, LANES), lambda i: (i, 0))],
        out_shape=[jax.ShapeDtypeStruct((s, d // 2), jnp.uint32), jax.ShapeDtypeStruct((s, LANES), F32)],
        compiler_params=_params("arbitrary"),
        name="router",
    )(x, gain.reshape(1, d), shift.reshape(1, d), scale.reshape(1, d), rw, rb)


def _topk_body(lg_ref, eidx_ref, gw_ref, rk_ref, cnt_ref):
    tm = lg_ref.shape[0]

    @pl.when(pl.program_id(0) == 0)
    def _():
        cnt_ref[...] = jnp.zeros_like(cnt_ref)

    lane = lax.broadcasted_iota(jnp.int32, (tm, LANES), 1).astype(F32)
    work = lg_ref[...]
    vals, idxs, hots = [], [], []
    for _ in range(TOP_K):
        m = jnp.max(work, axis=-1, keepdims=True)
        idx = jnp.min(jnp.where(work == m, lane, float(LANES)), axis=-1, keepdims=True)
        hot = lane == idx
        work = jnp.where(hot, -jnp.inf, work)
        vals.append(m)
        idxs.append(idx)
        hots.append(hot)
    exps = [jnp.exp(v - vals[0]) for v in vals]
    den = exps[0] + exps[1] + exps[2] + exps[3]
    sel = jnp.zeros((tm, LANES), F32)
    for hot in hots:
        sel = sel + hot.astype(F32)
    ri = lax.broadcasted_iota(jnp.int32, (tm, tm), 0)
    ci = lax.broadcasted_iota(jnp.int32, (tm, tm), 1)
    before = jnp.dot((ri > ci).astype(BF16), sel.astype(BF16), preferred_element_type=F32)
    rank = before + cnt_ref[...]
    cnt_ref[...] = cnt_ref[...] + jnp.sum(sel, axis=0, keepdims=True)
    eidx = jnp.zeros((tm, LANES), F32)
    gw = jnp.zeros((tm, LANES), F32)
    rk = jnp.zeros((tm, LANES), F32)
    for kslot in range(TOP_K):
        here = lane == kslot
        eidx = jnp.where(here, idxs[kslot], eidx)
        gw = jnp.where(here, exps[kslot] / den, gw)
        rk = jnp.where(here, jnp.sum(jnp.where(hots[kslot], rank, 0.0), axis=-1, keepdims=True), rk)
    eidx_ref[...] = eidx
    gw_ref[...] = gw
    rk_ref[...] = rk


def topk_route(logits, tm=TOK_BLOCK):
    s = logits.shape[0]
    blk = pl.BlockSpec((tm, LANES), lambda i: (i, 0))
    return pl.pallas_call(
        _topk_body,
        grid=(s // tm,),
        in_specs=[blk],
        out_specs=[blk, blk, blk, pl.BlockSpec((1, LANES), lambda i: (0, 0))],
        out_shape=[jax.ShapeDtypeStruct((s, LANES), F32), jax.ShapeDtypeStruct((s, LANES), F32),
                   jax.ShapeDtypeStruct((s, LANES), F32), jax.ShapeDtypeStruct((1, LANES), F32)],
        compiler_params=_params("arbitrary"),
        name="topk_route",
    )(logits)


def _slot_body(eidx_ref, rk_ref, ps_ref, pos_ref):
    tm = eidx_ref.shape[0]
    lane = lax.broadcasted_iota(jnp.int32, (tm, LANES), 1).astype(F32)
    eidx = eidx_ref[...]
    rk = rk_ref[...]
    pos = jnp.zeros((tm, LANES), F32)
    for kslot in range(TOP_K):
        e_k = jnp.sum(jnp.where(lane == kslot, eidx, 0.0), axis=-1, keepdims=True)
        start = jnp.sum(jnp.where(lane == e_k, ps_ref[...], 0.0), axis=-1, keepdims=True)
        r_k = jnp.sum(jnp.where(lane == kslot, rk, 0.0), axis=-1, keepdims=True)
        pos = jnp.where(lane == kslot, start + r_k, pos)
    pos_ref[...] = pos.astype(jnp.int32)


def slot_positions(eidx, rk, padded_start, tm=TOK_BLOCK):
    s = eidx.shape[0]
    blk = pl.BlockSpec((tm, LANES), lambda i: (i, 0))
    return pl.pallas_call(
        _slot_body,
        grid=(s // tm,),
        in_specs=[blk, blk, pl.BlockSpec((1, LANES), lambda i: (0, 0))],
        out_specs=blk,
        out_shape=jax.ShapeDtypeStruct((s, LANES), jnp.int32),
        compiler_params=_params("arbitrary"),
        name="slot_positions",
    )(eidx, rk, padded_start)


def _dispatch_body(pos_ref, hf_ref, xs_in_ref, xs_ref, sem):
    del xs_in_ref
    tm = hf_ref.shape[0]

    def issue(t, carry):
        for kslot in range(TOP_K):
            p = pos_ref[t * TOP_K + kslot]
            pltpu.make_async_copy(hf_ref.at[pl.ds(t, 1)], xs_ref.at[pl.ds(p, 1)], sem.at[0]).start(
                priority=kslot % 2)
        return carry

    lax.fori_loop(0, tm, issue, 0, unroll=4)

    for _ in range(TOP_K):
        pltpu.make_async_copy(hf_ref, xs_ref.at[pl.ds(0, tm)], sem.at[0]).wait()


def dispatch(pos_flat, hf, n_rows, tm=TOK_BLOCK):
    s, d = hf.shape
    return pl.pallas_call(
        _dispatch_body,
        grid=(s // tm,),
        in_specs=[pl.BlockSpec((tm * TOP_K,), lambda i: (i,), memory_space=pltpu.SMEM),
                  pl.BlockSpec((tm, d), lambda i: (i, 0)),
                  pl.BlockSpec(memory_space=pl.ANY)],
        out_specs=pl.BlockSpec(memory_space=pl.ANY),
        out_shape=jax.ShapeDtypeStruct((n_rows, d), hf.dtype),
        scratch_shapes=[pltpu.SemaphoreType.DMA((1,))],
        input_output_aliases={2: 0},
        compiler_params=_params("arbitrary"),
        name="moe_dispatch",
    )(pos_flat, hf, jnp.zeros((n_rows, d), hf.dtype))


def _expert_body(be_ref, first_ref, nxt_ref, tot_ref, xs_ref, wgu_hbm, bgu_ref, wdn_hbm, bdn_ref, perm_ref, ys_ref,
                 stg_gu, stg_dn, wgu_s, wdn_s, sem, *, layer, chunk):
    i = pl.program_id(0)
    rows = xs_ref.shape[0]
    d, n_gu = wgu_s.shape
    dh = wdn_s.shape[0]
    grp = perm_ref.shape[0]
    half = grp // 2

    def fetch(e):
        return (pltpu.make_async_copy(wgu_hbm.at[layer, e], stg_gu, sem.at[0]),
                pltpu.make_async_copy(wdn_hbm.at[layer, e], stg_dn, sem.at[1]))

    @pl.when(first_ref[i] == 1)
    def _():
        @pl.when(i == 0)
        def _():
            for cp in fetch(be_ref[0]):
                cp.start()

        for cp in fetch(be_ref[i]):
            cp.wait()
        for c in range(d // chunk):
            wgu_s[c * chunk:(c + 1) * chunk, :] = stg_gu[c * chunk:(c + 1) * chunk, :].astype(BF16)
        for c in range(dh // chunk):
            wdn_s[c * chunk:(c + 1) * chunk, :] = stg_dn[c * chunk:(c + 1) * chunk, :].astype(BF16)

        @pl.when(nxt_ref[i] >= 0)
        def _():
            for cp in fetch(nxt_ref[i]):
                cp.start()

    @pl.when(i * rows < tot_ref[0])
    def _():
        x_lo, x_hi = _unpack_rows(xs_ref[...])
        gu = _bdot(x_lo, wgu_s[0:d // 2, :]) + _bdot(x_hi, wgu_s[d // 2:d, :]) + bgu_ref[0]
        gub = gu.astype(BF16)
        gs, us = [], []
        for j in range(n_gu // grp):
            pj = jnp.dot(gub[:, j * grp:(j + 1) * grp], perm_ref[...], preferred_element_type=F32)
            gs.append(pj[:, :half])
            us.append(pj[:, half:])
        g = jnp.minimum(jnp.concatenate(gs, axis=1), SWIGLU_LIMIT)
        u = jnp.clip(jnp.concatenate(us, axis=1), -SWIGLU_LIMIT, SWIGLU_LIMIT)
        act = (u + 1.0) * (g * jax.nn.sigmoid(g * SWIGLU_ALPHA))
        ys_ref[...] = _pack_rows(_bdot(act, wdn_s[...]) + bdn_ref[0])

    @pl.when(i * rows >= tot_ref[0])
    def _():
        ys_ref[...] = jnp.zeros_like(ys_ref)


def expert_ffn(blk_expert, blk_first, blk_next, total_rows, xs, w_gu, b_gu, w_dn, b_dn, layer, rows=MOE_ROWS):
    n_rows, dp = xs.shape
    _, e, d, n_gu = w_gu.shape
    dh = w_dn.shape[2]
    grp = 2 * LANES
    perm = np.zeros((grp, grp), np.float32)
    perm[2 * np.arange(LANES), np.arange(LANES)] = 1.0
    perm[2 * np.arange(LANES) + 1, LANES + np.arange(LANES)] = 1.0
    grid_spec = pltpu.PrefetchScalarGridSpec(
        num_scalar_prefetch=4,
        grid=(n_rows // rows,),
        in_specs=[pl.BlockSpec((rows, dp), lambda i, be, fi, nx, tot: (i, 0)),
                  pl.BlockSpec(memory_space=pl.ANY),
                  pl.BlockSpec((1, 1, n_gu), lambda i, be, fi, nx, tot: (be[i], 0, 0)),
                  pl.BlockSpec(memory_space=pl.ANY),
                  pl.BlockSpec((1, 1, d), lambda i, be, fi, nx, tot: (be[i], 0, 0)),
                  pl.BlockSpec((grp, grp), lambda i, be, fi, nx, tot: (0, 0))],
        out_specs=pl.BlockSpec((rows, dp), lambda i, be, fi, nx, tot: (i, 0)),
        scratch_shapes=[pltpu.VMEM((d, n_gu), F32), pltpu.VMEM((dh, d), F32),
                        pltpu.VMEM((d, n_gu), BF16), pltpu.VMEM((dh, d), BF16),
                        pltpu.SemaphoreType.DMA((2,))],
    )
    return pl.pallas_call(
        functools.partial(_expert_body, layer=layer, chunk=256),
        grid_spec=grid_spec,
        out_shape=jax.ShapeDtypeStruct((n_rows, dp), jnp.uint32),
        compiler_params=_params("arbitrary"),
        name="moe_experts",
    )(blk_expert, blk_first, blk_next, total_rows, xs, w_gu, b_gu.reshape(e, 1, n_gu), w_dn,
      b_dn.reshape(e, 1, d), jnp.asarray(perm, BF16))


def _combine_body(pos_ref, x_ref, gw_ref, g2_ref, ys_ref, o_ref, buf, sem):
    tm = x_ref.shape[0]

    def issue(t, carry):
        for kslot in range(TOP_K):
            p = pos_ref[t * TOP_K + kslot]
            pltpu.make_async_copy(ys_ref.at[pl.ds(p, 1)], buf.at[kslot, pl.ds(t, 1)], sem.at[0]).start(
                priority=kslot % 2)
        return carry

    lax.fori_loop(0, tm, issue, 0, unroll=4)

    for kslot in range(TOP_K):
        pltpu.make_async_copy(ys_ref.at[pl.ds(0, tm)], buf.at[kslot], sem.at[0]).wait()

    gw = gw_ref[...]
    half = buf.shape[2]
    acc_lo = jnp.zeros((tm, half), F32)
    acc_hi = jnp.zeros((tm, half), F32)
    for kslot in range(TOP_K):
        lo, hi = _unpack_rows(buf[kslot])
        acc_lo = acc_lo + gw[:, kslot:kslot + 1] * lo
        acc_hi = acc_hi + gw[:, kslot:kslot + 1] * hi
    o_ref[:, 0:half] = x_ref[:, 0:half] + g2_ref[:, 0:half] * acc_lo
    o_ref[:, half:2 * half] = x_ref[:, half:2 * half] + g2_ref[:, half:2 * half] * acc_hi


def combine(pos_flat, x, gw, g2_row, ys, tm=TOK_BLOCK):
    s, d = x.shape
    return pl.pallas_call(
        _combine_body,
        grid=(s // tm,),
        in_specs=[pl.BlockSpec((tm * TOP_K,), lambda i: (i,), memory_space=pltpu.SMEM),
                  pl.BlockSpec((tm, d), lambda i: (i, 0)),
                  pl.BlockSpec((tm, LANES), lambda i: (i, 0)),
                  pl.BlockSpec((1, d), lambda i: (0, 0)),
                  pl.BlockSpec(memory_space=pl.ANY)],
        out_specs=pl.BlockSpec((tm, d), lambda i: (i, 0)),
        out_shape=jax.ShapeDtypeStruct((s, d), F32),
        scratch_shapes=[pltpu.VMEM((TOP_K, tm, d // 2), jnp.uint32), pltpu.SemaphoreType.DMA((1,))],
        compiler_params=_params("arbitrary"),
        name="moe_combine",
    )(pos_flat, x, gw, g2_row.reshape(1, d), ys)


def moe_layer(x, gain, shift, scale, g2_row, router_w, router_b, w_gu, b_gu, w_dn, b_dn, layer):
    s, d = x.shape
    e = router_w.shape[1]
    hf, logits = router(x, gain, shift, scale, router_w, router_b)
    eidx, gw, rk, counts = topk_route(logits)
    cnt = counts[0, :e].astype(jnp.int32)
    padded = (cnt + MOE_ROWS - 1) // MOE_ROWS * MOE_ROWS
    padded_end = jnp.cumsum(padded)
    padded_start = jnp.pad((padded_end - padded).astype(F32), (0, LANES - e)).reshape(1, LANES)
    n_rows = s * TOP_K + e * MOE_ROWS
    blk_first_row = jnp.arange(n_rows // MOE_ROWS, dtype=jnp.int32) * MOE_ROWS
    blk_expert = jnp.minimum(
        jnp.sum((padded_end[None, :] <= blk_first_row[:, None]).astype(jnp.int32), axis=1), e - 1)
    total_rows = padded_end[-1:].astype(jnp.int32)
    n_blk = n_rows // MOE_ROWS
    blk_id = jnp.arange(n_blk, dtype=jnp.int32)
    prev_expert = jnp.concatenate([jnp.full((1,), -1, jnp.int32), blk_expert[:-1]])
    blk_first = ((blk_first_row < total_rows[0]) & (blk_expert != prev_expert)).astype(jnp.int32)
    first_at = jnp.where(blk_first == 1, blk_id, n_blk)
    next_first = jnp.concatenate([lax.cummin(first_at, reverse=True)[1:], jnp.full((1,), n_blk, jnp.int32)])
    blk_next = jnp.where(next_first < n_blk, blk_expert[jnp.minimum(next_first, n_blk - 1)], -1).astype(jnp.int32)
    pos = slot_positions(eidx, rk, padded_start)
    pos_flat = pos[:, :TOP_K].reshape(-1)
    xs = dispatch(pos_flat, hf, n_rows)
    ys = expert_ffn(blk_expert, blk_first, blk_next, total_rows, xs, w_gu, b_gu, w_dn, b_dn, layer)
    return combine(pos_flat, x, gw, g2_row, ys)


def kernel(x, c, positions, norm_gain, ada_w, ada_b, mlstm_w_in, mlstm_b_gates, mlstm_norm_gain, mlstm_w_out, rwkv_mu, rwkv_w_rkv, rwkv_w0, rwkv_w1, rwkv_w2, rwkv_a0, rwkv_a1, rwkv_a2, rwkv_g1, rwkv_g2, rwkv_k_k, rwkv_k_a, rwkv_r_k, rwkv_ln_gain, rwkv_ln_bias, rwkv_w_o, diff_w_qkv, diff_lambda, diff_subln_gain, diff_w_o, moe_router_w, moe_router_b, moe_w_gate_up, moe_b_gate_up, moe_w_down, moe_b_down, final_gain):
    b, s, d = x.shape
    assert b == 1
    depth = norm_gain.shape[0]
    xt = x.reshape(s, d)
    pos = positions.reshape(s)
    mod = ada_mod(c, ada_w, ada_b).reshape(depth, 6, d)
    seen = [0, 0, 0]
    for layer in range(depth):
        kind = layer % 3
        j = seen[kind]
        seen[kind] += 1
        sh1, sc1, g1, sh2, sc2, g2 = (mod[layer, i] for i in range(6))
        gain = norm_gain[layer, 0]
        if kind == 0:
            xt = mlstm_mixer(xt, gain, sh1, sc1, mlstm_w_in[j], mlstm_b_gates[j], mlstm_norm_gain[j],
                             mlstm_w_out[j], g1)
        elif kind == 1:
            xt = rwkv_mixer(xt, gain, sh1, sc1, rwkv_mu[j], rwkv_w_rkv[j], rwkv_w0[j], rwkv_w1[j], rwkv_w2[j],
                            rwkv_a0[j], rwkv_a1[j], rwkv_a2[j], rwkv_g1[j], rwkv_g2[j], rwkv_k_k[j],
                            rwkv_k_a[j], rwkv_r_k[j], rwkv_ln_gain[j], rwkv_ln_bias[j], rwkv_w_o[j], g1)
        else:
            xt = diff_mixer(xt, gain, sh1, sc1, pos, diff_w_qkv[j], diff_lambda[j], diff_subln_gain[j],
                            diff_w_o[j], layer, g1)
        xt = moe_layer(xt, norm_gain[layer, 1], sh2, sc2, g2, moe_router_w[layer], moe_router_b[layer],
                       moe_w_gate_up, moe_b_gate_up[layer], moe_w_down, moe_b_down[layer], layer)
    zeros = jnp.zeros((d,), F32)
    return norm_mod(xt, final_gain, zeros, zeros, F32).reshape(b, s, d)
```

```python
import functools
import math

import numpy as np
import jax
import jax.numpy as jnp
from jax import lax
from jax.experimental import pallas as pl
from jax.experimental.pallas import tpu as pltpu

F32 = jnp.float32
BF16 = jnp.bfloat16
HIGHEST = lax.Precision.HIGHEST

LANES = 128
VMEM_LIMIT = 52 * 1024 * 1024

NORM_EPS = 1e-6
MLSTM_HEADS = 4
MLSTM_CHUNK = 128
GATE_SOFTCAP = 15.0
RWKV_HEAD = 64
RWKV_GN_EPS = 64e-5
DIFF_HEADS = 8
DIFF_DQK = 128
DIFF_DV = 256
DIFF_SUBLN_EPS = 1e-5
ROPE_DIM = 32
ROPE_THETA = 500000.0
N_EXPERTS = 32
TOP_K = 4
SWIGLU_LIMIT = 7.0
SWIGLU_ALPHA = 1.702
MOE_ROWS = 256
TOK_BLOCK = 256


def _params(*sem):
    return pltpu.CompilerParams(dimension_semantics=sem, vmem_limit_bytes=VMEM_LIMIT)


def _bdot(a, b):
    return jnp.dot(a.astype(BF16), b.astype(BF16), preferred_element_type=F32)


def _dot_nt(a, b):
    return lax.dot_general(a.astype(BF16), b.astype(BF16), (((1,), (1,)), ((), ())),
                           preferred_element_type=F32)


def _dot_tn(a, b):
    return lax.dot_general(a.astype(BF16), b.astype(BF16), (((0,), (0,)), ((), ())),
                           preferred_element_type=F32)


def _ada_body(c_ref, w_ref, b_ref, o_ref, *, tn):
    kb, n = w_ref.shape[1], w_ref.shape[2]

    @pl.when(pl.program_id(1) == 0)
    def _():
        o_ref[0] = b_ref[0]

    cv = c_ref[...]
    sc = cv * jax.nn.sigmoid(cv)
    for j in range(n // tn):
        acc = jnp.zeros((8, tn), F32)
        for g in range(kb // 8):
            acc = acc + sc[g * 8:(g + 1) * 8] * w_ref[0, g * 8:(g + 1) * 8, j * tn:(j + 1) * tn]
        o_ref[0, :, j * tn:(j + 1) * tn] += jnp.sum(acc, axis=0, keepdims=True)


def ada_mod(c, ada_w, ada_b, kb=256, tn=2048):
    depth, d, n = ada_w.shape
    return pl.pallas_call(
        functools.partial(_ada_body, tn=tn),
        grid=(depth, d // kb),
        in_specs=[pl.BlockSpec((kb, 1), lambda l, k: (k, 0)),
                  pl.BlockSpec((1, kb, n), lambda l, k: (l, k, 0)),
                  pl.BlockSpec((1, 1, n), lambda l, k: (l, 0, 0))],
        out_specs=pl.BlockSpec((1, 1, n), lambda l, k: (l, 0, 0)),
        out_shape=jax.ShapeDtypeStruct((depth, 1, n), F32),
        compiler_params=_params("arbitrary", "arbitrary"),
        name="ada_mod",
    )(c.reshape(d, 1), ada_w, ada_b.reshape(depth, 1, n))


def _norm_rows(x, gain, shift, scale, eps):
    y = x * lax.rsqrt(jnp.mean(x * x, axis=-1, keepdims=True) + eps)
    return (y * gain) * (1.0 + scale) + shift


def _norm_mod_body(x_ref, g_ref, sh_ref, sc_ref, o_ref):
    o_ref[...] = _norm_rows(x_ref[...], g_ref[...], sh_ref[...], sc_ref[...], NORM_EPS).astype(o_ref.dtype)


def norm_mod(x, gain, shift, scale, out_dtype, tm=256):
    s, d = x.shape
    row = pl.BlockSpec((1, d), lambda i: (0, 0))
    return pl.pallas_call(
        _norm_mod_body,
        grid=(s // tm,),
        in_specs=[pl.BlockSpec((tm, d), lambda i: (i, 0)), row, row, row],
        out_specs=pl.BlockSpec((tm, d), lambda i: (i, 0)),
        out_shape=jax.ShapeDtypeStruct((s, d), out_dtype),
        compiler_params=_params("arbitrary"),
        name="norm_mod",
    )(x, gain.reshape(1, d), shift.reshape(1, d), scale.reshape(1, d))


def _mm_body(*refs, fused):
    if fused:
        x_ref, w_ref, res_ref, gate_ref, o_ref, wb_ref = refs
    else:
        x_ref, w_ref, o_ref, wb_ref = refs

    @pl.when(pl.program_id(1) == 0)
    def _():
        wb_ref[...] = w_ref[...].astype(BF16)

    acc = jnp.dot(x_ref[...].astype(BF16), wb_ref[...], preferred_element_type=F32)
    if fused:
        acc = res_ref[...] + gate_ref[...] * acc
    o_ref[...] = acc.astype(o_ref.dtype)


def matmul(x, w, out_dtype=F32, res=None, gate=None, n_cols=None, tm=512, tn=1024):
    m, k = x.shape
    n = w.shape[1] if n_cols is None else n_cols
    tn = min(tn, n)
    tm = min(tm, m)
    fused = res is not None
    in_specs = [pl.BlockSpec((tm, k), lambda j, i: (i, 0)),
                pl.BlockSpec((k, tn), lambda j, i: (0, j))]
    args = [x, w]
    if fused:
        in_specs += [pl.BlockSpec((tm, tn), lambda j, i: (i, j)),
                     pl.BlockSpec((1, tn), lambda j, i: (0, j))]
        args += [res, gate.reshape(1, n)]
    return pl.pallas_call(
        functools.partial(_mm_body, fused=fused),
        grid=(n // tn, m // tm),
        in_specs=in_specs,
        out_specs=pl.BlockSpec((tm, tn), lambda j, i: (i, j)),
        out_shape=jax.ShapeDtypeStruct((m, n), out_dtype),
        scratch_shapes=[pltpu.VMEM((k, tn), BF16)],
        compiler_params=_params("arbitrary", "arbitrary"),
        name="matmul",
    )(*args)


def _mlstm_body(proj_ref, gt_ref, bg_ref, ng_ref, out_ref, c_s, n_s, m_s, *, dk, dv):
    H = MLSTM_HEADS

    @pl.when(pl.program_id(0) == 0)
    def _():
        c_s[...] = jnp.zeros_like(c_s)
        n_s[...] = jnp.zeros_like(n_s)
        m_s[...] = jnp.zeros_like(m_s)

    for h in range(H):
        _mlstm_head(h,
                    proj_ref.at[:, pl.ds(h * dk, dk)],
                    proj_ref.at[:, pl.ds(H * dk + h * dk, dk)],
                    proj_ref.at[:, pl.ds(2 * H * dk + h * dv, dv)],
                    proj_ref.at[:, pl.ds(2 * H * dk + H * dv + h * dv, dv)],
                    gt_ref, bg_ref, ng_ref.at[:, pl.ds(h * dv, dv)], out_ref.at[:, pl.ds(h * dv, dv)],
                    c_s.at[h], n_s.at[h], m_s.at[h])


def _mlstm_head(h, q_ref, k_ref, v_ref, o_ref, gt_ref, bg_ref, ng_ref, out_ref, c_s, n_s, m_s):
    L = q_ref.shape[0]
    dk = q_ref.shape[1]
    dv = v_ref.shape[1]

    lane = lax.broadcasted_iota(jnp.int32, (L, LANES), 1)
    row = lax.broadcasted_iota(jnp.int32, (L, L), 0)
    col = lax.broadcasted_iota(jnp.int32, (L, L), 1)

    capped = GATE_SOFTCAP * jnp.tanh((gt_ref[...] + bg_ref[...]) / GATE_SOFTCAP)
    li = jnp.sum(jnp.where(lane == h, capped, 0.0), axis=-1, keepdims=True)
    fpre = jnp.sum(jnp.where(lane == h + MLSTM_HEADS, capped, 0.0), axis=-1, keepdims=True)
    lf = jax.nn.log_sigmoid(fpre)

    tril = (row >= col).astype(F32)
    g_b = jnp.dot(tril, jnp.broadcast_to(lf, (L, L)), precision=HIGHEST,
                  preferred_element_type=F32)
    g = g_b[:, 0:1]
    g_tot = jnp.sum(lf, axis=0, keepdims=True)
    eye = row == col
    g_row = jnp.sum(jnp.where(eye, g_b, 0.0), axis=0, keepdims=True)
    li_row = jnp.sum(jnp.where(eye, jnp.broadcast_to(li, (L, L)), 0.0), axis=0, keepdims=True)

    m_prev = m_s[...]
    c_prev = c_s[...]
    n_prev = n_s[...]

    q = q_ref[...] * (dk ** -0.5)
    k = k_ref[...]
    vb = v_ref[...].astype(BF16)
    qb = q.astype(BF16)

    d_log = jnp.where(row >= col, g_b - g_row + li_row, -jnp.inf)
    inter_log = g + m_prev
    m_t = jnp.maximum(inter_log, jnp.max(d_log, axis=-1, keepdims=True))
    s = _dot_nt(qb, k) * jnp.exp(d_log - m_t)
    inter = jnp.exp(inter_log - m_t)
    num = _bdot(s, vb) + inter * _bdot(qb, c_prev)
    den = jnp.sum(s, axis=-1, keepdims=True) + inter * jnp.sum(q * n_prev, axis=-1, keepdims=True)
    hh = num / jnp.maximum(jnp.abs(den), jnp.exp(-m_t))
    hh = hh * lax.rsqrt(jnp.mean(hh * hh, axis=-1, keepdims=True) + NORM_EPS) * ng_ref[...]
    out_ref[...] = (jax.nn.sigmoid(o_ref[...]) * hh).astype(out_ref.dtype)

    w_state = g_tot - g + li
    m_loc = jnp.max(w_state, axis=0, keepdims=True)
    ks = jnp.exp(w_state - m_loc) * k
    kv = _dot_tn(ks, vb)
    n_c = jnp.sum(ks, axis=0, keepdims=True)
    m_new = jnp.maximum(g_tot + m_prev, m_loc)
    a = jnp.exp(g_tot + m_prev - m_new)
    b = jnp.exp(m_loc - m_new)
    c_s[...] = a * c_prev + b * kv
    n_s[...] = a * n_prev + b * n_c
    m_s[...] = m_new


def mlstm_cells(proj, gates, b_gates_pad, norm_gain):
    s = proj.shape[0]
    d = norm_gain.shape[0]
    H, L = MLSTM_HEADS, MLSTM_CHUNK
    dk = (proj.shape[1] - 2 * d) // (2 * H)
    dv = d // H
    n = proj.shape[1]
    return pl.pallas_call(
        functools.partial(_mlstm_body, dk=dk, dv=dv),
        grid=(s // L,),
        in_specs=[pl.BlockSpec((L, n), lambda c: (c, 0)),
                  pl.BlockSpec((L, LANES), lambda c: (c, 0)),
                  pl.BlockSpec((1, LANES), lambda c: (0, 0)),
                  pl.BlockSpec((1, d), lambda c: (0, 0))],
        out_specs=pl.BlockSpec((L, d), lambda c: (c, 0)),
        out_shape=jax.ShapeDtypeStruct((s, d), BF16),
        scratch_shapes=[pltpu.VMEM((H, dk, dv), F32), pltpu.VMEM((H, 1, dk), F32), pltpu.VMEM((H, 1, 1), F32)],
        compiler_params=_params("arbitrary"),
        name="mlstm_cells",
    )(proj, gates, b_gates_pad, norm_gain.reshape(1, d))


def mlstm_mixer(x, gain, shift, scale, w_in, b_gates, norm_gain, w_out, gate_row):
    d = x.shape[1]
    n_main = w_in.shape[1] - 2 * MLSTM_HEADS
    hm = norm_mod(x, gain, shift, scale, BF16)
    proj = matmul(hm, w_in, n_cols=n_main)
    w_g = jnp.pad(w_in[:, n_main:], ((0, 0), (0, LANES - 2 * MLSTM_HEADS)))
    gates = matmul(hm, w_g)
    b_pad = jnp.pad(b_gates, (0, LANES - 2 * MLSTM_HEADS)).reshape(1, LANES)
    cells = mlstm_cells(proj, gates, b_pad, norm_gain)
    return matmul(cells, w_out, res=x, gate=gate_row)


def _rwkv_prep_body(x_ref, xp_ref, g_ref, sh_ref, sc_ref, mu_ref, *outs):
    tm = x_ref.shape[0]
    hm = _norm_rows(x_ref[...], g_ref[...], sh_ref[...], sc_ref[...], NORM_EPS)
    prev = _norm_rows(xp_ref[7:8, :], g_ref[...], sh_ref[...], sc_ref[...], NORM_EPS)
    prev = jnp.where(pl.program_id(0) == 0, 0.0, prev)
    rows = lax.broadcasted_iota(jnp.int32, hm.shape, 0)
    shifted = jnp.where(rows == 0, prev, pltpu.roll(hm, 1, axis=0))
    xx = shifted - hm
    for j, o_ref in enumerate(outs):
        o_ref[...] = (hm + xx * mu_ref[j:j + 1, :]).astype(o_ref.dtype)


def rwkv_prep(x, gain, shift, scale, mu, tm=256):
    s, d = x.shape
    row = pl.BlockSpec((1, d), lambda i: (0, 0))
    blk = pl.BlockSpec((tm, d), lambda i: (i, 0))
    return pl.pallas_call(
        _rwkv_prep_body,
        grid=(s // tm,),
        in_specs=[blk, pl.BlockSpec((8, d), lambda i: (jnp.maximum(i * (tm // 8) - 1, 0), 0)),
                  row, row, row, pl.BlockSpec((8, d), lambda i: (0, 0))],
        out_specs=[blk] * 6,
        out_shape=[jax.ShapeDtypeStruct((s, d), BF16)] * 6,
        compiler_params=_params("arbitrary"),
        name="rwkv_prep",
    )(x, x, gain.reshape(1, d), shift.reshape(1, d), scale.reshape(1, d), jnp.pad(mu, ((0, 2), (0, 0))))


def _lora_body(x_ref, w1_ref, w2_ref, o_ref, *, act):
    t = _bdot(x_ref[...], w1_ref[...])
    if act == "tanh":
        t = jnp.tanh(t)
    elif act == "sigmoid":
        t = jax.nn.sigmoid(t)
    o_ref[...] = _bdot(t, w2_ref[...])


def lora(x, w1, w2, act, tm=512):
    s, d = x.shape
    r = w1.shape[1]
    rp = -(-r // LANES) * LANES
    w1p = jnp.pad(w1, ((0, 0), (0, rp - r)))
    w2p = jnp.pad(w2, ((0, rp - r), (0, 0)))
    n = w2.shape[1]
    return pl.pallas_call(
        functools.partial(_lora_body, act=act),
        grid=(s // tm,),
        in_specs=[pl.BlockSpec((tm, d), lambda i: (i, 0)),
                  pl.BlockSpec((d, rp), lambda i: (0, 0)),
                  pl.BlockSpec((rp, n), lambda i: (0, 0))],
        out_specs=pl.BlockSpec((tm, n), lambda i: (i, 0)),
        out_shape=jax.ShapeDtypeStruct((s, n), F32),
        compiler_params=_params("arbitrary"),
        name="lora",
    )(x, w1p, w2p)


def _split3(x):
    hi = x.astype(BF16)
    r1 = x - hi.astype(F32)
    mid = r1.astype(BF16)
    lo = (r1 - mid.astype(F32)).astype(BF16)
    return hi, mid, lo


def _segsum(x, bmat, passes):
    parts = _split3(x)[:passes] if passes > 1 else (x.astype(BF16),)
    acc = jnp.dot(parts[0], bmat, preferred_element_type=F32)
    for p in parts[1:]:
        acc = acc + jnp.dot(p, bmat, preferred_element_type=F32)
    return acc


SUBLANES = 8


SCAN_W = 256


def _head_ones(width):
    ri = lax.broadcasted_iota(jnp.int32, (width, width), 0)
    ci = lax.broadcasted_iota(jnp.int32, (width, width), 1)
    return ((ri // RWKV_HEAD) == (ci // RWKV_HEAD)).astype(BF16)


def _rwkv_pre_body(k_ref, wl_ref, al_ref, w0_ref, a0_ref, kk_ref, ka_ref, nkk_ref, w_ref, kka_ref, kp_ref):
    d = k_ref.shape[1]
    bmat = _head_ones(SCAN_W)
    for j in range(d // SCAN_W):
        cs = slice(j * SCAN_W, (j + 1) * SCAN_W)
        kraw = k_ref[:, cs]
        w_log = -jax.nn.softplus(-(w0_ref[:, cs] + wl_ref[:, cs])) - 0.5
        w_ref[:, cs] = -jnp.exp(w_log)
        a = jax.nn.sigmoid(a0_ref[:, cs] + al_ref[:, cs])
        kk = kraw * kk_ref[:, cs]
        nrm = jnp.sqrt(_segsum(kk * kk, bmat, 3))
        kk = kk / jnp.maximum(nrm, 1e-12)
        nkk_ref[:, cs] = -kk
        kka_ref[:, cs] = kk * a
        kp_ref[:, cs] = kraw * (1.0 + (a - 1.0) * ka_ref[:, cs])


def rwkv_pre(k, wl, al, w0, a0, k_k, k_a, tm=256):
    s, d = k.shape
    blk = pl.BlockSpec((tm, d), lambda i: (i, 0))
    row = pl.BlockSpec((1, d), lambda i: (0, 0))
    return pl.pallas_call(
        _rwkv_pre_body,
        grid=(s // tm,),
        in_specs=[blk] * 3 + [row] * 4,
        out_specs=[blk] * 4,
        out_shape=[jax.ShapeDtypeStruct((s, d), F32)] * 4,
        compiler_params=_params("arbitrary"),
        name="rwkv_pre",
    )(k, wl, al, *[t.reshape(1, d) for t in (w0, a0, k_k, k_a)])


def _rwkv_post_body(y_ref, r_ref, kp_ref, v_ref, gt_ref, rk_ref, lg_ref, lb_ref, o_ref):
    d = y_ref.shape[1]
    bmat = _head_ones(SCAN_W)
    inv = 1.0 / RWKV_HEAD
    for j in range(d // SCAN_W):
        cs = slice(j * SCAN_W, (j + 1) * SCAN_W)
        y = y_ref[:, cs]
        mean = _segsum(y, bmat, 3) * inv
        yc = y - mean
        var = _segsum(yc * yc, bmat, 3) * inv
        yn = yc * lax.rsqrt(var + RWKV_GN_EPS) * lg_ref[:, cs] + lb_ref[:, cs]
        bonus = _segsum(r_ref[:, cs] * kp_ref[:, cs] * rk_ref[:, cs], bmat, 3) * v_ref[:, cs]
        o_ref[:, cs] = ((yn + bonus) * gt_ref[:, cs]).astype(o_ref.dtype)


def rwkv_post(y, r, kp, v, gt, r_k, ln_gain, ln_bias, tm=256):
    s, d = y.shape
    blk = pl.BlockSpec((tm, d), lambda i: (i, 0))
    row = pl.BlockSpec((1, d), lambda i: (0, 0))
    return pl.pallas_call(
        _rwkv_post_body,
        grid=(s // tm,),
        in_specs=[blk] * 5 + [row] * 3,
        out_specs=blk,
        out_shape=jax.ShapeDtypeStruct((s, d), BF16),
        compiler_params=_params("arbitrary"),
        name="rwkv_post",
    )(y, r, kp, v, gt, *[t.reshape(1, d) for t in (r_k, ln_gain, ln_bias)])


RWKV_CHUNK = 32


def _rwkv_chunk_body(a_ref, lw_ref, b_ref, kp_ref, r_ref, v_ref, y_ref, st_s):
    L, d = r_ref.shape
    nv = RWKV_HEAD
    W = SCAN_W
    hpt = W // nv
    ngrp = L // SUBLANES

    @pl.when(pl.program_id(0) == 0)
    def _():
        st_s[...] = jnp.zeros_like(st_s)

    lane_head = lax.broadcasted_iota(jnp.int32, (1, W), 1) // nv
    heads = [lane_head == h for h in range(hpt)]
    ti = lax.broadcasted_iota(jnp.int32, (L, L), 0)
    si = lax.broadcasted_iota(jnp.int32, (L, L), 1)
    tril = (ti >= si).astype(F32)
    t_row = lax.broadcasted_iota(jnp.int32, (2 * L, 2 * hpt * L), 0)
    s_lane = lax.broadcasted_iota(jnp.int32, (2 * L, 2 * hpt * L), 1) % L
    g_mask = s_lane <= jnp.where(t_row < L, t_row - 1, t_row - L)
    sub = lax.broadcasted_iota(jnp.int32, (SUBLANES, W), 0)
    ones = _head_ones(W)

    def stack(x):
        return jnp.concatenate([jnp.where(hm, x, 0.0) for hm in heads], axis=0)

    for j in range(d // W):
        cs = slice(j * W, (j + 1) * W)
        rs = slice(j * nv, (j + 1) * nv)
        lw = lw_ref[:, cs]
        cum = jnp.dot(tril, lw, precision=HIGHEST, preferred_element_type=F32)
        tot = cum[L - 1:L, :]
        w_inv = jnp.exp(-cum)
        w_rest = jnp.exp(tot - cum)
        a_hat = a_ref[:, cs] * jnp.exp(cum - lw)
        r_hat = r_ref[:, cs] * jnp.exp(cum)
        k_hat = kp_ref[:, cs] * w_inv
        b_hat = b_ref[:, cs] * w_inv
        v = v_ref[:, cs]
        s0 = st_s[rs, :]

        ar = jnp.concatenate([a_hat, r_hat], axis=0).astype(BF16)
        uy0 = _dot_nt(ar, stack(s0))
        g = _dot_nt(ar, jnp.concatenate([stack(k_hat), stack(b_hat)], axis=0))
        g = jnp.where(g_mask, g, 0.0)
        v_st = stack(v)
        base = uy0[0:L] + _bdot(g[0:L, 0:hpt * L], v_st)

        need = [(s, g8) for s in range(L - 1) for g8 in range(s // SUBLANES, ngrp)]
        parts = [a_hat[g8 * SUBLANES:(g8 + 1) * SUBLANES] * b_hat[s:s + 1, :] for s, g8 in need]
        if len(parts) % 2:
            parts.append(jnp.zeros((SUBLANES, W), F32))
        lhs = jnp.concatenate(parts, axis=0)
        coef = jnp.dot(lhs.astype(BF16), ones, preferred_element_type=F32)
        at = {sg: n * SUBLANES for n, sg in enumerate(need)}
        u = [base[g8 * SUBLANES:(g8 + 1) * SUBLANES] for g8 in range(ngrp)]
        for s in range(L - 1):
            g0, i0 = divmod(s, SUBLANES)
            us = jnp.broadcast_to(u[g0][i0:i0 + 1, :], (SUBLANES, W))
            for g8 in range(g0, ngrp):
                c = coef[at[s, g8]:at[s, g8] + SUBLANES]
                if g8 == g0:
                    c = jnp.where(sub > i0, c, 0.0)
                u[g8] = u[g8] + c * us
        u = jnp.concatenate(u, axis=0)

        vu = jnp.concatenate([v_st, stack(u)], axis=0)
        y_ref[:, cs] = uy0[L:2 * L] + _bdot(g[L:2 * L], vu)

        kb_rest = jnp.concatenate([kp_ref[:, cs] * w_rest, b_ref[:, cs] * w_rest], axis=0)
        upd = _dot_tn(jnp.concatenate([v, u], axis=0), kb_rest)
        new = s0 * jnp.exp(tot)
        for h, hm in enumerate(heads):
            new = new + jnp.where(hm, upd[h * nv:(h + 1) * nv], 0.0)
        st_s[rs, :] = new


def rwkv_chunked(nkk, lw, kka, kp, r, v):
    s, d = r.shape
    L = RWKV_CHUNK
    blk = pl.BlockSpec((L, d), lambda i: (i, 0))
    return pl.pallas_call(
        _rwkv_chunk_body,
        grid=(s // L,),
        in_specs=[blk] * 6,
        out_specs=blk,
        out_shape=jax.ShapeDtypeStruct((s, d), F32),
        scratch_shapes=[pltpu.VMEM(((d // SCAN_W) * RWKV_HEAD, SCAN_W), F32)],
        compiler_params=_params("arbitrary"),
        name="rwkv_chunked",
    )(nkk, lw, kka, kp, r, v)


def rwkv_mixer(x, gain, shift, scale, mu, w_rkv, w0, w1, w2, a0, a1, a2, g1, g2, k_k, k_a, r_k,
               ln_gain, ln_bias, w_o, gate_row):
    xr, xk, xv, xw, xa, xg = rwkv_prep(x, gain, shift, scale, mu)
    r = matmul(xr, w_rkv[0])
    k = matmul(xk, w_rkv[1])
    v = matmul(xv, w_rkv[2])
    wl = lora(xw, w1, w2, "tanh")
    al = lora(xa, a1, a2, "none")
    gt = lora(xg, g1, g2, "sigmoid")
    nkk, lw, kka, kp = rwkv_pre(k, wl, al, w0, a0, k_k, k_a)
    y = rwkv_chunked(nkk, lw, kka, kp, r, v)
    y = rwkv_post(y, r, kp, v, gt, r_k.reshape(-1), ln_gain, ln_bias)
    return matmul(y, w_o, res=x, gate=gate_row)


def _rope_body(p_ref, pos_ref, o_ref, *, n_rot_tiles, q_tiles):
    tm, n = p_ref.shape
    half = ROPE_DIM // 2
    lane = lax.broadcasted_iota(jnp.int32, (tm, LANES), 1)
    inv_freq = jnp.exp((lane % half).astype(F32) * (-2.0 * math.log(ROPE_THETA) / ROPE_DIM))
    ang = pos_ref[...].astype(F32) * inv_freq
    roped = lane < ROPE_DIM
    cos = jnp.where(roped, jnp.cos(ang), 1.0)
    sin = jnp.where(roped, jnp.where(lane < half, -jnp.sin(ang), jnp.sin(ang)), 0.0)
    for j in range(n // LANES):
        x = p_ref[:, j * LANES:(j + 1) * LANES]
        if j < n_rot_tiles:
            partner = jnp.where(lane < half, pltpu.roll(x, LANES - half, axis=1), pltpu.roll(x, half, axis=1))
            x = x * cos + partner * sin
            if j < q_tiles:
                x = x * (DIFF_DQK ** -0.5)
        o_ref[:, j * LANES:(j + 1) * LANES] = x.astype(o_ref.dtype)


def rope_cast(proj, positions, tm=256):
    s, n = proj.shape
    q_tiles = 2 * DIFF_HEADS
    return pl.pallas_call(
        functools.partial(_rope_body, n_rot_tiles=2 * q_tiles, q_tiles=q_tiles),
        grid=(s // tm,),
        in_specs=[pl.BlockSpec((tm, n), lambda i: (i, 0)),
                  pl.BlockSpec((tm, 1), lambda i: (i, 0))],
        out_specs=pl.BlockSpec((tm, n), lambda i: (i, 0)),
        out_shape=jax.ShapeDtypeStruct((s, n), BF16),
        compiler_params=_params("arbitrary"),
        name="rope_cast",
    )(proj, positions.reshape(s, 1))


def _diff_attn_body(qi_ref, kj_ref, q_ref, k_ref, v_ref, lam_ref, sg_ref, o_ref, m_s, l_s, acc_s, *, lambda_init):
    step = pl.program_id(1)
    qi, kj = qi_ref[step], kj_ref[step]
    tb = q_ref.shape[0]

    @pl.when(kj == 0)
    def _():
        m_s[...] = jnp.full_like(m_s, -jnp.inf)
        l_s[...] = jnp.zeros_like(l_s)
        acc_s[...] = jnp.zeros_like(acc_s)

    def accumulate(diagonal):
        v = v_ref[...]
        for c in range(2):
            cs = slice(c * DIFF_DQK, (c + 1) * DIFF_DQK)
            s = _dot_nt(q_ref[:, cs], k_ref[:, cs])
            if diagonal:
                rows = lax.broadcasted_iota(jnp.int32, (tb, tb), 0)
                cols = lax.broadcasted_iota(jnp.int32, (tb, tb), 1)
                s = jnp.where(cols <= rows, s, -jnp.inf)
            m_old = m_s[c]
            m_new = jnp.maximum(m_old, jnp.max(s, axis=-1, keepdims=True))
            alpha = jnp.exp(m_old - m_new)
            p = jnp.exp(s - m_new)
            l_s[c] = alpha * l_s[c] + jnp.sum(p, axis=-1, keepdims=True)
            acc_s[c] = alpha * acc_s[c] + _bdot(p, v)
            m_s[c] = m_new

    @pl.when(kj < qi)
    def _():
        accumulate(False)

    @pl.when(kj == qi)
    def _():
        accumulate(True)
        lp = lam_ref[...]
        lam = (jnp.exp(jnp.sum(lp[0:1] * lp[1:2], axis=-1, keepdims=True))
               - jnp.exp(jnp.sum(lp[2:3] * lp[3:4], axis=-1, keepdims=True)) + lambda_init)
        out = acc_s[0] / l_s[0] - lam * (acc_s[1] / l_s[1])
        out = out * lax.rsqrt(jnp.mean(out * out, axis=-1, keepdims=True) + DIFF_SUBLN_EPS)
        o_ref[...] = (out * sg_ref[...] * (1.0 - lambda_init)).astype(o_ref.dtype)


def diff_attention(qkv, lam_params, subln_gain, lambda_init, tb=1024):
    s = qkv.shape[0]
    H, dv = DIFF_HEADS, DIFF_DV
    tb = min(tb, s)
    nb = s // tb
    pairs = [(qi, kj) for qi in range(nb) for kj in range(qi + 1)]
    qi_tab = jnp.asarray([p[0] for p in pairs], jnp.int32)
    kj_tab = jnp.asarray([p[1] for p in pairs], jnp.int32)
    grid_spec = pltpu.PrefetchScalarGridSpec(
        num_scalar_prefetch=2,
        grid=(H, len(pairs)),
        in_specs=[pl.BlockSpec((tb, dv), lambda h, t, qi, kj: (qi[t], h)),
                  pl.BlockSpec((tb, dv), lambda h, t, qi, kj: (kj[t], H + h)),
                  pl.BlockSpec((tb, dv), lambda h, t, qi, kj: (kj[t], 2 * H + h)),
                  pl.BlockSpec((4, DIFF_DQK), lambda h, t, qi, kj: (0, 0)),
                  pl.BlockSpec((1, dv), lambda h, t, qi, kj: (0, 0))],
        out_specs=pl.BlockSpec((tb, dv), lambda h, t, qi, kj: (qi[t], h)),
        scratch_shapes=[pltpu.VMEM((2, tb, 1), F32), pltpu.VMEM((2, tb, 1), F32),
                        pltpu.VMEM((2, tb, dv), F32)],
    )
    return pl.pallas_call(
        functools.partial(_diff_attn_body, lambda_init=lambda_init),
        grid_spec=grid_spec,
        out_shape=jax.ShapeDtypeStruct((s, H * dv), BF16),
        compiler_params=_params("arbitrary", "arbitrary"),
        name="diff_attention",
    )(qi_tab, kj_tab, qkv, qkv, qkv, lam_params, subln_gain.reshape(1, dv))


def diff_mixer(x, gain, shift, scale, positions, w_qkv, lam_params, subln_gain, w_o, layer, gate_row):
    hm = norm_mod(x, gain, shift, scale, BF16)
    proj = matmul(hm, w_qkv)
    qkv = rope_cast(proj, positions)
    lambda_init = 0.8 - 0.6 * math.exp(-0.3 * layer)
    att = diff_attention(qkv, lam_params, subln_gain, lambda_init)
    return matmul(att, w_o, res=x, gate=gate_row)


HI16 = 0xFFFF0000


def _pack_rows(x):
    m = x.shape[1] // 2
    bits = lax.bitcast_convert_type(x.astype(BF16).astype(F32), jnp.uint32)
    return (bits[:, :m] >> 16) | (bits[:, m:] & jnp.uint32(HI16))


def _unpack_rows(p):
    lo = lax.bitcast_convert_type(p << 16, F32)
    hi = lax.bitcast_convert_type(p & jnp.uint32(HI16), F32)
    return lo, hi


def _router_body(x_ref, g_ref, sh_ref, sc_ref, rw_ref, rb_ref, hf_ref, eidx_ref, gw_ref, rk_ref, cnt_ref):
    hf = _norm_rows(x_ref[...], g_ref[...], sh_ref[...], sc_ref[...], NORM_EPS)
    hf_ref[...] = _pack_rows(hf)
    rw = rw_ref[...]
    xh = hf.astype(BF16)
    xl = (hf - xh.astype(F32)).astype(BF16)
    wh = rw.astype(BF16)
    wl = (rw - wh.astype(F32)).astype(BF16)
    logits = (jnp.dot(xh, wh, preferred_element_type=F32) + jnp.dot(xh, wl, preferred_element_type=F32)
              + jnp.dot(xl, wh, preferred_element_type=F32)) + rb_ref[...]
    _topk_rows(logits, eidx_ref, gw_ref, rk_ref, cnt_ref)


def router(x, gain, shift, scale, router_w, router_b, tm=TOK_BLOCK):
    s, d = x.shape
    e = router_w.shape[1]
    rw = jnp.pad(router_w, ((0, 0), (0, LANES - e)))
    rb = jnp.pad(router_b, (0, LANES - e), constant_values=-1e30).reshape(1, LANES)
    row = pl.BlockSpec((1, d), lambda i: (0, 0))
    return pl.pallas_call(
        _router_body,
        grid=(s // tm,),
        in_specs=[pl.BlockSpec((tm, d), lambda i: (i, 0)), row, row, row,
                  pl.BlockSpec((d, LANES), lambda i: (0, 0)), pl.BlockSpec((1, LANES), lambda i: (0, 0))],
        out_specs=[pl.BlockSpec((tm, d // 2), lambda i: (i, 0))] + [pl.BlockSpec((tm, LANES), lambda i: (i, 0))] * 3
                  + [pl.BlockSpec((1, LANES), lambda i: (0, 0))],
        out_shape=[jax.ShapeDtypeStruct((s, d // 2), jnp.uint32)] + [jax.ShapeDtypeStruct((s, LANES), F32)] * 3
                  + [jax.ShapeDtypeStruct((1, LANES), F32)],
        compiler_params=_params("arbitrary"),
        name="router",
    )(x, gain.reshape(1, d), shift.reshape(1, d), scale.reshape(1, d), rw, rb)


def _topk_rows(work, eidx_ref, gw_ref, rk_ref, cnt_ref):
    tm = work.shape[0]

    @pl.when(pl.program_id(0) == 0)
    def _():
        cnt_ref[...] = jnp.zeros_like(cnt_ref)

    lane = lax.broadcasted_iota(jnp.int32, (tm, LANES), 1).astype(F32)
    vals, idxs, hots = [], [], []
    for _ in range(TOP_K):
        m = jnp.max(work, axis=-1, keepdims=True)
        idx = jnp.min(jnp.where(work == m, lane, float(LANES)), axis=-1, keepdims=True)
        hot = lane == idx
        work = jnp.where(hot, -jnp.inf, work)
        vals.append(m)
        idxs.append(idx)
        hots.append(hot)
    exps = [jnp.exp(v - vals[0]) for v in vals]
    den = exps[0] + exps[1] + exps[2] + exps[3]
    sel = jnp.zeros((tm, LANES), F32)
    for hot in hots:
        sel = sel + hot.astype(F32)
    ri = lax.broadcasted_iota(jnp.int32, (tm, tm), 0)
    ci = lax.broadcasted_iota(jnp.int32, (tm, tm), 1)
    before = jnp.dot((ri > ci).astype(BF16), sel.astype(BF16), preferred_element_type=F32)
    rank = before + cnt_ref[...]
    cnt_ref[...] = cnt_ref[...] + jnp.sum(sel, axis=0, keepdims=True)
    eidx = jnp.zeros((tm, LANES), F32)
    gw = jnp.zeros((tm, LANES), F32)
    rk = jnp.zeros((tm, LANES), F32)
    for kslot in range(TOP_K):
        here = lane == kslot
        eidx = jnp.where(here, idxs[kslot], eidx)
        gw = jnp.where(here, exps[kslot] / den, gw)
        rk = jnp.where(here, jnp.sum(jnp.where(hots[kslot], rank, 0.0), axis=-1, keepdims=True), rk)
    eidx_ref[...] = eidx
    gw_ref[...] = gw
    rk_ref[...] = rk


def _slot_body(eidx_ref, rk_ref, ps_ref, pos_ref):
    tm = eidx_ref.shape[0]
    lane = lax.broadcasted_iota(jnp.int32, (tm, LANES), 1).astype(F32)
    eidx = eidx_ref[...]
    rk = rk_ref[...]
    pos = jnp.zeros((tm, LANES), F32)
    for kslot in range(TOP_K):
        e_k = jnp.sum(jnp.where(lane == kslot, eidx, 0.0), axis=-1, keepdims=True)
        start = jnp.sum(jnp.where(lane == e_k, ps_ref[...], 0.0), axis=-1, keepdims=True)
        r_k = jnp.sum(jnp.where(lane == kslot, rk, 0.0), axis=-1, keepdims=True)
        pos = jnp.where(lane == kslot, start + r_k, pos)
    pos_ref[...] = pos.astype(jnp.int32)


def slot_positions(eidx, rk, padded_start, tm=TOK_BLOCK):
    s = eidx.shape[0]
    blk = pl.BlockSpec((tm, LANES), lambda i: (i, 0))
    return pl.pallas_call(
        _slot_body,
        grid=(s // tm,),
        in_specs=[blk, blk, pl.BlockSpec((1, LANES), lambda i: (0, 0))],
        out_specs=blk,
        out_shape=jax.ShapeDtypeStruct((s, LANES), jnp.int32),
        compiler_params=_params("arbitrary"),
        name="slot_positions",
    )(eidx, rk, padded_start)


def _dispatch_body(pos_ref, hf_ref, xs_in_ref, xs_ref, sem):
    del xs_in_ref
    tm = hf_ref.shape[0]

    def issue(t, carry):
        for kslot in range(TOP_K):
            p = pos_ref[t * TOP_K + kslot]
            pltpu.make_async_copy(hf_ref.at[pl.ds(t, 1)], xs_ref.at[pl.ds(p, 1)], sem.at[0]).start(
                priority=kslot % 2)
        return carry

    lax.fori_loop(0, tm, issue, 0, unroll=4)

    for _ in range(TOP_K):
        pltpu.make_async_copy(hf_ref, xs_ref.at[pl.ds(0, tm)], sem.at[0]).wait()


def dispatch(pos_flat, hf, n_rows, tm=TOK_BLOCK):
    s, d = hf.shape
    return pl.pallas_call(
        _dispatch_body,
        grid=(s // tm,),
        in_specs=[pl.BlockSpec((tm * TOP_K,), lambda i: (i,), memory_space=pltpu.SMEM),
                  pl.BlockSpec((tm, d), lambda i: (i, 0)),
                  pl.BlockSpec(memory_space=pl.ANY)],
        out_specs=pl.BlockSpec(memory_space=pl.ANY),
        out_shape=jax.ShapeDtypeStruct((n_rows, d), hf.dtype),
        scratch_shapes=[pltpu.SemaphoreType.DMA((1,))],
        input_output_aliases={2: 0},
        compiler_params=_params("arbitrary"),
        name="moe_dispatch",
    )(pos_flat, hf, jnp.zeros((n_rows, d), hf.dtype))


def _expert_body(be_ref, first_ref, nxt_ref, tot_ref, xs_ref, wgu_hbm, bgu_ref, wdn_hbm, bdn_ref, perm_ref, ys_ref,
                 stg_gu, stg_dn, wgu_s, wdn_s, sem, *, layer, chunk):
    i = pl.program_id(0)
    rows = xs_ref.shape[0]
    d, n_gu = wgu_s.shape
    dh = wdn_s.shape[0]
    grp = perm_ref.shape[0]
    half = grp // 2

    def fetch(e):
        return (pltpu.make_async_copy(wgu_hbm.at[layer, e], stg_gu, sem.at[0]),
                pltpu.make_async_copy(wdn_hbm.at[layer, e], stg_dn, sem.at[1]))

    @pl.when(first_ref[i] == 1)
    def _():
        @pl.when(i == 0)
        def _():
            for cp in fetch(be_ref[0]):
                cp.start()

        for cp in fetch(be_ref[i]):
            cp.wait()
        for c in range(d // chunk):
            wgu_s[c * chunk:(c + 1) * chunk, :] = stg_gu[c * chunk:(c + 1) * chunk, :].astype(BF16)
        for c in range(dh // chunk):
            wdn_s[c * chunk:(c + 1) * chunk, :] = stg_dn[c * chunk:(c + 1) * chunk, :].astype(BF16)

        @pl.when(nxt_ref[i] >= 0)
        def _():
            for cp in fetch(nxt_ref[i]):
                cp.start()

    @pl.when(i * rows < tot_ref[0])
    def _():
        x_lo, x_hi = _unpack_rows(xs_ref[...])
        gu = _bdot(x_lo, wgu_s[0:d // 2, :]) + _bdot(x_hi, wgu_s[d // 2:d, :]) + bgu_ref[0]
        gub = gu.astype(BF16)
        gs, us = [], []
        for j in range(n_gu // grp):
            pj = jnp.dot(gub[:, j * grp:(j + 1) * grp], perm_ref[...], preferred_element_type=F32)
            gs.append(pj[:, :half])
            us.append(pj[:, half:])
        g = jnp.minimum(jnp.concatenate(gs, axis=1), SWIGLU_LIMIT)
        u = jnp.clip(jnp.concatenate(us, axis=1), -SWIGLU_LIMIT, SWIGLU_LIMIT)
        act = (u + 1.0) * (g * jax.nn.sigmoid(g * SWIGLU_ALPHA))
        ys_ref[...] = _pack_rows(_bdot(act, wdn_s[...]) + bdn_ref[0])

    @pl.when(i * rows >= tot_ref[0])
    def _():
        ys_ref[...] = jnp.zeros_like(ys_ref)


def expert_ffn(blk_expert, blk_first, blk_next, total_rows, xs, w_gu, b_gu, w_dn, b_dn, layer, rows=MOE_ROWS):
    n_rows, dp = xs.shape
    _, e, d, n_gu = w_gu.shape
    dh = w_dn.shape[2]
    grp = 2 * LANES
    perm = np.zeros((grp, grp), np.float32)
    perm[2 * np.arange(LANES), np.arange(LANES)] = 1.0
    perm[2 * np.arange(LANES) + 1, LANES + np.arange(LANES)] = 1.0
    grid_spec = pltpu.PrefetchScalarGridSpec(
        num_scalar_prefetch=4,
        grid=(n_rows // rows,),
        in_specs=[pl.BlockSpec((rows, dp), lambda i, be, fi, nx, tot: (i, 0)),
                  pl.BlockSpec(memory_space=pl.ANY),
                  pl.BlockSpec((1, 1, n_gu), lambda i, be, fi, nx, tot: (be[i], 0, 0)),
                  pl.BlockSpec(memory_space=pl.ANY),
                  pl.BlockSpec((1, 1, d), lambda i, be, fi, nx, tot: (be[i], 0, 0)),
                  pl.BlockSpec((grp, grp), lambda i, be, fi, nx, tot: (0, 0))],
        out_specs=pl.BlockSpec((rows, dp), lambda i, be, fi, nx, tot: (i, 0)),
        scratch_shapes=[pltpu.VMEM((d, n_gu), F32), pltpu.VMEM((dh, d), F32),
                        pltpu.VMEM((d, n_gu), BF16), pltpu.VMEM((dh, d), BF16),
                        pltpu.SemaphoreType.DMA((2,))],
    )
    return pl.pallas_call(
        functools.partial(_expert_body, layer=layer, chunk=256),
        grid_spec=grid_spec,
        out_shape=jax.ShapeDtypeStruct((n_rows, dp), jnp.uint32),
        compiler_params=_params("arbitrary"),
        name="moe_experts",
    )(blk_expert, blk_first, blk_next, total_rows, xs, w_gu, b_gu.reshape(e, 1, n_gu), w_dn,
      b_dn.reshape(e, 1, d), jnp.asarray(perm, BF16))


def _combine_body(pos_ref, x_ref, gw_ref, g2_ref, ys_ref, o_ref, buf, sem):
    tm = x_ref.shape[0]

    def issue(t, carry):
        for kslot in range(TOP_K):
            p = pos_ref[t * TOP_K + kslot]
            pltpu.make_async_copy(ys_ref.at[pl.ds(p, 1)], buf.at[kslot, pl.ds(t, 1)], sem.at[0]).start(
                priority=kslot % 2)
        return carry

    lax.fori_loop(0, tm, issue, 0, unroll=4)

    for kslot in range(TOP_K):
        pltpu.make_async_copy(ys_ref.at[pl.ds(0, tm)], buf.at[kslot], sem.at[0]).wait()

    gw = gw_ref[...]
    half = buf.shape[2]
    acc_lo = jnp.zeros((tm, half), F32)
    acc_hi = jnp.zeros((tm, half), F32)
    for kslot in range(TOP_K):
        lo, hi = _unpack_rows(buf[kslot])
        acc_lo = acc_lo + gw[:, kslot:kslot + 1] * lo
        acc_hi = acc_hi + gw[:, kslot:kslot + 1] * hi
    o_ref[:, 0:half] = x_ref[:, 0:half] + g2_ref[:, 0:half] * acc_lo
    o_ref[:, half:2 * half] = x_ref[:, half:2 * half] + g2_ref[:, half:2 * half] * acc_hi


def combine(pos_flat, x, gw, g2_row, ys, tm=TOK_BLOCK):
    s, d = x.shape
    return pl.pallas_call(
        _combine_body,
        grid=(s // tm,),
        in_specs=[pl.BlockSpec((tm * TOP_K,), lambda i: (i,), memory_space=pltpu.SMEM),
                  pl.BlockSpec((tm, d), lambda i: (i, 0)),
                  pl.BlockSpec((tm, LANES), lambda i: (i, 0)),
                  pl.BlockSpec((1, d), lambda i: (0, 0)),
                  pl.BlockSpec(memory_space=pl.ANY)],
        out_specs=pl.BlockSpec((tm, d), lambda i: (i, 0)),
        out_shape=jax.ShapeDtypeStruct((s, d), F32),
        scratch_shapes=[pltpu.VMEM((TOP_K, tm, d // 2), jnp.uint32), pltpu.SemaphoreType.DMA((1,))],
        compiler_params=_params("arbitrary"),
        name="moe_combine",
    )(pos_flat, x, gw, g2_row.reshape(1, d), ys)


def moe_layer(x, gain, shift, scale, g2_row, router_w, router_b, w_gu, b_gu, w_dn, b_dn, layer):
    s, d = x.shape
    e = router_w.shape[1]
    hf, eidx, gw, rk, counts = router(x, gain, shift, scale, router_w, router_b)
    cnt = counts[0, :e].astype(jnp.int32)
    padded = (cnt + MOE_ROWS - 1) // MOE_ROWS * MOE_ROWS
    padded_end = jnp.cumsum(padded)
    padded_start = jnp.pad((padded_end - padded).astype(F32), (0, LANES - e)).reshape(1, LANES)
    n_rows = s * TOP_K + e * MOE_ROWS
    blk_first_row = jnp.arange(n_rows // MOE_ROWS, dtype=jnp.int32) * MOE_ROWS
    blk_expert = jnp.minimum(
        jnp.sum((padded_end[None, :] <= blk_first_row[:, None]).astype(jnp.int32), axis=1), e - 1)
    total_rows = padded_end[-1:].astype(jnp.int32)
    n_blk = n_rows // MOE_ROWS
    blk_id = jnp.arange(n_blk, dtype=jnp.int32)
    prev_expert = jnp.concatenate([jnp.full((1,), -1, jnp.int32), blk_expert[:-1]])
    blk_first = ((blk_first_row < total_rows[0]) & (blk_expert != prev_expert)).astype(jnp.int32)
    first_at = jnp.where(blk_first == 1, blk_id, n_blk)
    next_first = jnp.concatenate([lax.cummin(first_at, reverse=True)[1:], jnp.full((1,), n_blk, jnp.int32)])
    blk_next = jnp.where(next_first < n_blk, blk_expert[jnp.minimum(next_first, n_blk - 1)], -1).astype(jnp.int32)
    pos = slot_positions(eidx, rk, padded_start)
    pos_flat = pos[:, :TOP_K].reshape(-1)
    xs = dispatch(pos_flat, hf, n_rows)
    ys = expert_ffn(blk_expert, blk_first, blk_next, total_rows, xs, w_gu, b_gu, w_dn, b_dn, layer)
    return combine(pos_flat, x, gw, g2_row, ys)


def kernel(x, c, positions, norm_gain, ada_w, ada_b, mlstm_w_in, mlstm_b_gates, mlstm_norm_gain, mlstm_w_out, rwkv_mu, rwkv_w_rkv, rwkv_w0, rwkv_w1, rwkv_w2, rwkv_a0, rwkv_a1, rwkv_a2, rwkv_g1, rwkv_g2, rwkv_k_k, rwkv_k_a, rwkv_r_k, rwkv_ln_gain, rwkv_ln_bias, rwkv_w_o, diff_w_qkv, diff_lambda, diff_subln_gain, diff_w_o, moe_router_w, moe_router_b, moe_w_gate_up, moe_b_gate_up, moe_w_down, moe_b_down, final_gain):
    b, s, d = x.shape
    assert b == 1
    depth = norm_gain.shape[0]
    xt = x.reshape(s, d)
    pos = positions.reshape(s)
    mod = ada_mod(c, ada_w, ada_b).reshape(depth, 6, d)
    seen = [0, 0, 0]
    for layer in range(depth):
        kind = layer % 3
        j = seen[kind]
        seen[kind] += 1
        sh1, sc1, g1, sh2, sc2, g2 = (mod[layer, i] for i in range(6))
        gain = norm_gain[layer, 0]
        if kind == 0:
            xt = mlstm_mixer(xt, gain, sh1, sc1, mlstm_w_in[j], mlstm_b_gates[j], mlstm_norm_gain[j],
                             mlstm_w_out[j], g1)
        elif kind == 1:
            xt = rwkv_mixer(xt, gain, sh1, sc1, rwkv_mu[j], rwkv_w_rkv[j], rwkv_w0[j], rwkv_w1[j], rwkv_w2[j],
                            rwkv_a0[j], rwkv_a1[j], rwkv_a2[j], rwkv_g1[j], rwkv_g2[j], rwkv_k_k[j],
                            rwkv_k_a[j], rwkv_r_k[j], rwkv_ln_gain[j], rwkv_ln_bias[j], rwkv_w_o[j], g1)
        else:
            xt = diff_mixer(xt, gain, sh1, sc1, pos, diff_w_qkv[j], diff_lambda[j], diff_subln_gain[j],
                            diff_w_o[j], layer, g1)
        xt = moe_layer(xt, norm_gain[layer, 1], sh2, sc2, g2, moe_router_w[layer], moe_router_b[layer],
                       moe_w_gate_up, moe_b_gate_up[layer], moe_w_down, moe_b_down[layer], layer)
    zeros = jnp.zeros((d,), F32)
    return norm_mod(xt, final_gain, zeros, zeros, F32).reshape(b, s, d)
```

```python
import functools
import math

import numpy as np
import jax
import jax.numpy as jnp
from jax import lax
from jax.experimental import pallas as pl
from jax.experimental.pallas import tpu as pltpu

F32 = jnp.float32
BF16 = jnp.bfloat16
HIGHEST = lax.Precision.HIGHEST

LANES = 128
VMEM_LIMIT = 52 * 1024 * 1024

NORM_EPS = 1e-6
MLSTM_HEADS = 4
MLSTM_CHUNK = 128
GATE_SOFTCAP = 15.0
RWKV_HEAD = 64
RWKV_GN_EPS = 64e-5
DIFF_HEADS = 8
DIFF_DQK = 128
DIFF_DV = 256
DIFF_SUBLN_EPS = 1e-5
ROPE_DIM = 32
ROPE_THETA = 500000.0
N_EXPERTS = 32
TOP_K = 4
SWIGLU_LIMIT = 7.0
SWIGLU_ALPHA = 1.702
MOE_ROWS = 256
TOK_BLOCK = 256


def _params(*sem):
    return pltpu.CompilerParams(dimension_semantics=sem, vmem_limit_bytes=VMEM_LIMIT)


def _bdot(a, b):
    return jnp.dot(a.astype(BF16), b.astype(BF16), preferred_element_type=F32)


def _dot_nt(a, b):
    return lax.dot_general(a.astype(BF16), b.astype(BF16), (((1,), (1,)), ((), ())),
                           preferred_element_type=F32)


def _dot_tn(a, b):
    return lax.dot_general(a.astype(BF16), b.astype(BF16), (((0,), (0,)), ((), ())),
                           preferred_element_type=F32)


def _ada_body(c_ref, w_ref, b_ref, o_ref, *, tn):
    kb, n = w_ref.shape[1], w_ref.shape[2]

    @pl.when(pl.program_id(1) == 0)
    def _():
        o_ref[0] = b_ref[0]

    cv = c_ref[...]
    sc = cv * jax.nn.sigmoid(cv)
    for j in range(n // tn):
        acc = jnp.zeros((8, tn), F32)
        for g in range(kb // 8):
            acc = acc + sc[g * 8:(g + 1) * 8] * w_ref[0, g * 8:(g + 1) * 8, j * tn:(j + 1) * tn]
        o_ref[0, :, j * tn:(j + 1) * tn] += jnp.sum(acc, axis=0, keepdims=True)


def ada_mod(c, ada_w, ada_b, kb=256, tn=2048):
    depth, d, n = ada_w.shape
    return pl.pallas_call(
        functools.partial(_ada_body, tn=tn),
        grid=(depth, d // kb),
        in_specs=[pl.BlockSpec((kb, 1), lambda l, k: (k, 0)),
                  pl.BlockSpec((1, kb, n), lambda l, k: (l, k, 0)),
                  pl.BlockSpec((1, 1, n), lambda l, k: (l, 0, 0))],
        out_specs=pl.BlockSpec((1, 1, n), lambda l, k: (l, 0, 0)),
        out_shape=jax.ShapeDtypeStruct((depth, 1, n), F32),
        compiler_params=_params("arbitrary", "arbitrary"),
        name="ada_mod",
    )(c.reshape(d, 1), ada_w, ada_b.reshape(depth, 1, n))


def _norm_rows(x, gain, shift, scale, eps):
    y = x * lax.rsqrt(jnp.mean(x * x, axis=-1, keepdims=True) + eps)
    return (y * gain) * (1.0 + scale) + shift


def _norm_mod_body(x_ref, g_ref, sh_ref, sc_ref, o_ref):
    o_ref[...] = _norm_rows(x_ref[...], g_ref[...], sh_ref[...], sc_ref[...], NORM_EPS).astype(o_ref.dtype)


def norm_mod(x, gain, shift, scale, out_dtype, tm=256):
    s, d = x.shape
    row = pl.BlockSpec((1, d), lambda i: (0, 0))
    return pl.pallas_call(
        _norm_mod_body,
        grid=(s // tm,),
        in_specs=[pl.BlockSpec((tm, d), lambda i: (i, 0)), row, row, row],
        out_specs=pl.BlockSpec((tm, d), lambda i: (i, 0)),
        out_shape=jax.ShapeDtypeStruct((s, d), out_dtype),
        compiler_params=_params("arbitrary"),
        name="norm_mod",
    )(x, gain.reshape(1, d), shift.reshape(1, d), scale.reshape(1, d))


def _mm_body(*refs, fused):
    if fused:
        x_ref, w_ref, res_ref, gate_ref, o_ref, wb_ref = refs
    else:
        x_ref, w_ref, o_ref, wb_ref = refs

    @pl.when(pl.program_id(1) == 0)
    def _():
        wb_ref[...] = w_ref[...].astype(BF16)

    acc = jnp.dot(x_ref[...].astype(BF16), wb_ref[...], preferred_element_type=F32)
    if fused:
        acc = res_ref[...] + gate_ref[...] * acc
    o_ref[...] = acc.astype(o_ref.dtype)


def matmul(x, w, out_dtype=F32, res=None, gate=None, n_cols=None, tm=512, tn=1024):
    m, k = x.shape
    n = w.shape[1] if n_cols is None else n_cols
    tn = min(tn, n)
    tm = min(tm, m)
    fused = res is not None
    in_specs = [pl.BlockSpec((tm, k), lambda j, i: (i, 0)),
                pl.BlockSpec((k, tn), lambda j, i: (0, j))]
    args = [x, w]
    if fused:
        in_specs += [pl.BlockSpec((tm, tn), lambda j, i: (i, j)),
                     pl.BlockSpec((1, tn), lambda j, i: (0, j))]
        args += [res, gate.reshape(1, n)]
    return pl.pallas_call(
        functools.partial(_mm_body, fused=fused),
        grid=(n // tn, m // tm),
        in_specs=in_specs,
        out_specs=pl.BlockSpec((tm, tn), lambda j, i: (i, j)),
        out_shape=jax.ShapeDtypeStruct((m, n), out_dtype),
        scratch_shapes=[pltpu.VMEM((k, tn), BF16)],
        compiler_params=_params("arbitrary", "arbitrary"),
        name="matmul",
    )(*args)


def _mlstm_body(proj_ref, gt_ref, bg_ref, ng_ref, out_ref, c_s, n_s, m_s, *, dk, dv):
    H = MLSTM_HEADS

    @pl.when(pl.program_id(0) == 0)
    def _():
        c_s[...] = jnp.zeros_like(c_s)
        n_s[...] = jnp.zeros_like(n_s)
        m_s[...] = jnp.zeros_like(m_s)

    for h in range(H):
        _mlstm_head(h,
                    proj_ref.at[:, pl.ds(h * dk, dk)],
                    proj_ref.at[:, pl.ds(H * dk + h * dk, dk)],
                    proj_ref.at[:, pl.ds(2 * H * dk + h * dv, dv)],
                    proj_ref.at[:, pl.ds(2 * H * dk + H * dv + h * dv, dv)],
                    gt_ref, bg_ref, ng_ref.at[:, pl.ds(h * dv, dv)], out_ref.at[:, pl.ds(h * dv, dv)],
                    c_s.at[h], n_s.at[h], m_s.at[h])


def _mlstm_head(h, q_ref, k_ref, v_ref, o_ref, gt_ref, bg_ref, ng_ref, out_ref, c_s, n_s, m_s):
    L = q_ref.shape[0]
    dk = q_ref.shape[1]
    dv = v_ref.shape[1]

    lane = lax.broadcasted_iota(jnp.int32, (L, LANES), 1)
    row = lax.broadcasted_iota(jnp.int32, (L, L), 0)
    col = lax.broadcasted_iota(jnp.int32, (L, L), 1)

    capped = GATE_SOFTCAP * jnp.tanh((gt_ref[...] + bg_ref[...]) / GATE_SOFTCAP)
    li = jnp.sum(jnp.where(lane == h, capped, 0.0), axis=-1, keepdims=True)
    fpre = jnp.sum(jnp.where(lane == h + MLSTM_HEADS, capped, 0.0), axis=-1, keepdims=True)
    lf = jax.nn.log_sigmoid(fpre)

    tril = (row >= col).astype(F32)
    g_b = jnp.dot(tril, jnp.broadcast_to(lf, (L, L)), precision=HIGHEST,
                  preferred_element_type=F32)
    g = g_b[:, 0:1]
    g_tot = jnp.sum(lf, axis=0, keepdims=True)
    eye = row == col
    g_row = jnp.sum(jnp.where(eye, g_b, 0.0), axis=0, keepdims=True)
    li_row = jnp.sum(jnp.where(eye, jnp.broadcast_to(li, (L, L)), 0.0), axis=0, keepdims=True)

    m_prev = m_s[...]
    c_prev = c_s[...]
    n_prev = n_s[...]

    q = q_ref[...] * (dk ** -0.5)
    k = k_ref[...]
    vb = v_ref[...].astype(BF16)
    qb = q.astype(BF16)

    d_log = jnp.where(row >= col, g_b - g_row + li_row, -jnp.inf)
    inter_log = g + m_prev
    m_t = jnp.maximum(inter_log, jnp.max(d_log, axis=-1, keepdims=True))
    s = _dot_nt(qb, k) * jnp.exp(d_log - m_t)
    inter = jnp.exp(inter_log - m_t)
    num = _bdot(s, vb) + inter * _bdot(qb, c_prev)
    den = jnp.sum(s, axis=-1, keepdims=True) + inter * jnp.sum(q * n_prev, axis=-1, keepdims=True)
    hh = num / jnp.maximum(jnp.abs(den), jnp.exp(-m_t))
    hh = hh * lax.rsqrt(jnp.mean(hh * hh, axis=-1, keepdims=True) + NORM_EPS) * ng_ref[...]
    out_ref[...] = (jax.nn.sigmoid(o_ref[...]) * hh).astype(out_ref.dtype)

    w_state = g_tot - g + li
    m_loc = jnp.max(w_state, axis=0, keepdims=True)
    ks = jnp.exp(w_state - m_loc) * k
    kv = _dot_tn(ks, vb)
    n_c = jnp.sum(ks, axis=0, keepdims=True)
    m_new = jnp.maximum(g_tot + m_prev, m_loc)
    a = jnp.exp(g_tot + m_prev - m_new)
    b = jnp.exp(m_loc - m_new)
    c_s[...] = a * c_prev + b * kv
    n_s[...] = a * n_prev + b * n_c
    m_s[...] = m_new


def mlstm_cells(proj, gates, b_gates_pad, norm_gain):
    s = proj.shape[0]
    d = norm_gain.shape[0]
    H, L = MLSTM_HEADS, MLSTM_CHUNK
    dk = (proj.shape[1] - 2 * d) // (2 * H)
    dv = d // H
    n = proj.shape[1]
    return pl.pallas_call(
        functools.partial(_mlstm_body, dk=dk, dv=dv),
        grid=(s // L,),
        in_specs=[pl.BlockSpec((L, n), lambda c: (c, 0)),
                  pl.BlockSpec((L, LANES), lambda c: (c, 0)),
                  pl.BlockSpec((1, LANES), lambda c: (0, 0)),
                  pl.BlockSpec((1, d), lambda c: (0, 0))],
        out_specs=pl.BlockSpec((L, d), lambda c: (c, 0)),
        out_shape=jax.ShapeDtypeStruct((s, d), BF16),
        scratch_shapes=[pltpu.VMEM((H, dk, dv), F32), pltpu.VMEM((H, 1, dk), F32), pltpu.VMEM((H, 1, 1), F32)],
        compiler_params=_params("arbitrary"),
        name="mlstm_cells",
    )(proj, gates, b_gates_pad, norm_gain.reshape(1, d))


def mlstm_mixer(x, gain, shift, scale, w_in, b_gates, norm_gain, w_out, gate_row):
    d = x.shape[1]
    n_main = w_in.shape[1] - 2 * MLSTM_HEADS
    hm = norm_mod(x, gain, shift, scale, BF16)
    proj = matmul(hm, w_in, n_cols=n_main)
    w_g = jnp.pad(w_in[:, n_main:], ((0, 0), (0, LANES - 2 * MLSTM_HEADS)))
    gates = matmul(hm, w_g)
    b_pad = jnp.pad(b_gates, (0, LANES - 2 * MLSTM_HEADS)).reshape(1, LANES)
    cells = mlstm_cells(proj, gates, b_pad, norm_gain)
    return matmul(cells, w_out, res=x, gate=gate_row)


def _rwkv_prep_body(x_ref, xp_ref, g_ref, sh_ref, sc_ref, mu_ref, *outs):
    tm = x_ref.shape[0]
    hm = _norm_rows(x_ref[...], g_ref[...], sh_ref[...], sc_ref[...], NORM_EPS)
    prev = _norm_rows(xp_ref[7:8, :], g_ref[...], sh_ref[...], sc_ref[...], NORM_EPS)
    prev = jnp.where(pl.program_id(0) == 0, 0.0, prev)
    rows = lax.broadcasted_iota(jnp.int32, hm.shape, 0)
    shifted = jnp.where(rows == 0, prev, pltpu.roll(hm, 1, axis=0))
    xx = shifted - hm
    for j, o_ref in enumerate(outs):
        o_ref[...] = (hm + xx * mu_ref[j:j + 1, :]).astype(o_ref.dtype)


def rwkv_prep(x, gain, shift, scale, mu, tm=256):
    s, d = x.shape
    row = pl.BlockSpec((1, d), lambda i: (0, 0))
    blk = pl.BlockSpec((tm, d), lambda i: (i, 0))
    return pl.pallas_call(
        _rwkv_prep_body,
        grid=(s // tm,),
        in_specs=[blk, pl.BlockSpec((8, d), lambda i: (jnp.maximum(i * (tm // 8) - 1, 0), 0)),
                  row, row, row, pl.BlockSpec((8, d), lambda i: (0, 0))],
        out_specs=[blk] * 6,
        out_shape=[jax.ShapeDtypeStruct((s, d), BF16)] * 6,
        compiler_params=_params("arbitrary"),
        name="rwkv_prep",
    )(x, x, gain.reshape(1, d), shift.reshape(1, d), scale.reshape(1, d), jnp.pad(mu, ((0, 2), (0, 0))))


def _lora_body(x_ref, w1_ref, w2_ref, o_ref, *, act):
    t = _bdot(x_ref[...], w1_ref[...])
    if act == "tanh":
        t = jnp.tanh(t)
    elif act == "sigmoid":
        t = jax.nn.sigmoid(t)
    o_ref[...] = _bdot(t, w2_ref[...])


def lora(x, w1, w2, act, tm=512):
    s, d = x.shape
    r = w1.shape[1]
    rp = -(-r // LANES) * LANES
    w1p = jnp.pad(w1, ((0, 0), (0, rp - r)))
    w2p = jnp.pad(w2, ((0, rp - r), (0, 0)))
    n = w2.shape[1]
    return pl.pallas_call(
        functools.partial(_lora_body, act=act),
        grid=(s // tm,),
        in_specs=[pl.BlockSpec((tm, d), lambda i: (i, 0)),
                  pl.BlockSpec((d, rp), lambda i: (0, 0)),
                  pl.BlockSpec((rp, n), lambda i: (0, 0))],
        out_specs=pl.BlockSpec((tm, n), lambda i: (i, 0)),
        out_shape=jax.ShapeDtypeStruct((s, n), F32),
        compiler_params=_params("arbitrary"),
        name="lora",
    )(x, w1p, w2p)


def _split3(x):
    hi = x.astype(BF16)
    r1 = x - hi.astype(F32)
    mid = r1.astype(BF16)
    lo = (r1 - mid.astype(F32)).astype(BF16)
    return hi, mid, lo


def _segsum(x, bmat, passes):
    parts = _split3(x)[:passes] if passes > 1 else (x.astype(BF16),)
    acc = jnp.dot(parts[0], bmat, preferred_element_type=F32)
    for p in parts[1:]:
        acc = acc + jnp.dot(p, bmat, preferred_element_type=F32)
    return acc


SUBLANES = 8


SCAN_W = 256


def _head_ones(width):
    ri = lax.broadcasted_iota(jnp.int32, (width, width), 0)
    ci = lax.broadcasted_iota(jnp.int32, (width, width), 1)
    return ((ri // RWKV_HEAD) == (ci // RWKV_HEAD)).astype(BF16)


def _rwkv_pre_body(k_ref, wl_ref, al_ref, w0_ref, a0_ref, kk_ref, ka_ref, nkk_ref, w_ref, kka_ref, kp_ref):
    d = k_ref.shape[1]
    bmat = _head_ones(SCAN_W)
    for j in range(d // SCAN_W):
        cs = slice(j * SCAN_W, (j + 1) * SCAN_W)
        kraw = k_ref[:, cs]
        w_log = -jax.nn.softplus(-(w0_ref[:, cs] + wl_ref[:, cs])) - 0.5
        w_ref[:, cs] = -jnp.exp(w_log)
        a = jax.nn.sigmoid(a0_ref[:, cs] + al_ref[:, cs])
        kk = kraw * kk_ref[:, cs]
        nrm = jnp.sqrt(_segsum(kk * kk, bmat, 3))
        kk = kk / jnp.maximum(nrm, 1e-12)
        nkk_ref[:, cs] = -kk
        kka_ref[:, cs] = kk * a
        kp_ref[:, cs] = kraw * (1.0 + (a - 1.0) * ka_ref[:, cs])


def rwkv_pre(k, wl, al, w0, a0, k_k, k_a, tm=256):
    s, d = k.shape
    blk = pl.BlockSpec((tm, d), lambda i: (i, 0))
    row = pl.BlockSpec((1, d), lambda i: (0, 0))
    return pl.pallas_call(
        _rwkv_pre_body,
        grid=(s // tm,),
        in_specs=[blk] * 3 + [row] * 4,
        out_specs=[blk] * 4,
        out_shape=[jax.ShapeDtypeStruct((s, d), F32)] * 4,
        compiler_params=_params("arbitrary"),
        name="rwkv_pre",
    )(k, wl, al, *[t.reshape(1, d) for t in (w0, a0, k_k, k_a)])


def _rwkv_post_body(y_ref, r_ref, kp_ref, v_ref, gt_ref, rk_ref, lg_ref, lb_ref, o_ref):
    d = y_ref.shape[1]
    bmat = _head_ones(SCAN_W)
    inv = 1.0 / RWKV_HEAD
    for j in range(d // SCAN_W):
        cs = slice(j * SCAN_W, (j + 1) * SCAN_W)
        y = y_ref[:, cs]
        mean = _segsum(y, bmat, 3) * inv
        yc = y - mean
        var = _segsum(yc * yc, bmat, 3) * inv
        yn = yc * lax.rsqrt(var + RWKV_GN_EPS) * lg_ref[:, cs] + lb_ref[:, cs]
        bonus = _segsum(r_ref[:, cs] * kp_ref[:, cs] * rk_ref[:, cs], bmat, 3) * v_ref[:, cs]
        o_ref[:, cs] = ((yn + bonus) * gt_ref[:, cs]).astype(o_ref.dtype)


def rwkv_post(y, r, kp, v, gt, r_k, ln_gain, ln_bias, tm=256):
    s, d = y.shape
    blk = pl.BlockSpec((tm, d), lambda i: (i, 0))
    row = pl.BlockSpec((1, d), lambda i: (0, 0))
    return pl.pallas_call(
        _rwkv_post_body,
        grid=(s // tm,),
        in_specs=[blk] * 5 + [row] * 3,
        out_specs=blk,
        out_shape=jax.ShapeDtypeStruct((s, d), BF16),
        compiler_params=_params("arbitrary"),
        name="rwkv_post",
    )(y, r, kp, v, gt, *[t.reshape(1, d) for t in (r_k, ln_gain, ln_bias)])


RWKV_CHUNK = 32


def _rwkv_chunk_body(a_ref, lw_ref, b_ref, kp_ref, r_ref, v_ref, y_ref, st_s):
    L, d = r_ref.shape
    nv = RWKV_HEAD
    W = SCAN_W
    hpt = W // nv
    ngrp = L // SUBLANES

    @pl.when(pl.program_id(0) == 0)
    def _():
        st_s[...] = jnp.zeros_like(st_s)

    lane_head = lax.broadcasted_iota(jnp.int32, (1, W), 1) // nv
    heads = [lane_head == h for h in range(hpt)]
    ti = lax.broadcasted_iota(jnp.int32, (L, L), 0)
    si = lax.broadcasted_iota(jnp.int32, (L, L), 1)
    tril = (ti >= si).astype(F32)
    t_row = lax.broadcasted_iota(jnp.int32, (2 * L, 2 * hpt * L), 0)
    s_lane = lax.broadcasted_iota(jnp.int32, (2 * L, 2 * hpt * L), 1) % L
    g_mask = s_lane <= jnp.where(t_row < L, t_row - 1, t_row - L)
    sub = lax.broadcasted_iota(jnp.int32, (SUBLANES, W), 0)
    ones = _head_ones(W)

    def stack(x):
        return jnp.concatenate([jnp.where(hm, x, 0.0) for hm in heads], axis=0)

    for j in range(d // W):
        cs = slice(j * W, (j + 1) * W)
        rs = slice(j * nv, (j + 1) * nv)
        lw = lw_ref[:, cs]
        cum = jnp.dot(tril, lw, precision=HIGHEST, preferred_element_type=F32)
        tot = cum[L - 1:L, :]
        w_inv = jnp.exp(-cum)
        w_rest = jnp.exp(tot - cum)
        a_hat = a_ref[:, cs] * jnp.exp(cum - lw)
        r_hat = r_ref[:, cs] * jnp.exp(cum)
        k_hat = kp_ref[:, cs] * w_inv
        b_hat = b_ref[:, cs] * w_inv
        v = v_ref[:, cs]
        s0 = st_s[rs, :]

        ar = jnp.concatenate([a_hat, r_hat], axis=0).astype(BF16)
        uy0 = _dot_nt(ar, stack(s0))
        g = _dot_nt(ar, jnp.concatenate([stack(k_hat), stack(b_hat)], axis=0))
        g = jnp.where(g_mask, g, 0.0)
        v_st = stack(v)
        base = uy0[0:L] + _bdot(g[0:L, 0:hpt * L], v_st)

        need = [(s, g8) for s in range(L - 1) for g8 in range(s // SUBLANES, ngrp)]
        parts = [a_hat[g8 * SUBLANES:(g8 + 1) * SUBLANES] * b_hat[s:s + 1, :] for s, g8 in need]
        if len(parts) % 2:
            parts.append(jnp.zeros((SUBLANES, W), F32))
        lhs = jnp.concatenate(parts, axis=0)
        coef = jnp.dot(lhs.astype(BF16), ones, preferred_element_type=F32)
        at = {sg: n * SUBLANES for n, sg in enumerate(need)}
        u = [base[g8 * SUBLANES:(g8 + 1) * SUBLANES] for g8 in range(ngrp)]
        for s in range(L - 1):
            g0, i0 = divmod(s, SUBLANES)
            us = jnp.broadcast_to(u[g0][i0:i0 + 1, :], (SUBLANES, W))
            for g8 in range(g0, ngrp):
                c = coef[at[s, g8]:at[s, g8] + SUBLANES]
                if g8 == g0:
                    c = jnp.where(sub > i0, c, 0.0)
                u[g8] = u[g8] + c * us
        u = jnp.concatenate(u, axis=0)

        vu = jnp.concatenate([v_st, stack(u)], axis=0)
        y_ref[:, cs] = uy0[L:2 * L] + _bdot(g[L:2 * L], vu)

        kb_rest = jnp.concatenate([kp_ref[:, cs] * w_rest, b_ref[:, cs] * w_rest], axis=0)
        upd = _dot_tn(jnp.concatenate([v, u], axis=0), kb_rest)
        new = s0 * jnp.exp(tot)
        for h, hm in enumerate(heads):
            new = new + jnp.where(hm, upd[h * nv:(h + 1) * nv], 0.0)
        st_s[rs, :] = new


def rwkv_chunked(nkk, lw, kka, kp, r, v):
    s, d = r.shape
    L = RWKV_CHUNK
    blk = pl.BlockSpec((L, d), lambda i: (i, 0))
    return pl.pallas_call(
        _rwkv_chunk_body,
        grid=(s // L,),
        in_specs=[blk] * 6,
        out_specs=blk,
        out_shape=jax.ShapeDtypeStruct((s, d), F32),
        scratch_shapes=[pltpu.VMEM(((d // SCAN_W) * RWKV_HEAD, SCAN_W), F32)],
        compiler_params=_params("arbitrary"),
        name="rwkv_chunked",
    )(nkk, lw, kka, kp, r, v)


def rwkv_mixer(x, gain, shift, scale, mu, w_rkv, w0, w1, w2, a0, a1, a2, g1, g2, k_k, k_a, r_k,
               ln_gain, ln_bias, w_o, gate_row):
    xr, xk, xv, xw, xa, xg = rwkv_prep(x, gain, shift, scale, mu)
    r = matmul(xr, w_rkv[0])
    k = matmul(xk, w_rkv[1])
    v = matmul(xv, w_rkv[2])
    wl = lora(xw, w1, w2, "tanh")
    al = lora(xa, a1, a2, "none")
    gt = lora(xg, g1, g2, "sigmoid")
    nkk, lw, kka, kp = rwkv_pre(k, wl, al, w0, a0, k_k, k_a)
    y = rwkv_chunked(nkk, lw, kka, kp, r, v)
    y = rwkv_post(y, r, kp, v, gt, r_k.reshape(-1), ln_gain, ln_bias)
    return matmul(y, w_o, res=x, gate=gate_row)


def _rope_body(p_ref, pos_ref, o_ref, *, n_rot_tiles, q_tiles):
    tm, n = p_ref.shape
    half = ROPE_DIM // 2
    lane = lax.broadcasted_iota(jnp.int32, (tm, LANES), 1)
    inv_freq = jnp.exp((lane % half).astype(F32) * (-2.0 * math.log(ROPE_THETA) / ROPE_DIM))
    ang = pos_ref[...].astype(F32) * inv_freq
    roped = lane < ROPE_DIM
    cos = jnp.where(roped, jnp.cos(ang), 1.0)
    sin = jnp.where(roped, jnp.where(lane < half, -jnp.sin(ang), jnp.sin(ang)), 0.0)
    for j in range(n // LANES):
        x = p_ref[:, j * LANES:(j + 1) * LANES]
        if j < n_rot_tiles:
            partner = jnp.where(lane < half, pltpu.roll(x, LANES - half, axis=1), pltpu.roll(x, half, axis=1))
            x = x * cos + partner * sin
            if j < q_tiles:
                x = x * (DIFF_DQK ** -0.5)
        o_ref[:, j * LANES:(j + 1) * LANES] = x.astype(o_ref.dtype)


def rope_cast(proj, positions, tm=256):
    s, n = proj.shape
    q_tiles = 2 * DIFF_HEADS
    return pl.pallas_call(
        functools.partial(_rope_body, n_rot_tiles=2 * q_tiles, q_tiles=q_tiles),
        grid=(s // tm,),
        in_specs=[pl.BlockSpec((tm, n), lambda i: (i, 0)),
                  pl.BlockSpec((tm, 1), lambda i: (i, 0))],
        out_specs=pl.BlockSpec((tm, n), lambda i: (i, 0)),
        out_shape=jax.ShapeDtypeStruct((s, n), BF16),
        compiler_params=_params("arbitrary"),
        name="rope_cast",
    )(proj, positions.reshape(s, 1))


def _diff_attn_body(qi_ref, kj_ref, q_ref, k_ref, v_ref, lam_ref, sg_ref, o_ref, m_s, l_s, acc_s, *, lambda_init):
    step = pl.program_id(1)
    qi, kj = qi_ref[step], kj_ref[step]
    tb = q_ref.shape[0]

    @pl.when(kj == 0)
    def _():
        m_s[...] = jnp.full_like(m_s, -jnp.inf)
        l_s[...] = jnp.zeros_like(l_s)
        acc_s[...] = jnp.zeros_like(acc_s)

    def accumulate(diagonal):
        v = v_ref[...]
        for c in range(2):
            cs = slice(c * DIFF_DQK, (c + 1) * DIFF_DQK)
            s = _dot_nt(q_ref[:, cs], k_ref[:, cs])
            if diagonal:
                rows = lax.broadcasted_iota(jnp.int32, (tb, tb), 0)
                cols = lax.broadcasted_iota(jnp.int32, (tb, tb), 1)
                s = jnp.where(cols <= rows, s, -jnp.inf)
            m_old = m_s[c]
            m_new = jnp.maximum(m_old, jnp.max(s, axis=-1, keepdims=True))
            alpha = jnp.exp(m_old - m_new)
            p = jnp.exp(s - m_new)
            l_s[c] = alpha * l_s[c] + jnp.sum(p, axis=-1, keepdims=True)
            acc_s[c] = alpha * acc_s[c] + _bdot(p, v)
            m_s[c] = m_new

    @pl.when(kj < qi)
    def _():
        accumulate(False)

    @pl.when(kj == qi)
    def _():
        accumulate(True)
        lp = lam_ref[...]
        lam = (jnp.exp(jnp.sum(lp[0:1] * lp[1:2], axis=-1, keepdims=True))
               - jnp.exp(jnp.sum(lp[2:3] * lp[3:4], axis=-1, keepdims=True)) + lambda_init)
        out = acc_s[0] / l_s[0] - lam * (acc_s[1] / l_s[1])
        out = out * lax.rsqrt(jnp.mean(out * out, axis=-1, keepdims=True) + DIFF_SUBLN_EPS)
        o_ref[...] = (out * sg_ref[...] * (1.0 - lambda_init)).astype(o_ref.dtype)


def diff_attention(qkv, lam_params, subln_gain, lambda_init, tb=1024):
    s = qkv.shape[0]
    H, dv = DIFF_HEADS, DIFF_DV
    tb = min(tb, s)
    nb = s // tb
    pairs = [(qi, kj) for qi in range(nb) for kj in range(qi + 1)]
    qi_tab = jnp.asarray([p[0] for p in pairs], jnp.int32)
    kj_tab = jnp.asarray([p[1] for p in pairs], jnp.int32)
    grid_spec = pltpu.PrefetchScalarGridSpec(
        num_scalar_prefetch=2,
        grid=(H, len(pairs)),
        in_specs=[pl.BlockSpec((tb, dv), lambda h, t, qi, kj: (qi[t], h)),
                  pl.BlockSpec((tb, dv), lambda h, t, qi, kj: (kj[t], H + h)),
                  pl.BlockSpec((tb, dv), lambda h, t, qi, kj: (kj[t], 2 * H + h)),
                  pl.BlockSpec((4, DIFF_DQK), lambda h, t, qi, kj: (0, 0)),
                  pl.BlockSpec((1, dv), lambda h, t, qi, kj: (0, 0))],
        out_specs=pl.BlockSpec((tb, dv), lambda h, t, qi, kj: (qi[t], h)),
        scratch_shapes=[pltpu.VMEM((2, tb, 1), F32), pltpu.VMEM((2, tb, 1), F32),
                        pltpu.VMEM((2, tb, dv), F32)],
    )
    return pl.pallas_call(
        functools.partial(_diff_attn_body, lambda_init=lambda_init),
        grid_spec=grid_spec,
        out_shape=jax.ShapeDtypeStruct((s, H * dv), BF16),
        compiler_params=_params("arbitrary", "arbitrary"),
        name="diff_attention",
    )(qi_tab, kj_tab, qkv, qkv, qkv, lam_params, subln_gain.reshape(1, dv))


def diff_mixer(x, gain, shift, scale, positions, w_qkv, lam_params, subln_gain, w_o, layer, gate_row):
    hm = norm_mod(x, gain, shift, scale, BF16)
    proj = matmul(hm, w_qkv)
    qkv = rope_cast(proj, positions)
    lambda_init = 0.8 - 0.6 * math.exp(-0.3 * layer)
    att = diff_attention(qkv, lam_params, subln_gain, lambda_init)
    return matmul(att, w_o, res=x, gate=gate_row)


HI16 = 0xFFFF0000


def _pack_rows(x):
    m = x.shape[1] // 2
    bits = lax.bitcast_convert_type(x.astype(BF16).astype(F32), jnp.uint32)
    return (bits[:, :m] >> 16) | (bits[:, m:] & jnp.uint32(HI16))


def _unpack_rows(p):
    lo = lax.bitcast_convert_type(p << 16, F32)
    hi = lax.bitcast_convert_type(p & jnp.uint32(HI16), F32)
    return lo, hi


def _router_body(x_ref, g_ref, sh_ref, sc_ref, rw_ref, rb_ref, hf_ref, eidx_ref, gw_ref, rk_ref, cnt_ref):
    hf = _norm_rows(x_ref[...], g_ref[...], sh_ref[...], sc_ref[...], NORM_EPS)
    hf_ref[...] = _pack_rows(hf)
    rw = rw_ref[...]
    xh = hf.astype(BF16)
    xl = (hf - xh.astype(F32)).astype(BF16)
    wh = rw.astype(BF16)
    wl = (rw - wh.astype(F32)).astype(BF16)
    logits = (jnp.dot(xh, wh, preferred_element_type=F32) + jnp.dot(xh, wl, preferred_element_type=F32)
              + jnp.dot(xl, wh, preferred_element_type=F32)) + rb_ref[...]
    _topk_rows(logits, eidx_ref, gw_ref, rk_ref, cnt_ref)


def router(x, gain, shift, scale, router_w, router_b, tm=TOK_BLOCK):
    s, d = x.shape
    e = router_w.shape[1]
    rw = jnp.pad(router_w, ((0, 0), (0, LANES - e)))
    rb = jnp.pad(router_b, (0, LANES - e), constant_values=-1e30).reshape(1, LANES)
    row = pl.BlockSpec((1, d), lambda i: (0, 0))
    return pl.pallas_call(
        _router_body,
        grid=(s // tm,),
        in_specs=[pl.BlockSpec((tm, d), lambda i: (i, 0)), row, row, row,
                  pl.BlockSpec((d, LANES), lambda i: (0, 0)), pl.BlockSpec((1, LANES), lambda i: (0, 0))],
        out_specs=[pl.BlockSpec((tm, d // 2), lambda i: (i, 0))] + [pl.BlockSpec((tm, LANES), lambda i: (i, 0))] * 3
                  + [pl.BlockSpec((1, LANES), lambda i: (0, 0))],
        out_shape=[jax.ShapeDtypeStruct((s, d // 2), jnp.uint32)] + [jax.ShapeDtypeStruct((s, LANES), F32)] * 3
                  + [jax.ShapeDtypeStruct((1, LANES), F32)],
        compiler_params=_params("arbitrary"),
        name="router",
    )(x, gain.reshape(1, d), shift.reshape(1, d), scale.reshape(1, d), rw, rb)


def _topk_rows(work, eidx_ref, gw_ref, rk_ref, cnt_ref):
    tm = work.shape[0]

    @pl.when(pl.program_id(0) == 0)
    def _():
        cnt_ref[...] = jnp.zeros_like(cnt_ref)

    lane = lax.broadcasted_iota(jnp.int32, (tm, LANES), 1).astype(F32)
    vals, idxs, hots = [], [], []
    for _ in range(TOP_K):
        m = jnp.max(work, axis=-1, keepdims=True)
        idx = jnp.min(jnp.where(work == m, lane, float(LANES)), axis=-1, keepdims=True)
        hot = lane == idx
        work = jnp.where(hot, -jnp.inf, work)
        vals.append(m)
        idxs.append(idx)
        hots.append(hot)
    exps = [jnp.exp(v - vals[0]) for v in vals]
    den = exps[0] + exps[1] + exps[2] + exps[3]
    sel = jnp.zeros((tm, LANES), F32)
    for hot in hots:
        sel = sel + hot.astype(F32)
    ri = lax.broadcasted_iota(jnp.int32, (tm, tm), 0)
    ci = lax.broadcasted_iota(jnp.int32, (tm, tm), 1)
    before = jnp.dot((ri > ci).astype(BF16), sel.astype(BF16), preferred_element_type=F32)
    rank = before + cnt_ref[...]
    cnt_ref[...] = cnt_ref[...] + jnp.sum(sel, axis=0, keepdims=True)
    eidx = jnp.zeros((tm, LANES), F32)
    gw = jnp.zeros((tm, LANES), F32)
    rk = jnp.zeros((tm, LANES), F32)
    for kslot in range(TOP_K):
        here = lane == kslot
        eidx = jnp.where(here, idxs[kslot], eidx)
        gw = jnp.where(here, exps[kslot] / den, gw)
        rk = jnp.where(here, jnp.sum(jnp.where(hots[kslot], rank, 0.0), axis=-1, keepdims=True), rk)
    eidx_ref[...] = eidx
    gw_ref[...] = gw
    rk_ref[...] = rk


def _slot_body(eidx_ref, rk_ref, ps_ref, pos_ref):
    tm = eidx_ref.shape[0]
    lane = lax.broadcasted_iota(jnp.int32, (tm, LANES), 1).astype(F32)
    eidx = eidx_ref[...]
    rk = rk_ref[...]
    pos = jnp.zeros((tm, LANES), F32)
    for kslot in range(TOP_K):
        e_k = jnp.sum(jnp.where(lane == kslot, eidx, 0.0), axis=-1, keepdims=True)
        start = jnp.sum(jnp.where(lane == e_k, ps_ref[...], 0.0), axis=-1, keepdims=True)
        r_k = jnp.sum(jnp.where(lane == kslot, rk, 0.0), axis=-1, keepdims=True)
        pos = jnp.where(lane == kslot, start + r_k, pos)
    pos_ref[...] = pos.astype(jnp.int32)


def slot_positions(eidx, rk, padded_start, tm=TOK_BLOCK):
    s = eidx.shape[0]
    blk = pl.BlockSpec((tm, LANES), lambda i: (i, 0))
    return pl.pallas_call(
        _slot_body,
        grid=(s // tm,),
        in_specs=[blk, blk, pl.BlockSpec((1, LANES), lambda i: (0, 0))],
        out_specs=blk,
        out_shape=jax.ShapeDtypeStruct((s, LANES), jnp.int32),
        compiler_params=_params("arbitrary"),
        name="slot_positions",
    )(eidx, rk, padded_start)


def _dispatch_body(fs_ref, fl_ref, tot_ref, pos_ref, hf_ref, xs_ref, zero_s, sem, *, rows):
    tm = hf_ref.shape[0]
    n_rows = xs_ref.shape[0]
    n_exp = fs_ref.shape[0]
    bits = [1 << b for b in reversed(range(rows.bit_length() - 1))]

    @pl.when(pl.program_id(0) == 0)
    def _():
        zero_s[...] = jnp.zeros_like(zero_s)

        def pieces(e, wait):
            off = fs_ref[e]
            for bit in bits:
                @pl.when((fl_ref[e] & bit) != 0)
                def _(off=off, bit=bit):
                    dst = xs_ref.at[pl.ds(pl.multiple_of(off, bit), bit)]
                    cp = pltpu.make_async_copy(zero_s.at[pl.ds(0, bit)], dst, sem.at[1])
                    cp.wait() if wait else cp.start()
                off = off + (fl_ref[e] & bit)

        def tail(b, wait):
            cp = pltpu.make_async_copy(zero_s, xs_ref.at[pl.ds(pl.multiple_of(b * rows, rows), rows)], sem.at[1])
            cp.wait() if wait else cp.start()

        first_tail = tot_ref[0] // rows
        for wait in (False, True):
            lax.fori_loop(0, n_exp, lambda e, c, wait=wait: (pieces(e, wait), c)[1], 0)
            lax.fori_loop(first_tail, n_rows // rows, lambda b, c, wait=wait: (tail(b, wait), c)[1], 0)

    def issue(t, carry):
        for kslot in range(TOP_K):
            p = pos_ref[t * TOP_K + kslot]
            pltpu.make_async_copy(hf_ref.at[pl.ds(t, 1)], xs_ref.at[pl.ds(p, 1)], sem.at[0]).start(
                priority=kslot % 2)
        return carry

    lax.fori_loop(0, tm, issue, 0, unroll=4)

    for _ in range(TOP_K):
        pltpu.make_async_copy(hf_ref, xs_ref.at[pl.ds(0, tm)], sem.at[0]).wait()


def dispatch(fill_start, fill_len, total_rows, pos_flat, hf, n_rows, tm=TOK_BLOCK, rows=MOE_ROWS):
    s, d = hf.shape
    grid_spec = pltpu.PrefetchScalarGridSpec(
        num_scalar_prefetch=3,
        grid=(s // tm,),
        in_specs=[pl.BlockSpec((tm * TOP_K,), lambda i, fs, fl, tot: (i,), memory_space=pltpu.SMEM),
                  pl.BlockSpec((tm, d), lambda i, fs, fl, tot: (i, 0))],
        out_specs=pl.BlockSpec(memory_space=pl.ANY),
        scratch_shapes=[pltpu.VMEM((rows, d), hf.dtype), pltpu.SemaphoreType.DMA((2,))],
    )
    return pl.pallas_call(
        functools.partial(_dispatch_body, rows=rows),
        grid_spec=grid_spec,
        out_shape=jax.ShapeDtypeStruct((n_rows, d), hf.dtype),
        compiler_params=_params("arbitrary"),
        name="moe_dispatch",
    )(fill_start, fill_len, total_rows, pos_flat, hf)


def _expert_body(be_ref, first_ref, nxt_ref, tot_ref, xs_ref, wgu_hbm, bgu_ref, wdn_hbm, bdn_ref, perm_ref, ys_ref,
                 stg_gu, stg_dn, wgu_s, wdn_s, sem, *, layer, chunk):
    i = pl.program_id(0)
    rows = xs_ref.shape[0]
    d, n_gu = wgu_s.shape
    dh = wdn_s.shape[0]
    grp = perm_ref.shape[0]
    half = grp // 2

    def fetch(e):
        return (pltpu.make_async_copy(wgu_hbm.at[layer, e], stg_gu, sem.at[0]),
                pltpu.make_async_copy(wdn_hbm.at[layer, e], stg_dn, sem.at[1]))

    @pl.when(first_ref[i] == 1)
    def _():
        @pl.when(i == 0)
        def _():
            for cp in fetch(be_ref[0]):
                cp.start()

        for cp in fetch(be_ref[i]):
            cp.wait()
        for c in range(d // chunk):
            wgu_s[c * chunk:(c + 1) * chunk, :] = stg_gu[c * chunk:(c + 1) * chunk, :].astype(BF16)
        for c in range(dh // chunk):
            wdn_s[c * chunk:(c + 1) * chunk, :] = stg_dn[c * chunk:(c + 1) * chunk, :].astype(BF16)

        @pl.when(nxt_ref[i] >= 0)
        def _():
            for cp in fetch(nxt_ref[i]):
                cp.start()

    @pl.when(i * rows < tot_ref[0])
    def _():
        x_lo, x_hi = _unpack_rows(xs_ref[...])
        gu = _bdot(x_lo, wgu_s[0:d // 2, :]) + _bdot(x_hi, wgu_s[d // 2:d, :]) + bgu_ref[0]
        gub = gu.astype(BF16)
        gs, us = [], []
        for j in range(n_gu // grp):
            pj = jnp.dot(gub[:, j * grp:(j + 1) * grp], perm_ref[...], preferred_element_type=F32)
            gs.append(pj[:, :half])
            us.append(pj[:, half:])
        g = jnp.minimum(jnp.concatenate(gs, axis=1), SWIGLU_LIMIT)
        u = jnp.clip(jnp.concatenate(us, axis=1), -SWIGLU_LIMIT, SWIGLU_LIMIT)
        act = (u + 1.0) * (g * jax.nn.sigmoid(g * SWIGLU_ALPHA))
        ys_ref[...] = _pack_rows(_bdot(act, wdn_s[...]) + bdn_ref[0])

    @pl.when(i * rows >= tot_ref[0])
    def _():
        ys_ref[...] = jnp.zeros_like(ys_ref)


def expert_ffn(blk_expert, blk_first, blk_next, total_rows, xs, w_gu, b_gu, w_dn, b_dn, layer, rows=MOE_ROWS):
    n_rows, dp = xs.shape
    _, e, d, n_gu = w_gu.shape
    dh = w_dn.shape[2]
    grp = 2 * LANES
    perm = np.zeros((grp, grp), np.float32)
    perm[2 * np.arange(LANES), np.arange(LANES)] = 1.0
    perm[2 * np.arange(LANES) + 1, LANES + np.arange(LANES)] = 1.0
    grid_spec = pltpu.PrefetchScalarGridSpec(
        num_scalar_prefetch=4,
        grid=(n_rows // rows,),
        in_specs=[pl.BlockSpec((rows, dp), lambda i, be, fi, nx, tot: (i, 0)),
                  pl.BlockSpec(memory_space=pl.ANY),
                  pl.BlockSpec((1, 1, n_gu), lambda i, be, fi, nx, tot: (be[i], 0, 0)),
                  pl.BlockSpec(memory_space=pl.ANY),
                  pl.BlockSpec((1, 1, d), lambda i, be, fi, nx, tot: (be[i], 0, 0)),
                  pl.BlockSpec((grp, grp), lambda i, be, fi, nx, tot: (0, 0))],
        out_specs=pl.BlockSpec((rows, dp), lambda i, be, fi, nx, tot: (i, 0)),
        scratch_shapes=[pltpu.VMEM((d, n_gu), F32), pltpu.VMEM((dh, d), F32),
                        pltpu.VMEM((d, n_gu), BF16), pltpu.VMEM((dh, d), BF16),
                        pltpu.SemaphoreType.DMA((2,))],
    )
    return pl.pallas_call(
        functools.partial(_expert_body, layer=layer, chunk=256),
        grid_spec=grid_spec,
        out_shape=jax.ShapeDtypeStruct((n_rows, dp), jnp.uint32),
        compiler_params=_params("arbitrary"),
        name="moe_experts",
    )(blk_expert, blk_first, blk_next, total_rows, xs, w_gu, b_gu.reshape(e, 1, n_gu), w_dn,
      b_dn.reshape(e, 1, d), jnp.asarray(perm, BF16))


def _combine_body(pos_ref, x_ref, gw_ref, g2_ref, ys_ref, o_ref, buf, sem):
    tm = x_ref.shape[0]

    def issue(t, carry):
        for kslot in range(TOP_K):
            p = pos_ref[t * TOP_K + kslot]
            pltpu.make_async_copy(ys_ref.at[pl.ds(p, 1)], buf.at[kslot, pl.ds(t, 1)], sem.at[0]).start(
                priority=kslot % 2)
        return carry

    lax.fori_loop(0, tm, issue, 0, unroll=4)

    for kslot in range(TOP_K):
        pltpu.make_async_copy(ys_ref.at[pl.ds(0, tm)], buf.at[kslot], sem.at[0]).wait()

    gw = gw_ref[...]
    half = buf.shape[2]
    acc_lo = jnp.zeros((tm, half), F32)
    acc_hi = jnp.zeros((tm, half), F32)
    for kslot in range(TOP_K):
        lo, hi = _unpack_rows(buf[kslot])
        acc_lo = acc_lo + gw[:, kslot:kslot + 1] * lo
        acc_hi = acc_hi + gw[:, kslot:kslot + 1] * hi
    o_ref[:, 0:half] = x_ref[:, 0:half] + g2_ref[:, 0:half] * acc_lo
    o_ref[:, half:2 * half] = x_ref[:, half:2 * half] + g2_ref[:, half:2 * half] * acc_hi


def combine(pos_flat, x, gw, g2_row, ys, tm=TOK_BLOCK):
    s, d = x.shape
    return pl.pallas_call(
        _combine_body,
        grid=(s // tm,),
        in_specs=[pl.BlockSpec((tm * TOP_K,), lambda i: (i,), memory_space=pltpu.SMEM),
                  pl.BlockSpec((tm, d), lambda i: (i, 0)),
                  pl.BlockSpec((tm, LANES), lambda i: (i, 0)),
                  pl.BlockSpec((1, d), lambda i: (0, 0)),
                  pl.BlockSpec(memory_space=pl.ANY)],
        out_specs=pl.BlockSpec((tm, d), lambda i: (i, 0)),
        out_shape=jax.ShapeDtypeStruct((s, d), F32),
        scratch_shapes=[pltpu.VMEM((TOP_K, tm, d // 2), jnp.uint32), pltpu.SemaphoreType.DMA((1,))],
        compiler_params=_params("arbitrary"),
        name="moe_combine",
    )(pos_flat, x, gw, g2_row.reshape(1, d), ys)


def moe_layer(x, gain, shift, scale, g2_row, router_w, router_b, w_gu, b_gu, w_dn, b_dn, layer):
    s, d = x.shape
    e = router_w.shape[1]
    hf, eidx, gw, rk, counts = router(x, gain, shift, scale, router_w, router_b)
    cnt = counts[0, :e].astype(jnp.int32)
    padded = (cnt + MOE_ROWS - 1) // MOE_ROWS * MOE_ROWS
    padded_end = jnp.cumsum(padded)
    fill_start = (padded_end - padded).astype(jnp.int32)
    fill_len = (padded - cnt).astype(jnp.int32)
    padded_start = jnp.pad((fill_start + fill_len).astype(F32), (0, LANES - e)).reshape(1, LANES)
    n_rows = s * TOP_K + e * MOE_ROWS
    blk_first_row = jnp.arange(n_rows // MOE_ROWS, dtype=jnp.int32) * MOE_ROWS
    blk_expert = jnp.minimum(
        jnp.sum((padded_end[None, :] <= blk_first_row[:, None]).astype(jnp.int32), axis=1), e - 1)
    total_rows = padded_end[-1:].astype(jnp.int32)
    n_blk = n_rows // MOE_ROWS
    blk_id = jnp.arange(n_blk, dtype=jnp.int32)
    prev_expert = jnp.concatenate([jnp.full((1,), -1, jnp.int32), blk_expert[:-1]])
    blk_first = ((blk_first_row < total_rows[0]) & (blk_expert != prev_expert)).astype(jnp.int32)
    first_at = jnp.where(blk_first == 1, blk_id, n_blk)
    next_first = jnp.concatenate([lax.cummin(first_at, reverse=True)[1:], jnp.full((1,), n_blk, jnp.int32)])
    blk_next = jnp.where(next_first < n_blk, blk_expert[jnp.minimum(next_first, n_blk - 1)], -1).astype(jnp.int32)
    pos = slot_positions(eidx, rk, padded_start)
    pos_flat = pos[:, :TOP_K].reshape(-1)
    xs = dispatch(fill_start, fill_len, total_rows, pos_flat, hf, n_rows)
    ys = expert_ffn(blk_expert, blk_first, blk_next, total_rows, xs, w_gu, b_gu, w_dn, b_dn, layer)
    return combine(pos_flat, x, gw, g2_row, ys)


def kernel(x, c, positions, norm_gain, ada_w, ada_b, mlstm_w_in, mlstm_b_gates, mlstm_norm_gain, mlstm_w_out, rwkv_mu, rwkv_w_rkv, rwkv_w0, rwkv_w1, rwkv_w2, rwkv_a0, rwkv_a1, rwkv_a2, rwkv_g1, rwkv_g2, rwkv_k_k, rwkv_k_a, rwkv_r_k, rwkv_ln_gain, rwkv_ln_bias, rwkv_w_o, diff_w_qkv, diff_lambda, diff_subln_gain, diff_w_o, moe_router_w, moe_router_b, moe_w_gate_up, moe_b_gate_up, moe_w_down, moe_b_down, final_gain):
    b, s, d = x.shape
    assert b == 1
    depth = norm_gain.shape[0]
    xt = x.reshape(s, d)
    pos = positions.reshape(s)
    mod = ada_mod(c, ada_w, ada_b).reshape(depth, 6, d)
    seen = [0, 0, 0]
    for layer in range(depth):
        kind = layer % 3
        j = seen[kind]
        seen[kind] += 1
        sh1, sc1, g1, sh2, sc2, g2 = (mod[layer, i] for i in range(6))
        gain = norm_gain[layer, 0]
        if kind == 0:
            xt = mlstm_mixer(xt, gain, sh1, sc1, mlstm_w_in[j], mlstm_b_gates[j], mlstm_norm_gain[j],
                             mlstm_w_out[j], g1)
        elif kind == 1:
            xt = rwkv_mixer(xt, gain, sh1, sc1, rwkv_mu[j], rwkv_w_rkv[j], rwkv_w0[j], rwkv_w1[j], rwkv_w2[j],
                            rwkv_a0[j], rwkv_a1[j], rwkv_a2[j], rwkv_g1[j], rwkv_g2[j], rwkv_k_k[j],
                            rwkv_k_a[j], rwkv_r_k[j], rwkv_ln_gain[j], rwkv_ln_bias[j], rwkv_w_o[j], g1)
        else:
            xt = diff_mixer(xt, gain, sh1, sc1, pos, diff_w_qkv[j], diff_lambda[j], diff_subln_gain[j],
                            diff_w_o[j], layer, g1)
        xt = moe_layer(xt, norm_gain[layer, 1], sh2, sc2, g2, moe_router_w[layer], moe_router_b[layer],
                       moe_w_gate_up, moe_b_gate_up[layer], moe_w_down, moe_b_down[layer], layer)
    zeros = jnp.zeros((d,), F32)
    return norm_mod(xt, final_gain, zeros, zeros, F32).reshape(b, s, d)
```

```python
import functools
import math

import numpy as np
import jax
import jax.numpy as jnp
from jax import lax
from jax.experimental import pallas as pl
from jax.experimental.pallas import tpu as pltpu

F32 = jnp.float32
BF16 = jnp.bfloat16
HIGHEST = lax.Precision.HIGHEST

LANES = 128
VMEM_LIMIT = 52 * 1024 * 1024

NORM_EPS = 1e-6
MLSTM_HEADS = 4
MLSTM_CHUNK = 128
GATE_SOFTCAP = 15.0
RWKV_HEAD = 64
RWKV_GN_EPS = 64e-5
DIFF_HEADS = 8
DIFF_DQK = 128
DIFF_DV = 256
DIFF_SUBLN_EPS = 1e-5
ROPE_DIM = 32
ROPE_THETA = 500000.0
N_EXPERTS = 32
TOP_K = 4
SWIGLU_LIMIT = 7.0
SWIGLU_ALPHA = 1.702
MOE_ROWS = 256
TOK_BLOCK = 256


def _params(*sem):
    return pltpu.CompilerParams(dimension_semantics=sem, vmem_limit_bytes=VMEM_LIMIT)


def _bdot(a, b):
    return jnp.dot(a.astype(BF16), b.astype(BF16), preferred_element_type=F32)


def _dot_nt(a, b):
    return lax.dot_general(a.astype(BF16), b.astype(BF16), (((1,), (1,)), ((), ())),
                           preferred_element_type=F32)


def _dot_tn(a, b):
    return lax.dot_general(a.astype(BF16), b.astype(BF16), (((0,), (0,)), ((), ())),
                           preferred_element_type=F32)


def _ada_body(c_ref, wa_ref, wb_ref, b_ref, o_ref, *, tn):
    kh, n = wa_ref.shape[1], wa_ref.shape[2]

    @pl.when(pl.program_id(1) == 0)
    def _():
        o_ref[0] = b_ref[0]

    cv = c_ref[...]
    sc = cv * jax.nn.sigmoid(cv)
    for j in range(n // tn):
        acc = jnp.zeros((8, tn), F32)
        for half, w_ref in enumerate((wa_ref, wb_ref)):
            for g in range(kh // 8):
                r0 = half * kh + g * 8
                acc = acc + sc[r0:r0 + 8] * w_ref[0, g * 8:(g + 1) * 8, j * tn:(j + 1) * tn]
        o_ref[0, :, j * tn:(j + 1) * tn] += jnp.sum(acc, axis=0, keepdims=True)


def ada_mod(c, ada_w, ada_b, kb=256, tn=2048):
    depth, d, n = ada_w.shape
    kh = kb // 2
    return pl.pallas_call(
        functools.partial(_ada_body, tn=tn),
        grid=(depth, d // kb),
        in_specs=[pl.BlockSpec((kb, 1), lambda l, k: (k, 0)),
                  pl.BlockSpec((1, kh, n), lambda l, k: (l, 2 * k, 0)),
                  pl.BlockSpec((1, kh, n), lambda l, k: (l, 2 * k + 1, 0)),
                  pl.BlockSpec((1, 1, n), lambda l, k: (l, 0, 0))],
        out_specs=pl.BlockSpec((1, 1, n), lambda l, k: (l, 0, 0)),
        out_shape=jax.ShapeDtypeStruct((depth, 1, n), F32),
        compiler_params=_params("arbitrary", "arbitrary"),
        name="ada_mod",
    )(c.reshape(d, 1), ada_w, ada_w, ada_b.reshape(depth, 1, n))


def _norm_rows(x, gain, shift, scale, eps):
    y = x * lax.rsqrt(jnp.mean(x * x, axis=-1, keepdims=True) + eps)
    return (y * gain) * (1.0 + scale) + shift


def _norm_mod_body(x_ref, g_ref, sh_ref, sc_ref, o_ref):
    o_ref[...] = _norm_rows(x_ref[...], g_ref[...], sh_ref[...], sc_ref[...], NORM_EPS).astype(o_ref.dtype)


def norm_mod(x, gain, shift, scale, out_dtype, tm=256):
    s, d = x.shape
    row = pl.BlockSpec((1, d), lambda i: (0, 0))
    return pl.pallas_call(
        _norm_mod_body,
        grid=(s // tm,),
        in_specs=[pl.BlockSpec((tm, d), lambda i: (i, 0)), row, row, row],
        out_specs=pl.BlockSpec((tm, d), lambda i: (i, 0)),
        out_shape=jax.ShapeDtypeStruct((s, d), out_dtype),
        compiler_params=_params("arbitrary"),
        name="norm_mod",
    )(x, gain.reshape(1, d), shift.reshape(1, d), scale.reshape(1, d))


def _mm_body(*refs, fused):
    if fused:
        x_ref, w_ref, res_ref, gate_ref, o_ref, wb_ref = refs
    else:
        x_ref, w_ref, o_ref, wb_ref = refs

    @pl.when(pl.program_id(1) == 0)
    def _():
        wb_ref[...] = w_ref[...].astype(BF16)

    acc = jnp.dot(x_ref[...].astype(BF16), wb_ref[...], preferred_element_type=F32)
    if fused:
        acc = res_ref[...] + gate_ref[...] * acc
    o_ref[...] = acc.astype(o_ref.dtype)


def matmul(x, w, out_dtype=F32, res=None, gate=None, n_cols=None, tm=512, tn=1024):
    m, k = x.shape
    n = w.shape[1] if n_cols is None else n_cols
    tn = min(tn, n)
    tm = min(tm, m)
    fused = res is not None
    in_specs = [pl.BlockSpec((tm, k), lambda j, i: (i, 0)),
                pl.BlockSpec((k, tn), lambda j, i: (0, j))]
    args = [x, w]
    if fused:
        in_specs += [pl.BlockSpec((tm, tn), lambda j, i: (i, j)),
                     pl.BlockSpec((1, tn), lambda j, i: (0, j))]
        args += [res, gate.reshape(1, n)]
    return pl.pallas_call(
        functools.partial(_mm_body, fused=fused),
        grid=(n // tn, m // tm),
        in_specs=in_specs,
        out_specs=pl.BlockSpec((tm, tn), lambda j, i: (i, j)),
        out_shape=jax.ShapeDtypeStruct((m, n), out_dtype),
        scratch_shapes=[pltpu.VMEM((k, tn), BF16)],
        compiler_params=_params("arbitrary", "arbitrary"),
        name="matmul",
    )(*args)


def _mlstm_body(proj_ref, gt_ref, bg_ref, ng_ref, out_ref, c_s, n_s, m_s, *, dk, dv):
    H = MLSTM_HEADS

    @pl.when(pl.program_id(0) == 0)
    def _():
        c_s[...] = jnp.zeros_like(c_s)
        n_s[...] = jnp.zeros_like(n_s)
        m_s[...] = jnp.zeros_like(m_s)

    for h in range(H):
        _mlstm_head(h,
                    proj_ref.at[:, pl.ds(h * dk, dk)],
                    proj_ref.at[:, pl.ds(H * dk + h * dk, dk)],
                    proj_ref.at[:, pl.ds(2 * H * dk + h * dv, dv)],
                    proj_ref.at[:, pl.ds(2 * H * dk + H * dv + h * dv, dv)],
                    gt_ref, bg_ref, ng_ref.at[:, pl.ds(h * dv, dv)], out_ref.at[:, pl.ds(h * dv, dv)],
                    c_s.at[h], n_s.at[h], m_s.at[h])


def _mlstm_head(h, q_ref, k_ref, v_ref, o_ref, gt_ref, bg_ref, ng_ref, out_ref, c_s, n_s, m_s):
    L = q_ref.shape[0]
    dk = q_ref.shape[1]
    dv = v_ref.shape[1]

    lane = lax.broadcasted_iota(jnp.int32, (L, LANES), 1)
    row = lax.broadcasted_iota(jnp.int32, (L, L), 0)
    col = lax.broadcasted_iota(jnp.int32, (L, L), 1)

    capped = GATE_SOFTCAP * jnp.tanh((gt_ref[...] + bg_ref[...]) / GATE_SOFTCAP)
    li = jnp.sum(jnp.where(lane == h, capped, 0.0), axis=-1, keepdims=True)
    fpre = jnp.sum(jnp.where(lane == h + MLSTM_HEADS, capped, 0.0), axis=-1, keepdims=True)
    lf = jax.nn.log_sigmoid(fpre)

    tril = (row >= col).astype(F32)
    g_b = jnp.dot(tril, jnp.broadcast_to(lf, (L, L)), precision=HIGHEST,
                  preferred_element_type=F32)
    g = g_b[:, 0:1]
    g_tot = jnp.sum(lf, axis=0, keepdims=True)
    eye = row == col
    g_row = jnp.sum(jnp.where(eye, g_b, 0.0), axis=0, keepdims=True)
    li_row = jnp.sum(jnp.where(eye, jnp.broadcast_to(li, (L, L)), 0.0), axis=0, keepdims=True)

    m_prev = m_s[...]
    c_prev = c_s[...]
    n_prev = n_s[...]

    q = q_ref[...] * (dk ** -0.5)
    k = k_ref[...]
    vb = v_ref[...].astype(BF16)
    qb = q.astype(BF16)

    d_log = jnp.where(row >= col, g_b - g_row + li_row, -jnp.inf)
    inter_log = g + m_prev
    m_t = jnp.maximum(inter_log, jnp.max(d_log, axis=-1, keepdims=True))
    s = _dot_nt(qb, k) * jnp.exp(d_log - m_t)
    inter = jnp.exp(inter_log - m_t)
    num = _bdot(s, vb) + inter * _bdot(qb, c_prev)
    den = jnp.sum(s, axis=-1, keepdims=True) + inter * jnp.sum(q * n_prev, axis=-1, keepdims=True)
    hh = num / jnp.maximum(jnp.abs(den), jnp.exp(-m_t))
    hh = hh * lax.rsqrt(jnp.mean(hh * hh, axis=-1, keepdims=True) + NORM_EPS) * ng_ref[...]
    out_ref[...] = (jax.nn.sigmoid(o_ref[...]) * hh).astype(out_ref.dtype)

    w_state = g_tot - g + li
    m_loc = jnp.max(w_state, axis=0, keepdims=True)
    ks = jnp.exp(w_state - m_loc) * k
    kv = _dot_tn(ks, vb)
    n_c = jnp.sum(ks, axis=0, keepdims=True)
    m_new = jnp.maximum(g_tot + m_prev, m_loc)
    a = jnp.exp(g_tot + m_prev - m_new)
    b = jnp.exp(m_loc - m_new)
    c_s[...] = a * c_prev + b * kv
    n_s[...] = a * n_prev + b * n_c
    m_s[...] = m_new


def mlstm_cells(proj, gates, b_gates_pad, norm_gain):
    s = proj.shape[0]
    d = norm_gain.shape[0]
    H, L = MLSTM_HEADS, MLSTM_CHUNK
    dk = (proj.shape[1] - 2 * d) // (2 * H)
    dv = d // H
    n = proj.shape[1]
    return pl.pallas_call(
        functools.partial(_mlstm_body, dk=dk, dv=dv),
        grid=(s // L,),
        in_specs=[pl.BlockSpec((L, n), lambda c: (c, 0)),
                  pl.BlockSpec((L, LANES), lambda c: (c, 0)),
                  pl.BlockSpec((1, LANES), lambda c: (0, 0)),
                  pl.BlockSpec((1, d), lambda c: (0, 0))],
        out_specs=pl.BlockSpec((L, d), lambda c: (c, 0)),
        out_shape=jax.ShapeDtypeStruct((s, d), BF16),
        scratch_shapes=[pltpu.VMEM((H, dk, dv), F32), pltpu.VMEM((H, 1, dk), F32), pltpu.VMEM((H, 1, 1), F32)],
        compiler_params=_params("arbitrary"),
        name="mlstm_cells",
    )(proj, gates, b_gates_pad, norm_gain.reshape(1, d))


def mlstm_mixer(x, gain, shift, scale, w_in, b_gates, norm_gain, w_out, gate_row):
    d = x.shape[1]
    n_main = w_in.shape[1] - 2 * MLSTM_HEADS
    hm = norm_mod(x, gain, shift, scale, BF16)
    proj = matmul(hm, w_in, n_cols=n_main)
    w_g = jnp.pad(w_in[:, n_main:], ((0, 0), (0, LANES - 2 * MLSTM_HEADS)))
    gates = matmul(hm, w_g)
    b_pad = jnp.pad(b_gates, (0, LANES - 2 * MLSTM_HEADS)).reshape(1, LANES)
    cells = mlstm_cells(proj, gates, b_pad, norm_gain)
    return matmul(cells, w_out, res=x, gate=gate_row)


def _rwkv_prep_body(x_ref, xp_ref, g_ref, sh_ref, sc_ref, mu_ref, *outs):
    tm = x_ref.shape[0]
    hm = _norm_rows(x_ref[...], g_ref[...], sh_ref[...], sc_ref[...], NORM_EPS)
    prev = _norm_rows(xp_ref[7:8, :], g_ref[...], sh_ref[...], sc_ref[...], NORM_EPS)
    prev = jnp.where(pl.program_id(0) == 0, 0.0, prev)
    rows = lax.broadcasted_iota(jnp.int32, hm.shape, 0)
    shifted = jnp.where(rows == 0, prev, pltpu.roll(hm, 1, axis=0))
    xx = shifted - hm
    for j, o_ref in enumerate(outs):
        o_ref[...] = (hm + xx * mu_ref[j:j + 1, :]).astype(o_ref.dtype)


def rwkv_prep(x, gain, shift, scale, mu, tm=256):
    s, d = x.shape
    row = pl.BlockSpec((1, d), lambda i: (0, 0))
    blk = pl.BlockSpec((tm, d), lambda i: (i, 0))
    return pl.pallas_call(
        _rwkv_prep_body,
        grid=(s // tm,),
        in_specs=[blk, pl.BlockSpec((8, d), lambda i: (jnp.maximum(i * (tm // 8) - 1, 0), 0)),
                  row, row, row, pl.BlockSpec((8, d), lambda i: (0, 0))],
        out_specs=[blk] * 6,
        out_shape=[jax.ShapeDtypeStruct((s, d), BF16)] * 6,
        compiler_params=_params("arbitrary"),
        name="rwkv_prep",
    )(x, x, gain.reshape(1, d), shift.reshape(1, d), scale.reshape(1, d), jnp.pad(mu, ((0, 2), (0, 0))))


def _lora_body(x_ref, w1_ref, w2_ref, o_ref, *, act):
    t = _bdot(x_ref[...], w1_ref[...])
    if act == "tanh":
        t = jnp.tanh(t)
    elif act == "sigmoid":
        t = jax.nn.sigmoid(t)
    o_ref[...] = _bdot(t, w2_ref[...])


def lora(x, w1, w2, act, tm=512):
    s, d = x.shape
    r = w1.shape[1]
    rp = -(-r // LANES) * LANES
    w1p = jnp.pad(w1, ((0, 0), (0, rp - r)))
    w2p = jnp.pad(w2, ((0, rp - r), (0, 0)))
    n = w2.shape[1]
    return pl.pallas_call(
        functools.partial(_lora_body, act=act),
        grid=(s // tm,),
        in_specs=[pl.BlockSpec((tm, d), lambda i: (i, 0)),
                  pl.BlockSpec((d, rp), lambda i: (0, 0)),
                  pl.BlockSpec((rp, n), lambda i: (0, 0))],
        out_specs=pl.BlockSpec((tm, n), lambda i: (i, 0)),
        out_shape=jax.ShapeDtypeStruct((s, n), F32),
        compiler_params=_params("arbitrary"),
        name="lora",
    )(x, w1p, w2p)


def _split3(x):
    hi = x.astype(BF16)
    r1 = x - hi.astype(F32)
    mid = r1.astype(BF16)
    lo = (r1 - mid.astype(F32)).astype(BF16)
    return hi, mid, lo


def _segsum(x, bmat, passes):
    parts = _split3(x)[:passes] if passes > 1 else (x.astype(BF16),)
    acc = jnp.dot(parts[0], bmat, preferred_element_type=F32)
    for p in parts[1:]:
        acc = acc + jnp.dot(p, bmat, preferred_element_type=F32)
    return acc


SUBLANES = 8


SCAN_W = 256


def _head_ones(width):
    ri = lax.broadcasted_iota(jnp.int32, (width, width), 0)
    ci = lax.broadcasted_iota(jnp.int32, (width, width), 1)
    return ((ri // RWKV_HEAD) == (ci // RWKV_HEAD)).astype(BF16)


def _rwkv_pre_body(k_ref, wl_ref, al_ref, w0_ref, a0_ref, kk_ref, ka_ref, nkk_ref, w_ref, kka_ref, kp_ref):
    d = k_ref.shape[1]
    bmat = _head_ones(SCAN_W)
    for j in range(d // SCAN_W):
        cs = slice(j * SCAN_W, (j + 1) * SCAN_W)
        kraw = k_ref[:, cs]
        w_log = -jax.nn.softplus(-(w0_ref[:, cs] + wl_ref[:, cs])) - 0.5
        w_ref[:, cs] = -jnp.exp(w_log)
        a = jax.nn.sigmoid(a0_ref[:, cs] + al_ref[:, cs])
        kk = kraw * kk_ref[:, cs]
        nrm = jnp.sqrt(_segsum(kk * kk, bmat, 3))
        kk = kk / jnp.maximum(nrm, 1e-12)
        nkk_ref[:, cs] = -kk
        kka_ref[:, cs] = kk * a
        kp_ref[:, cs] = kraw * (1.0 + (a - 1.0) * ka_ref[:, cs])


def rwkv_pre(k, wl, al, w0, a0, k_k, k_a, tm=256):
    s, d = k.shape
    blk = pl.BlockSpec((tm, d), lambda i: (i, 0))
    row = pl.BlockSpec((1, d), lambda i: (0, 0))
    return pl.pallas_call(
        _rwkv_pre_body,
        grid=(s // tm,),
        in_specs=[blk] * 3 + [row] * 4,
        out_specs=[blk] * 4,
        out_shape=[jax.ShapeDtypeStruct((s, d), F32)] * 4,
        compiler_params=_params("arbitrary"),
        name="rwkv_pre",
    )(k, wl, al, *[t.reshape(1, d) for t in (w0, a0, k_k, k_a)])


def _rwkv_post_body(y_ref, r_ref, kp_ref, v_ref, gt_ref, rk_ref, lg_ref, lb_ref, o_ref):
    d = y_ref.shape[1]
    bmat = _head_ones(SCAN_W)
    inv = 1.0 / RWKV_HEAD
    for j in range(d // SCAN_W):
        cs = slice(j * SCAN_W, (j + 1) * SCAN_W)
        y = y_ref[:, cs]
        mean = _segsum(y, bmat, 3) * inv
        yc = y - mean
        var = _segsum(yc * yc, bmat, 3) * inv
        yn = yc * lax.rsqrt(var + RWKV_GN_EPS) * lg_ref[:, cs] + lb_ref[:, cs]
        bonus = _segsum(r_ref[:, cs] * kp_ref[:, cs] * rk_ref[:, cs], bmat, 3) * v_ref[:, cs]
        o_ref[:, cs] = ((yn + bonus) * gt_ref[:, cs]).astype(o_ref.dtype)


def rwkv_post(y, r, kp, v, gt, r_k, ln_gain, ln_bias, tm=256):
    s, d = y.shape
    blk = pl.BlockSpec((tm, d), lambda i: (i, 0))
    row = pl.BlockSpec((1, d), lambda i: (0, 0))
    return pl.pallas_call(
        _rwkv_post_body,
        grid=(s // tm,),
        in_specs=[blk] * 5 + [row] * 3,
        out_specs=blk,
        out_shape=jax.ShapeDtypeStruct((s, d), BF16),
        compiler_params=_params("arbitrary"),
        name="rwkv_post",
    )(y, r, kp, v, gt, *[t.reshape(1, d) for t in (r_k, ln_gain, ln_bias)])


RWKV_CHUNK = 32


def _rwkv_chunk_body(a_ref, lw_ref, b_ref, kp_ref, r_ref, v_ref, y_ref, st_s):
    L, d = r_ref.shape
    nv = RWKV_HEAD
    W = SCAN_W
    hpt = W // nv
    ngrp = L // SUBLANES

    @pl.when(pl.program_id(0) == 0)
    def _():
        st_s[...] = jnp.zeros_like(st_s)

    lane_head = lax.broadcasted_iota(jnp.int32, (1, W), 1) // nv
    heads = [lane_head == h for h in range(hpt)]
    ti = lax.broadcasted_iota(jnp.int32, (L, L), 0)
    si = lax.broadcasted_iota(jnp.int32, (L, L), 1)
    tril = (ti >= si).astype(F32)
    t_row = lax.broadcasted_iota(jnp.int32, (2 * L, 2 * hpt * L), 0)
    s_lane = lax.broadcasted_iota(jnp.int32, (2 * L, 2 * hpt * L), 1) % L
    g_mask = s_lane <= jnp.where(t_row < L, t_row - 1, t_row - L)
    sub = lax.broadcasted_iota(jnp.int32, (SUBLANES, W), 0)
    ones = _head_ones(W)

    def stack(x):
        return jnp.concatenate([jnp.where(hm, x, 0.0) for hm in heads], axis=0)

    for j in range(d // W):
        cs = slice(j * W, (j + 1) * W)
        rs = slice(j * nv, (j + 1) * nv)
        lw = lw_ref[:, cs]
        cum = jnp.dot(tril, lw, precision=HIGHEST, preferred_element_type=F32)
        tot = cum[L - 1:L, :]
        w_inv = jnp.exp(-cum)
        w_rest = jnp.exp(tot - cum)
        a_hat = a_ref[:, cs] * jnp.exp(cum - lw)
        r_hat = r_ref[:, cs] * jnp.exp(cum)
        k_hat = kp_ref[:, cs] * w_inv
        b_hat = b_ref[:, cs] * w_inv
        v = v_ref[:, cs]
        s0 = st_s[rs, :]

        ar = jnp.concatenate([a_hat, r_hat], axis=0).astype(BF16)
        uy0 = _dot_nt(ar, stack(s0))
        g = _dot_nt(ar, jnp.concatenate([stack(k_hat), stack(b_hat)], axis=0))
        g = jnp.where(g_mask, g, 0.0)
        v_st = stack(v)
        base = uy0[0:L] + _bdot(g[0:L, 0:hpt * L], v_st)

        need = [(s, g8) for s in range(L - 1) for g8 in range(s // SUBLANES, ngrp)]
        parts = [a_hat[g8 * SUBLANES:(g8 + 1) * SUBLANES] * b_hat[s:s + 1, :] for s, g8 in need]
        if len(parts) % 2:
            parts.append(jnp.zeros((SUBLANES, W), F32))
        lhs = jnp.concatenate(parts, axis=0)
        coef = jnp.dot(lhs.astype(BF16), ones, preferred_element_type=F32)
        at = {sg: n * SUBLANES for n, sg in enumerate(need)}
        u = [base[g8 * SUBLANES:(g8 + 1) * SUBLANES] for g8 in range(ngrp)]
        for s in range(L - 1):
            g0, i0 = divmod(s, SUBLANES)
            us = jnp.broadcast_to(u[g0][i0:i0 + 1, :], (SUBLANES, W))
            for g8 in range(g0, ngrp):
                c = coef[at[s, g8]:at[s, g8] + SUBLANES]
                if g8 == g0:
                    c = jnp.where(sub > i0, c, 0.0)
                u[g8] = u[g8] + c * us
        u = jnp.concatenate(u, axis=0)

        vu = jnp.concatenate([v_st, stack(u)], axis=0)
        y_ref[:, cs] = uy0[L:2 * L] + _bdot(g[L:2 * L], vu)

        kb_rest = jnp.concatenate([kp_ref[:, cs] * w_rest, b_ref[:, cs] * w_rest], axis=0)
        upd = _dot_tn(jnp.concatenate([v, u], axis=0), kb_rest)
        new = s0 * jnp.exp(tot)
        for h, hm in enumerate(heads):
            new = new + jnp.where(hm, upd[h * nv:(h + 1) * nv], 0.0)
        st_s[rs, :] = new


def rwkv_chunked(nkk, lw, kka, kp, r, v):
    s, d = r.shape
    L = RWKV_CHUNK
    blk = pl.BlockSpec((L, d), lambda i: (i, 0))
    return pl.pallas_call(
        _rwkv_chunk_body,
        grid=(s // L,),
        in_specs=[blk] * 6,
        out_specs=blk,
        out_shape=jax.ShapeDtypeStruct((s, d), F32),
        scratch_shapes=[pltpu.VMEM(((d // SCAN_W) * RWKV_HEAD, SCAN_W), F32)],
        compiler_params=_params("arbitrary"),
        name="rwkv_chunked",
    )(nkk, lw, kka, kp, r, v)


def rwkv_mixer(x, gain, shift, scale, mu, w_rkv, w0, w1, w2, a0, a1, a2, g1, g2, k_k, k_a, r_k,
               ln_gain, ln_bias, w_o, gate_row):
    xr, xk, xv, xw, xa, xg = rwkv_prep(x, gain, shift, scale, mu)
    r = matmul(xr, w_rkv[0])
    k = matmul(xk, w_rkv[1])
    v = matmul(xv, w_rkv[2])
    wl = lora(xw, w1, w2, "tanh")
    al = lora(xa, a1, a2, "none")
    gt = lora(xg, g1, g2, "sigmoid")
    nkk, lw, kka, kp = rwkv_pre(k, wl, al, w0, a0, k_k, k_a)
    y = rwkv_chunked(nkk, lw, kka, kp, r, v)
    y = rwkv_post(y, r, kp, v, gt, r_k.reshape(-1), ln_gain, ln_bias)
    return matmul(y, w_o, res=x, gate=gate_row)


def _rope_body(p_ref, pos_ref, o_ref, *, n_rot_tiles, q_tiles):
    tm, n = p_ref.shape
    half = ROPE_DIM // 2
    lane = lax.broadcasted_iota(jnp.int32, (tm, LANES), 1)
    inv_freq = jnp.exp((lane % half).astype(F32) * (-2.0 * math.log(ROPE_THETA) / ROPE_DIM))
    ang = pos_ref[...].astype(F32) * inv_freq
    roped = lane < ROPE_DIM
    cos = jnp.where(roped, jnp.cos(ang), 1.0)
    sin = jnp.where(roped, jnp.where(lane < half, -jnp.sin(ang), jnp.sin(ang)), 0.0)
    for j in range(n // LANES):
        x = p_ref[:, j * LANES:(j + 1) * LANES]
        if j < n_rot_tiles:
            partner = jnp.where(lane < half, pltpu.roll(x, LANES - half, axis=1), pltpu.roll(x, half, axis=1))
            x = x * cos + partner * sin
            if j < q_tiles:
                x = x * (DIFF_DQK ** -0.5)
        o_ref[:, j * LANES:(j + 1) * LANES] = x.astype(o_ref.dtype)


def rope_cast(proj, positions, tm=256):
    s, n = proj.shape
    q_tiles = 2 * DIFF_HEADS
    return pl.pallas_call(
        functools.partial(_rope_body, n_rot_tiles=2 * q_tiles, q_tiles=q_tiles),
        grid=(s // tm,),
        in_specs=[pl.BlockSpec((tm, n), lambda i: (i, 0)),
                  pl.BlockSpec((tm, 1), lambda i: (i, 0))],
        out_specs=pl.BlockSpec((tm, n), lambda i: (i, 0)),
        out_shape=jax.ShapeDtypeStruct((s, n), BF16),
        compiler_params=_params("arbitrary"),
        name="rope_cast",
    )(proj, positions.reshape(s, 1))


def _diff_attn_body(qi_ref, kj_ref, q_ref, k_ref, v_ref, lam_ref, sg_ref, o_ref, m_s, l_s, acc_s, *, lambda_init):
    step = pl.program_id(1)
    qi, kj = qi_ref[step], kj_ref[step]
    tb = q_ref.shape[0]

    @pl.when(kj == 0)
    def _():
        m_s[...] = jnp.full_like(m_s, -jnp.inf)
        l_s[...] = jnp.zeros_like(l_s)
        acc_s[...] = jnp.zeros_like(acc_s)

    def accumulate(diagonal):
        v = v_ref[...]
        for c in range(2):
            cs = slice(c * DIFF_DQK, (c + 1) * DIFF_DQK)
            s = _dot_nt(q_ref[:, cs], k_ref[:, cs])
            if diagonal:
                rows = lax.broadcasted_iota(jnp.int32, (tb, tb), 0)
                cols = lax.broadcasted_iota(jnp.int32, (tb, tb), 1)
                s = jnp.where(cols <= rows, s, -jnp.inf)
            m_old = m_s[c]
            m_new = jnp.maximum(m_old, jnp.max(s, axis=-1, keepdims=True))
            alpha = jnp.exp(m_old - m_new)
            p = jnp.exp(s - m_new)
            l_s[c] = alpha * l_s[c] + jnp.sum(p, axis=-1, keepdims=True)
            acc_s[c] = alpha * acc_s[c] + _bdot(p, v)
            m_s[c] = m_new

    @pl.when(kj < qi)
    def _():
        accumulate(False)

    @pl.when(kj == qi)
    def _():
        accumulate(True)
        lp = lam_ref[...]
        lam = (jnp.exp(jnp.sum(lp[0:1] * lp[1:2], axis=-1, keepdims=True))
               - jnp.exp(jnp.sum(lp[2:3] * lp[3:4], axis=-1, keepdims=True)) + lambda_init)
        out = acc_s[0] / l_s[0] - lam * (acc_s[1] / l_s[1])
        out = out * lax.rsqrt(jnp.mean(out * out, axis=-1, keepdims=True) + DIFF_SUBLN_EPS)
        o_ref[...] = (out * sg_ref[...] * (1.0 - lambda_init)).astype(o_ref.dtype)


def diff_attention(qkv, lam_params, subln_gain, lambda_init, tb=1024):
    s = qkv.shape[0]
    H, dv = DIFF_HEADS, DIFF_DV
    tb = min(tb, s)
    nb = s // tb
    pairs = [(qi, kj) for qi in range(nb) for kj in range(qi + 1)]
    qi_tab = jnp.asarray([p[0] for p in pairs], jnp.int32)
    kj_tab = jnp.asarray([p[1] for p in pairs], jnp.int32)
    grid_spec = pltpu.PrefetchScalarGridSpec(
        num_scalar_prefetch=2,
        grid=(H, len(pairs)),
        in_specs=[pl.BlockSpec((tb, dv), lambda h, t, qi, kj: (qi[t], h)),
                  pl.BlockSpec((tb, dv), lambda h, t, qi, kj: (kj[t], H + h)),
                  pl.BlockSpec((tb, dv), lambda h, t, qi, kj: (kj[t], 2 * H + h)),
                  pl.BlockSpec((4, DIFF_DQK), lambda h, t, qi, kj: (0, 0)),
                  pl.BlockSpec((1, dv), lambda h, t, qi, kj: (0, 0))],
        out_specs=pl.BlockSpec((tb, dv), lambda h, t, qi, kj: (qi[t], h)),
        scratch_shapes=[pltpu.VMEM((2, tb, 1), F32), pltpu.VMEM((2, tb, 1), F32),
                        pltpu.VMEM((2, tb, dv), F32)],
    )
    return pl.pallas_call(
        functools.partial(_diff_attn_body, lambda_init=lambda_init),
        grid_spec=grid_spec,
        out_shape=jax.ShapeDtypeStruct((s, H * dv), BF16),
        compiler_params=_params("arbitrary", "arbitrary"),
        name="diff_attention",
    )(qi_tab, kj_tab, qkv, qkv, qkv, lam_params, subln_gain.reshape(1, dv))


def diff_mixer(x, gain, shift, scale, positions, w_qkv, lam_params, subln_gain, w_o, layer, gate_row):
    hm = norm_mod(x, gain, shift, scale, BF16)
    proj = matmul(hm, w_qkv)
    qkv = rope_cast(proj, positions)
    lambda_init = 0.8 - 0.6 * math.exp(-0.3 * layer)
    att = diff_attention(qkv, lam_params, subln_gain, lambda_init)
    return matmul(att, w_o, res=x, gate=gate_row)


HI16 = 0xFFFF0000


def _pack_rows(x):
    m = x.shape[1] // 2
    bits = lax.bitcast_convert_type(x.astype(BF16).astype(F32), jnp.uint32)
    return (bits[:, :m] >> 16) | (bits[:, m:] & jnp.uint32(HI16))


def _unpack_rows(p):
    lo = lax.bitcast_convert_type(p << 16, F32)
    hi = lax.bitcast_convert_type(p & jnp.uint32(HI16), F32)
    return lo, hi


def _router_body(x_ref, g_ref, sh_ref, sc_ref, rw_ref, rb_ref, hf_ref, eidx_ref, gw_ref, rk_ref, cnt_ref):
    hf = _norm_rows(x_ref[...], g_ref[...], sh_ref[...], sc_ref[...], NORM_EPS)
    hf_ref[...] = _pack_rows(hf)
    rw = rw_ref[...]
    xh = hf.astype(BF16)
    xl = (hf - xh.astype(F32)).astype(BF16)
    wh = rw.astype(BF16)
    wl = (rw - wh.astype(F32)).astype(BF16)
    logits = (jnp.dot(xh, wh, preferred_element_type=F32) + jnp.dot(xh, wl, preferred_element_type=F32)
              + jnp.dot(xl, wh, preferred_element_type=F32)) + rb_ref[...]
    _topk_rows(logits, eidx_ref, gw_ref, rk_ref, cnt_ref)


def router(x, gain, shift, scale, router_w, router_b, tm=TOK_BLOCK):
    s, d = x.shape
    e = router_w.shape[1]
    rw = jnp.pad(router_w, ((0, 0), (0, LANES - e)))
    rb = jnp.pad(router_b, (0, LANES - e), constant_values=-1e30).reshape(1, LANES)
    row = pl.BlockSpec((1, d), lambda i: (0, 0))
    return pl.pallas_call(
        _router_body,
        grid=(s // tm,),
        in_specs=[pl.BlockSpec((tm, d), lambda i: (i, 0)), row, row, row,
                  pl.BlockSpec((d, LANES), lambda i: (0, 0)), pl.BlockSpec((1, LANES), lambda i: (0, 0))],
        out_specs=[pl.BlockSpec((tm, d // 2), lambda i: (i, 0))] + [pl.BlockSpec((tm, LANES), lambda i: (i, 0))] * 3
                  + [pl.BlockSpec((1, LANES), lambda i: (0, 0))],
        out_shape=[jax.ShapeDtypeStruct((s, d // 2), jnp.uint32)] + [jax.ShapeDtypeStruct((s, LANES), F32)] * 3
                  + [jax.ShapeDtypeStruct((1, LANES), F32)],
        compiler_params=_params("arbitrary"),
        name="router",
    )(x, gain.reshape(1, d), shift.reshape(1, d), scale.reshape(1, d), rw, rb)


def _topk_rows(work, eidx_ref, gw_ref, rk_ref, cnt_ref):
    tm = work.shape[0]

    @pl.when(pl.program_id(0) == 0)
    def _():
        cnt_ref[...] = jnp.zeros_like(cnt_ref)

    lane = lax.broadcasted_iota(jnp.int32, (tm, LANES), 1).astype(F32)
    vals, idxs, hots = [], [], []
    for _ in range(TOP_K):
        m = jnp.max(work, axis=-1, keepdims=True)
        idx = jnp.min(jnp.where(work == m, lane, float(LANES)), axis=-1, keepdims=True)
        hot = lane == idx
        work = jnp.where(hot, -jnp.inf, work)
        vals.append(m)
        idxs.append(idx)
        hots.append(hot)
    exps = [jnp.exp(v - vals[0]) for v in vals]
    den = exps[0] + exps[1] + exps[2] + exps[3]
    sel = jnp.zeros((tm, LANES), F32)
    for hot in hots:
        sel = sel + hot.astype(F32)
    ri = lax.broadcasted_iota(jnp.int32, (tm, tm), 0)
    ci = lax.broadcasted_iota(jnp.int32, (tm, tm), 1)
    before = jnp.dot((ri > ci).astype(BF16), sel.astype(BF16), preferred_element_type=F32)
    rank = before + cnt_ref[...]
    cnt_ref[...] = cnt_ref[...] + jnp.sum(sel, axis=0, keepdims=True)
    eidx = jnp.zeros((tm, LANES), F32)
    gw = jnp.zeros((tm, LANES), F32)
    rk = jnp.zeros((tm, LANES), F32)
    for kslot in range(TOP_K):
        here = lane == kslot
        eidx = jnp.where(here, idxs[kslot], eidx)
        gw = jnp.where(here, exps[kslot] / den, gw)
        rk = jnp.where(here, jnp.sum(jnp.where(hots[kslot], rank, 0.0), axis=-1, keepdims=True), rk)
    eidx_ref[...] = eidx
    gw_ref[...] = gw
    rk_ref[...] = rk


def _slot_body(eidx_ref, rk_ref, ps_ref, pos_ref):
    tm = eidx_ref.shape[0]
    lane = lax.broadcasted_iota(jnp.int32, (tm, LANES), 1).astype(F32)
    eidx = eidx_ref[...]
    rk = rk_ref[...]
    pos = jnp.zeros((tm, LANES), F32)
    for kslot in range(TOP_K):
        e_k = jnp.sum(jnp.where(lane == kslot, eidx, 0.0), axis=-1, keepdims=True)
        start = jnp.sum(jnp.where(lane == e_k, ps_ref[...], 0.0), axis=-1, keepdims=True)
        r_k = jnp.sum(jnp.where(lane == kslot, rk, 0.0), axis=-1, keepdims=True)
        pos = jnp.where(lane == kslot, start + r_k, pos)
    pos_ref[...] = pos.astype(jnp.int32)


def slot_positions(eidx, rk, padded_start, tm=TOK_BLOCK):
    s = eidx.shape[0]
    blk = pl.BlockSpec((tm, LANES), lambda i: (i, 0))
    return pl.pallas_call(
        _slot_body,
        grid=(s // tm,),
        in_specs=[blk, blk, pl.BlockSpec((1, LANES), lambda i: (0, 0))],
        out_specs=blk,
        out_shape=jax.ShapeDtypeStruct((s, LANES), jnp.int32),
        compiler_params=_params("arbitrary"),
        name="slot_positions",
    )(eidx, rk, padded_start)


def _dispatch_body(fs_ref, fl_ref, tot_ref, pos_ref, hf_ref, xs_ref, zero_s, sem, *, rows):
    tm = hf_ref.shape[0]
    n_rows = xs_ref.shape[0]
    n_exp = fs_ref.shape[0]
    bits = [1 << b for b in reversed(range(rows.bit_length() - 1))]

    @pl.when(pl.program_id(0) == 0)
    def _():
        zero_s[...] = jnp.zeros_like(zero_s)

        def pieces(e, wait):
            off = fs_ref[e]
            for bit in bits:
                @pl.when((fl_ref[e] & bit) != 0)
                def _(off=off, bit=bit):
                    dst = xs_ref.at[pl.ds(pl.multiple_of(off, bit), bit)]
                    cp = pltpu.make_async_copy(zero_s.at[pl.ds(0, bit)], dst, sem.at[1])
                    cp.wait() if wait else cp.start()
                off = off + (fl_ref[e] & bit)

        def tail(b, wait):
            cp = pltpu.make_async_copy(zero_s, xs_ref.at[pl.ds(pl.multiple_of(b * rows, rows), rows)], sem.at[1])
            cp.wait() if wait else cp.start()

        first_tail = tot_ref[0] // rows
        for wait in (False, True):
            lax.fori_loop(0, n_exp, lambda e, c, wait=wait: (pieces(e, wait), c)[1], 0)
            lax.fori_loop(first_tail, n_rows // rows, lambda b, c, wait=wait: (tail(b, wait), c)[1], 0)

    def issue(t, carry):
        for kslot in range(TOP_K):
            p = pos_ref[t * TOP_K + kslot]
            pltpu.make_async_copy(hf_ref.at[pl.ds(t, 1)], xs_ref.at[pl.ds(p, 1)], sem.at[0]).start(
                priority=kslot % 2)
        return carry

    lax.fori_loop(0, tm, issue, 0, unroll=4)

    for _ in range(TOP_K):
        pltpu.make_async_copy(hf_ref, xs_ref.at[pl.ds(0, tm)], sem.at[0]).wait()


def dispatch(fill_start, fill_len, total_rows, pos_flat, hf, n_rows, tm=TOK_BLOCK, rows=MOE_ROWS):
    s, d = hf.shape
    grid_spec = pltpu.PrefetchScalarGridSpec(
        num_scalar_prefetch=3,
        grid=(s // tm,),
        in_specs=[pl.BlockSpec((tm * TOP_K,), lambda i, fs, fl, tot: (i,), memory_space=pltpu.SMEM),
                  pl.BlockSpec((tm, d), lambda i, fs, fl, tot: (i, 0))],
        out_specs=pl.BlockSpec(memory_space=pl.ANY),
        scratch_shapes=[pltpu.VMEM((rows, d), hf.dtype), pltpu.SemaphoreType.DMA((2,))],
    )
    return pl.pallas_call(
        functools.partial(_dispatch_body, rows=rows),
        grid_spec=grid_spec,
        out_shape=jax.ShapeDtypeStruct((n_rows, d), hf.dtype),
        compiler_params=_params("arbitrary"),
        name="moe_dispatch",
    )(fill_start, fill_len, total_rows, pos_flat, hf)


def _expert_body(be_ref, first_ref, nxt_ref, tot_ref, xs_ref, wgu_hbm, bgu_ref, wdn_hbm, bdn_ref, perm_ref, ys_ref,
                 stg_gu, stg_dn, wgu_s, wdn_s, sem, *, layer, chunk):
    i = pl.program_id(0)
    rows = xs_ref.shape[0]
    d, n_gu = wgu_s.shape
    dh = wdn_s.shape[0]
    grp = perm_ref.shape[0]
    half = grp // 2

    def fetch(e):
        return (pltpu.make_async_copy(wgu_hbm.at[layer, e], stg_gu, sem.at[0]),
                pltpu.make_async_copy(wdn_hbm.at[layer, e], stg_dn, sem.at[1]))

    @pl.when(first_ref[i] == 1)
    def _():
        @pl.when(i == 0)
        def _():
            for cp in fetch(be_ref[0]):
                cp.start()

        for cp in fetch(be_ref[i]):
            cp.wait()
        for c in range(d // chunk):
            wgu_s[c * chunk:(c + 1) * chunk, :] = stg_gu[c * chunk:(c + 1) * chunk, :].astype(BF16)
        for c in range(dh // chunk):
            wdn_s[c * chunk:(c + 1) * chunk, :] = stg_dn[c * chunk:(c + 1) * chunk, :].astype(BF16)

        @pl.when(nxt_ref[i] >= 0)
        def _():
            for cp in fetch(nxt_ref[i]):
                cp.start()

    @pl.when(i * rows < tot_ref[0])
    def _():
        x_lo, x_hi = _unpack_rows(xs_ref[...])
        gu = _bdot(x_lo, wgu_s[0:d // 2, :]) + _bdot(x_hi, wgu_s[d // 2:d, :]) + bgu_ref[0]
        gub = gu.astype(BF16)
        gs, us = [], []
        for j in range(n_gu // grp):
            pj = jnp.dot(gub[:, j * grp:(j + 1) * grp], perm_ref[...], preferred_element_type=F32)
            gs.append(pj[:, :half])
            us.append(pj[:, half:])
        g = jnp.minimum(jnp.concatenate(gs, axis=1), SWIGLU_LIMIT)
        u = jnp.clip(jnp.concatenate(us, axis=1), -SWIGLU_LIMIT, SWIGLU_LIMIT)
        act = (u + 1.0) * (g * jax.nn.sigmoid(g * SWIGLU_ALPHA))
        ys_ref[...] = _pack_rows(_bdot(act, wdn_s[...]) + bdn_ref[0])

    @pl.when(i * rows >= tot_ref[0])
    def _():
        ys_ref[...] = jnp.zeros_like(ys_ref)


def expert_ffn(blk_expert, blk_first, blk_next, total_rows, xs, w_gu, b_gu, w_dn, b_dn, layer, rows=MOE_ROWS):
    n_rows, dp = xs.shape
    _, e, d, n_gu = w_gu.shape
    dh = w_dn.shape[2]
    grp = 2 * LANES
    perm = np.zeros((grp, grp), np.float32)
    perm[2 * np.arange(LANES), np.arange(LANES)] = 1.0
    perm[2 * np.arange(LANES) + 1, LANES + np.arange(LANES)] = 1.0
    grid_spec = pltpu.PrefetchScalarGridSpec(
        num_scalar_prefetch=4,
        grid=(n_rows // rows,),
        in_specs=[pl.BlockSpec((rows, dp), lambda i, be, fi, nx, tot: (i, 0)),
                  pl.BlockSpec(memory_space=pl.ANY),
                  pl.BlockSpec((1, 1, n_gu), lambda i, be, fi, nx, tot: (be[i], 0, 0)),
                  pl.BlockSpec(memory_space=pl.ANY),
                  pl.BlockSpec((1, 1, d), lambda i, be, fi, nx, tot: (be[i], 0, 0)),
                  pl.BlockSpec((grp, grp), lambda i, be, fi, nx, tot: (0, 0))],
        out_specs=pl.BlockSpec((rows, dp), lambda i, be, fi, nx, tot: (i, 0)),
        scratch_shapes=[pltpu.VMEM((d, n_gu), F32), pltpu.VMEM((dh, d), F32),
                        pltpu.VMEM((d, n_gu), BF16), pltpu.VMEM((dh, d), BF16),
                        pltpu.SemaphoreType.DMA((2,))],
    )
    return pl.pallas_call(
        functools.partial(_expert_body, layer=layer, chunk=256),
        grid_spec=grid_spec,
        out_shape=jax.ShapeDtypeStruct((n_rows, dp), jnp.uint32),
        compiler_params=_params("arbitrary"),
        name="moe_experts",
    )(blk_expert, blk_first, blk_next, total_rows, xs, w_gu, b_gu.reshape(e, 1, n_gu), w_dn,
      b_dn.reshape(e, 1, d), jnp.asarray(perm, BF16))


def _combine_body(pos_ref, x_ref, gw_ref, g2_ref, ys_ref, o_ref, buf, sem):
    tm = x_ref.shape[0]

    def issue(t, carry):
        for kslot in range(TOP_K):
            p = pos_ref[t * TOP_K + kslot]
            pltpu.make_async_copy(ys_ref.at[pl.ds(p, 1)], buf.at[kslot, pl.ds(t, 1)], sem.at[0]).start(
                priority=kslot % 2)
        return carry

    lax.fori_loop(0, tm, issue, 0, unroll=4)

    for kslot in range(TOP_K):
        pltpu.make_async_copy(ys_ref.at[pl.ds(0, tm)], buf.at[kslot], sem.at[0]).wait()

    gw = gw_ref[...]
    half = buf.shape[2]
    acc_lo = jnp.zeros((tm, half), F32)
    acc_hi = jnp.zeros((tm, half), F32)
    for kslot in range(TOP_K):
        lo, hi = _unpack_rows(buf[kslot])
        acc_lo = acc_lo + gw[:, kslot:kslot + 1] * lo
        acc_hi = acc_hi + gw[:, kslot:kslot + 1] * hi
    o_ref[:, 0:half] = x_ref[:, 0:half] + g2_ref[:, 0:half] * acc_lo
    o_ref[:, half:2 * half] = x_ref[:, half:2 * half] + g2_ref[:, half:2 * half] * acc_hi


def combine(pos_flat, x, gw, g2_row, ys, tm=TOK_BLOCK):
    s, d = x.shape
    return pl.pallas_call(
        _combine_body,
        grid=(s // tm,),
        in_specs=[pl.BlockSpec((tm * TOP_K,), lambda i: (i,), memory_space=pltpu.SMEM),
                  pl.BlockSpec((tm, d), lambda i: (i, 0)),
                  pl.BlockSpec((tm, LANES), lambda i: (i, 0)),
                  pl.BlockSpec((1, d), lambda i: (0, 0)),
                  pl.BlockSpec(memory_space=pl.ANY)],
        out_specs=pl.BlockSpec((tm, d), lambda i: (i, 0)),
        out_shape=jax.ShapeDtypeStruct((s, d), F32),
        scratch_shapes=[pltpu.VMEM((TOP_K, tm, d // 2), jnp.uint32), pltpu.SemaphoreType.DMA((1,))],
        compiler_params=_params("arbitrary"),
        name="moe_combine",
    )(pos_flat, x, gw, g2_row.reshape(1, d), ys)


def moe_layer(x, gain, shift, scale, g2_row, router_w, router_b, w_gu, b_gu, w_dn, b_dn, layer):
    s, d = x.shape
    e = router_w.shape[1]
    hf, eidx, gw, rk, counts = router(x, gain, shift, scale, router_w, router_b)
    cnt = counts[0, :e].astype(jnp.int32)
    padded = (cnt + MOE_ROWS - 1) // MOE_ROWS * MOE_ROWS
    padded_end = jnp.cumsum(padded)
    fill_start = (padded_end - padded).astype(jnp.int32)
    fill_len = (padded - cnt).astype(jnp.int32)
    padded_start = jnp.pad((fill_start + fill_len).astype(F32), (0, LANES - e)).reshape(1, LANES)
    n_rows = s * TOP_K + e * MOE_ROWS
    blk_first_row = jnp.arange(n_rows // MOE_ROWS, dtype=jnp.int32) * MOE_ROWS
    blk_expert = jnp.minimum(
        jnp.sum((padded_end[None, :] <= blk_first_row[:, None]).astype(jnp.int32), axis=1), e - 1)
    total_rows = padded_end[-1:].astype(jnp.int32)
    n_blk = n_rows // MOE_ROWS
    blk_id = jnp.arange(n_blk, dtype=jnp.int32)
    prev_expert = jnp.concatenate([jnp.full((1,), -1, jnp.int32), blk_expert[:-1]])
    blk_first = ((blk_first_row < total_rows[0]) & (blk_expert != prev_expert)).astype(jnp.int32)
    first_at = jnp.where(blk_first == 1, blk_id, n_blk)
    next_first = jnp.concatenate([lax.cummin(first_at, reverse=True)[1:], jnp.full((1,), n_blk, jnp.int32)])
    blk_next = jnp.where(next_first < n_blk, blk_expert[jnp.minimum(next_first, n_blk - 1)], -1).astype(jnp.int32)
    pos = slot_positions(eidx, rk, padded_start)
    pos_flat = pos[:, :TOP_K].reshape(-1)
    xs = dispatch(fill_start, fill_len, total_rows, pos_flat, hf, n_rows)
    ys = expert_ffn(blk_expert, blk_first, blk_next, total_rows, xs, w_gu, b_gu, w_dn, b_dn, layer)
    return combine(pos_flat, x, gw, g2_row, ys)


def kernel(x, c, positions, norm_gain, ada_w, ada_b, mlstm_w_in, mlstm_b_gates, mlstm_norm_gain, mlstm_w_out, rwkv_mu, rwkv_w_rkv, rwkv_w0, rwkv_w1, rwkv_w2, rwkv_a0, rwkv_a1, rwkv_a2, rwkv_g1, rwkv_g2, rwkv_k_k, rwkv_k_a, rwkv_r_k, rwkv_ln_gain, rwkv_ln_bias, rwkv_w_o, diff_w_qkv, diff_lambda, diff_subln_gain, diff_w_o, moe_router_w, moe_router_b, moe_w_gate_up, moe_b_gate_up, moe_w_down, moe_b_down, final_gain):
    b, s, d = x.shape
    assert b == 1
    depth = norm_gain.shape[0]
    xt = x.reshape(s, d)
    pos = positions.reshape(s)
    mod = ada_mod(c, ada_w, ada_b).reshape(depth, 6, d)
    seen = [0, 0, 0]
    for layer in range(depth):
        kind = layer % 3
        j = seen[kind]
        seen[kind] += 1
        sh1, sc1, g1, sh2, sc2, g2 = (mod[layer, i] for i in range(6))
        gain = norm_gain[layer, 0]
        if kind == 0:
            xt = mlstm_mixer(xt, gain, sh1, sc1, mlstm_w_in[j], mlstm_b_gates[j], mlstm_norm_gain[j],
                             mlstm_w_out[j], g1)
        elif kind == 1:
            xt = rwkv_mixer(xt, gain, sh1, sc1, rwkv_mu[j], rwkv_w_rkv[j], rwkv_w0[j], rwkv_w1[j], rwkv_w2[j],
                            rwkv_a0[j], rwkv_a1[j], rwkv_a2[j], rwkv_g1[j], rwkv_g2[j], rwkv_k_k[j],
                            rwkv_k_a[j], rwkv_r_k[j], rwkv_ln_gain[j], rwkv_ln_bias[j], rwkv_w_o[j], g1)
        else:
            xt = diff_mixer(xt, gain, sh1, sc1, pos, diff_w_qkv[j], diff_lambda[j], diff_subln_gain[j],
                            diff_w_o[j], layer, g1)
        xt = moe_layer(xt, norm_gain[layer, 1], sh2, sc2, g2, moe_router_w[layer], moe_router_b[layer],
                       moe_w_gate_up, moe_b_gate_up[layer], moe_w_down, moe_b_down[layer], layer)
    zeros = jnp.zeros((d,), F32)
    return norm_mod(xt, final_gain, zeros, zeros, F32).reshape(b, s, d)
```
